```python
import math
import jax
import jax.numpy as jnp
from jax import lax
import numpy as np

D_MODEL = 1024
BATCH = 16
SEQ = 2048
DEPTH = 4

GRID_W = 64
CTX_LEN = 256

HEAD_DIM = 64
N_Q_HEADS = 8
N_KV_HEADS = 2
D_Q = N_Q_HEADS * HEAD_DIM
D_KV = N_KV_HEADS * HEAD_DIM
WINDOW = 128
ATTN_BLOCK = 128
ROPE_BASE = 10000.0

D_SSM = D_MODEL // 4
SSM_GROUP = 16
N_SSM_GROUPS = D_SSM // SSM_GROUP
SSM_STATE = 64

D_CONV = D_MODEL // 4
CONV_WIDTH = 3

N_BRANCH = 3
IN_SIZES = (D_SSM, D_CONV, D_CONV, D_CONV, D_Q, D_KV, D_KV, N_BRANCH * D_MODEL)
D_IN = sum(IN_SIZES)

N_EXPERTS = 32
TOP_K = 4
D_EXPERT = D_MODEL
SWIGLU_LIMIT = 7.0
SWIGLU_ALPHA = 1.702
MOE_BLOCK = 256

EPS = 1e-6

kernel_name = 'hybrid_s5_conv_swa_moe_dit'


def rms_norm(x, g):
    xf = x.astype(jnp.float32)
    y = xf * lax.rsqrt(jnp.mean(xf * xf, axis=-1, keepdims=True) + EPS)
    return (y * g.astype(jnp.float32)).astype(x.dtype)


def modulate(h, shift, scale):
    return h * (1.0 + scale) + shift


def softmax_with_sink(logits, sink):
    sink_col = jnp.broadcast_to(sink, logits.shape[:-1] + (1,))
    probs = jax.nn.softmax(jnp.concatenate([logits, sink_col], axis=-1), axis=-1)
    return probs[..., :-1]


def axial_rope(n_lat):
    rows = n_lat // GRID_W
    t = jnp.arange(rows * GRID_W)
    row = (t // GRID_W).astype(jnp.float32)
    col = (t % GRID_W).astype(jnp.float32)
    n_pairs = HEAD_DIM // 4
    inv_freq = ROPE_BASE ** (-jnp.arange(n_pairs, dtype=jnp.float32) / n_pairs)
    ang = jnp.concatenate([row[:, None] * inv_freq, col[:, None] * inv_freq], axis=-1)
    return jnp.cos(ang), jnp.sin(ang)


def apply_rope(x, cos, sin):
    xf = x.astype(jnp.float32)
    x1, x2 = jnp.split(xf, 2, axis=-1)
    c = cos[None, :, None, :]
    s = sin[None, :, None, :]
    return jnp.concatenate([x1 * c - x2 * s, x1 * s + x2 * c], axis=-1).astype(x.dtype)


def _linear_recurrence(left, right):
    a_l, b_l = left
    a_r, b_r = right
    return a_l * a_r, a_r * b_l + b_r


def s5_bidirectional(u_c, u_l, lam_re, lam_im, log_dt, b_re, b_im, c_re, c_im, d_skip):
    f32 = jnp.float32
    bsz, n_ctx, _ = u_c.shape
    n_lat = u_l.shape[1]
    n_tot = n_ctx + n_lat
    lam = lax.complex(lam_re.astype(f32), lam_im.astype(f32))
    dt = jnp.exp(log_dt.astype(f32))[..., None]
    lam_bar = jnp.exp(lam * dt)
    b_bar = ((lam_bar - 1.0) / lam)[..., None] * lax.complex(b_re.astype(f32), b_im.astype(f32))
    c_mat = lax.complex(c_re.astype(f32), c_im.astype(f32))
    seq_fwd = jnp.concatenate([u_c, u_l], axis=1)
    seq_bwd = jnp.concatenate([u_c[:, ::-1], u_l[:, ::-1]], axis=1)
    u = jnp.stack([seq_fwd, seq_bwd]).astype(f32).reshape(2, bsz, n_tot, N_SSM_GROUPS, SSM_GROUP)
    bu = jnp.einsum('zbtgh,zgph->zbtgp', u.astype(jnp.complex64), b_bar)
    a = jnp.broadcast_to(lam_bar[:, None, None], (2, 1, n_tot, N_SSM_GROUPS, SSM_STATE))
    _, states = lax.associative_scan(_linear_recurrence, (a, bu), axis=2)
    y = jnp.einsum('zbtgp,zghp->zbtgh', states, c_mat).real.reshape(2, bsz, n_tot, D_SSM)
    y_c = y[0, :, :n_ctx] + y[1, :, :n_ctx][:, ::-1]
    y_l = y[0, :, n_ctx:] + y[1, :, n_ctx:][:, ::-1]
    d = d_skip.astype(f32)
    out_c = (y_c + d * u_c.astype(f32)).astype(u_c.dtype)
    out_l = (y_l + d * u_l.astype(f32)).astype(u_l.dtype)
    return out_c, out_l


def short_conv_mixer(b_gate, c_gate, x_in, conv_w):
    z = c_gate * x_in
    taps = conv_w[:, None, :].astype(z.dtype)
    pad = CONV_WIDTH // 2
    zc = lax.conv_general_dilated(z, taps, window_strides=(1,), padding=[(pad, pad)],
                                  dimension_numbers=('NWC', 'WIO', 'NWC'),
                                  feature_group_count=D_CONV)
    return b_gate * zc


def context_attention(q_c, k_c, v_c, sinks):
    bsz, n_ctx = q_c.shape[:2]
    rep = N_Q_HEADS // N_KV_HEADS
    q = q_c.reshape(bsz, n_ctx, N_KV_HEADS, rep, HEAD_DIM)
    s = jnp.einsum('blgrd,bmgd->bgrlm', q, k_c).astype(jnp.float32) * HEAD_DIM ** -0.5
    p = softmax_with_sink(s, sinks.astype(jnp.float32).reshape(1, N_KV_HEADS, rep, 1, 1))
    o = jnp.einsum('bgrlm,bmgd->blgrd', p.astype(v_c.dtype), v_c)
    return o.reshape(bsz, n_ctx, D_Q)


def windowed_latent_attention(q_l, k_l, v_l, k_c, v_c, sinks):
    bsz, n_lat = q_l.shape[:2]
    rep = N_Q_HEADS // N_KV_HEADS
    blk = ATTN_BLOCK
    nb = n_lat // blk
    scale = HEAD_DIM ** -0.5
    sink = sinks.astype(jnp.float32).reshape(1, N_KV_HEADS, rep, 1, 1)

    def band(t):
        tp = jnp.pad(t, ((0, 0), (blk, blk), (0, 0), (0, 0))).reshape(bsz, nb + 2, blk, N_KV_HEADS, HEAD_DIM)
        tb = jnp.concatenate([tp[:, :-2], tp[:, 1:-1], tp[:, 2:]], axis=2)
        return jnp.moveaxis(tb, 1, 0)

    q_blocks = jnp.moveaxis(q_l.reshape(bsz, nb, blk, N_KV_HEADS, rep, HEAD_DIM), 1, 0)
    k_band = band(k_l)
    v_band = band(v_l)
    q_pos = jnp.arange(nb)[:, None] * blk + jnp.arange(blk)[None, :]
    k_pos = (jnp.arange(nb)[:, None] - 1) * blk + jnp.arange(3 * blk)[None, :]
    valid = ((jnp.abs(q_pos[:, :, None] - k_pos[:, None, :]) <= WINDOW)
             & (k_pos[:, None, :] >= 0) & (k_pos[:, None, :] < n_lat))

    def one_block(args):
        q, k, v, m = args
        s_loc = jnp.einsum('bqgrd,bkgd->bgrqk', q, k).astype(jnp.float32) * scale
        s_loc = jnp.where(m, s_loc, -jnp.inf)
        s_ctx = jnp.einsum('bqgrd,bmgd->bgrqm', q, k_c).astype(jnp.float32) * scale
        p = softmax_with_sink(jnp.concatenate([s_loc, s_ctx], axis=-1), sink).astype(v.dtype)
        return (jnp.einsum('bgrqk,bkgd->bqgrd', p[..., :3 * blk], v)
                + jnp.einsum('bgrqm,bmgd->bqgrd', p[..., 3 * blk:], v_c))

    o = lax.map(one_block, (q_blocks, k_band, v_band, valid))
    return jnp.moveaxis(o, 0, 1).reshape(bsz, n_lat, D_Q)


def branch_merge(y_ssm, y_conv, y_attn, gate_logits, p):
    z = jax.nn.gelu(y_ssm)
    br_ssm = (z * jax.nn.sigmoid(z @ p['w_glu'])) @ p['w_ssm_out']
    br_conv = y_conv @ p['w_conv_out']
    br_attn = y_attn @ p['w_attn_out']
    g_ssm, g_conv, g_attn = jnp.split(jax.nn.sigmoid(gate_logits), N_BRANCH, axis=-1)
    return (g_ssm * br_ssm + g_conv * br_conv + g_attn * br_attn) @ p['w_o']


def hybrid_mixer(h_c, h_l, p, with_ctx_out):
    splits = np.cumsum(IN_SIZES)[:-1].tolist()
    u_c, cb_c, cc_c, cx_c, q_c, k_c, v_c, gt_c = jnp.split(h_c @ p['w_in'], splits, axis=-1)
    u_l, cb_l, cc_l, cx_l, q_l, k_l, v_l, gt_l = jnp.split(h_l @ p['w_in'], splits, axis=-1)
    n_lat = h_l.shape[1]

    ys_c, ys_l = s5_bidirectional(u_c, u_l, p['ssm_lam_re'], p['ssm_lam_im'], p['ssm_log_dt'],
                                  p['ssm_b_re'], p['ssm_b_im'], p['ssm_c_re'], p['ssm_c_im'], p['ssm_d'])

    def heads(t, n):
        return t.reshape(t.shape[:2] + (n, HEAD_DIM))
    kh_c = rms_norm(heads(k_c, N_KV_HEADS), p['k_norm_g'])
    vh_c = heads(v_c, N_KV_HEADS)
    cos, sin = axial_rope(n_lat)
    qh_l = apply_rope(rms_norm(heads(q_l, N_Q_HEADS), p['q_norm_g']), cos, sin)
    kh_l = apply_rope(rms_norm(heads(k_l, N_KV_HEADS), p['k_norm_g']), cos, sin)
    vh_l = heads(v_l, N_KV_HEADS)
    ya_l = windowed_latent_attention(qh_l, kh_l, vh_l, kh_c, vh_c, p['attn_sinks'])

    yb_l = short_conv_mixer(cb_l, cc_l, cx_l, p['conv_w'])
    out_l = branch_merge(ys_l, yb_l, ya_l, gt_l, p)

    out_c = None
    if with_ctx_out:
        qh_c = rms_norm(heads(q_c, N_Q_HEADS), p['q_norm_g'])
        ya_c = context_attention(qh_c, kh_c, vh_c, p['attn_sinks'])
        yb_c = short_conv_mixer(cb_c, cc_c, cx_c, p['conv_w'])
        out_c = branch_merge(ys_c, yb_c, ya_c, gt_c, p)
    return out_c, out_l


def moe_ffn(h, w_router, b_router, w_gu, b_gu, w_down, b_down):
    n_tok, d = h.shape
    logits = (h @ w_router).astype(jnp.float32) + b_router.astype(jnp.float32)
    top_val, top_idx = lax.top_k(logits, TOP_K)
    gates = jax.nn.softmax(top_val, axis=-1)
    n_assign = n_tok * TOP_K
    flat_e = top_idx.reshape(-1)
    order = jnp.argsort(flat_e)
    sorted_e = flat_e[order]
    sorted_tok = (order // TOP_K).astype(jnp.int32)
    counts = jnp.zeros((N_EXPERTS,), jnp.int32).at[flat_e].add(1)
    padded = (counts + MOE_BLOCK - 1) // MOE_BLOCK * MOE_BLOCK
    pad_end = jnp.cumsum(padded)
    pad_start = pad_end - padded
    start = jnp.cumsum(counts) - counts
    slot = pad_start[sorted_e] + jnp.arange(n_assign, dtype=jnp.int32) - start[sorted_e]
    n_blocks = (n_assign + N_EXPERTS * (MOE_BLOCK - 1)) // MOE_BLOCK
    n_slots = n_blocks * MOE_BLOCK
    slot_tok = jnp.full((n_slots,), n_tok, jnp.int32).at[slot].set(sorted_tok)
    slot_gate = jnp.zeros((n_slots,), jnp.float32).at[slot].set(gates.reshape(-1)[order])
    block_expert = jnp.minimum(
        jnp.searchsorted(pad_end, jnp.arange(n_blocks, dtype=jnp.int32) * MOE_BLOCK, side='right'),
        N_EXPERTS - 1)
    h_pad = jnp.concatenate([h, jnp.zeros((1, d), h.dtype)], axis=0)

    def expert_block(args):
        tok, g, e = args
        gu = h_pad[tok] @ w_gu[e] + b_gu[e]
        glu = jnp.minimum(gu[:, 0::2], SWIGLU_LIMIT)
        up = jnp.clip(gu[:, 1::2], -SWIGLU_LIMIT, SWIGLU_LIMIT)
        act = glu * jax.nn.sigmoid(SWIGLU_ALPHA * glu) * (up + 1.0)
        y = act @ w_down[e] + b_down[e]
        return y * g[:, None].astype(y.dtype)

    ys = lax.map(expert_block, (slot_tok.reshape(n_blocks, MOE_BLOCK),
                                slot_gate.reshape(n_blocks, MOE_BLOCK), block_expert))
    out = jnp.zeros((n_tok + 1, d), ys.dtype).at[slot_tok].add(ys.reshape(n_slots, d))
    return out[:n_tok]


def hybrid_layer(x_c, x_l, c, c_ctx, p, with_ctx_out):
    mod_l = (jax.nn.silu(c) @ p['w_mod'] + p['b_mod'])[:, None, :]
    mod_c = (jax.nn.silu(c_ctx) @ p['w_mod'] + p['b_mod'])[None, None, :]
    sh1_l, sc1_l, g1_l, sh2_l, sc2_l, g2_l = jnp.split(mod_l, 6, axis=-1)
    sh1_c, sc1_c, g1_c, sh2_c, sc2_c, g2_c = jnp.split(mod_c, 6, axis=-1)
    h_l = modulate(rms_norm(x_l, p['norm1_g']), sh1_l, sc1_l)
    h_c = modulate(rms_norm(x_c, p['norm1_g']), sh1_c, sc1_c)
    mix_c, mix_l = hybrid_mixer(h_c, h_l, p, with_ctx_out)
    x_l = x_l + g1_l * mix_l
    h2_l = modulate(rms_norm(x_l, p['norm2_g']), sh2_l, sc2_l)
    if not with_ctx_out:
        y_l = moe_ffn(h2_l.reshape(-1, D_MODEL), p['w_router'], p['b_router'], p['w_gate_up'],
                      p['b_gate_up'], p['w_down'], p['b_down'])
        return x_c, x_l + g2_l * y_l.reshape(x_l.shape)
    x_c = x_c + g1_c * mix_c
    h2_c = modulate(rms_norm(x_c, p['norm2_g']), sh2_c, sc2_c)
    n_c = h2_c.shape[0] * h2_c.shape[1]
    tokens = jnp.concatenate([h2_c.reshape(-1, D_MODEL), h2_l.reshape(-1, D_MODEL)], axis=0)
    y = moe_ffn(tokens, p['w_router'], p['b_router'], p['w_gate_up'], p['b_gate_up'],
                p['w_down'], p['b_down'])
    x_c = x_c + g2_c * y[:n_c].reshape(x_c.shape)
    x_l = x_l + g2_l * y[n_c:].reshape(x_l.shape)
    return x_c, x_l


def setup_inputs(seed: int = 0) -> dict:
    key = jax.random.key(seed)
    ks = jax.random.split(key, 32)
    f32 = jnp.float32
    nl, d, g, st, gh = DEPTH, D_MODEL, N_SSM_GROUPS, SSM_STATE, SSM_GROUP

    def nrm(k, shape, scale):
        return jax.random.normal(k, shape, f32) * scale

    lam_im_init = jnp.pi * jnp.arange(st, dtype=f32)
    return {
        'x': nrm(ks[0], (BATCH, SEQ, d), 1.0),
        'c': nrm(ks[1], (BATCH, d), 1.0),
        'ctx': nrm(ks[2], (BATCH, CTX_LEN, d), 1.0),
        'c_ctx': nrm(ks[3], (d,), 1.0),
        'w_mod': nrm(ks[4], (nl, d, 6 * d), 0.5 * d ** -0.5),
        'b_mod': nrm(ks[5], (nl, 6 * d), 0.02),
        'norm1_g': 1.0 + nrm(ks[6], (nl, d), 0.05),
        'norm2_g': 1.0 + nrm(ks[7], (nl, d), 0.05),
        'w_in': nrm(ks[8], (nl, d, D_IN), d ** -0.5),
        'ssm_lam_re': -0.5 + nrm(ks[9], (nl, 2, g, st), 0.01),
        'ssm_lam_im': lam_im_init + nrm(ks[10], (nl, 2, g, st), 0.01),
        'ssm_log_dt': jax.random.uniform(ks[11], (nl, 2, g), f32, math.log(1e-3), math.log(1e-1)),
        'ssm_b_re': nrm(ks[12], (nl, 2, g, st, gh), (2 * gh) ** -0.5),
        'ssm_b_im': nrm(ks[13], (nl, 2, g, st, gh), (2 * gh) ** -0.5),
        'ssm_c_re': nrm(ks[14], (nl, 2, g, gh, st), st ** -0.5),
        'ssm_c_im': nrm(ks[15], (nl, 2, g, gh, st), st ** -0.5),
        'ssm_d': nrm(ks[16], (nl, D_SSM), 1.0),
        'w_glu': nrm(ks[17], (nl, D_SSM, D_SSM), D_SSM ** -0.5),
        'w_ssm_out': nrm(ks[18], (nl, D_SSM, d), D_SSM ** -0.5),
        'conv_w': nrm(ks[19], (nl, CONV_WIDTH, D_CONV), CONV_WIDTH ** -0.5),
        'w_conv_out': nrm(ks[20], (nl, D_CONV, d), D_CONV ** -0.5),
        'q_norm_g': 1.0 + nrm(ks[21], (nl, HEAD_DIM), 0.05),
        'k_norm_g': 1.0 + nrm(ks[22], (nl, HEAD_DIM), 0.05),
        'attn_sinks': nrm(ks[23], (nl, N_Q_HEADS), 1.0),
        'w_attn_out': nrm(ks[24], (nl, D_Q, d), D_Q ** -0.5),
        'w_o': nrm(ks[25], (nl, d, d), d ** -0.5),
        'w_router': nrm(ks[26], (nl, d, N_EXPERTS), d ** -0.5),
        'b_router': nrm(ks[27], (nl, N_EXPERTS), 0.01),
        'w_gate_up': nrm(ks[28], (nl, N_EXPERTS, d, 2 * D_EXPERT), d ** -0.5),
        'b_gate_up': nrm(ks[29], (nl, N_EXPERTS, 2 * D_EXPERT), 0.01),
        'w_down': nrm(ks[30], (nl, N_EXPERTS, D_EXPERT, d), D_EXPERT ** -0.5),
        'b_down': nrm(ks[31], (nl, N_EXPERTS, d), 0.01),
    }


def reference(x, c, ctx, c_ctx, w_mod, b_mod, norm1_g, norm2_g, w_in, ssm_lam_re, ssm_lam_im,
              ssm_log_dt, ssm_b_re, ssm_b_im, ssm_c_re, ssm_c_im, ssm_d, w_glu, w_ssm_out, conv_w,
              w_conv_out, q_norm_g, k_norm_g, attn_sinks, w_attn_out, w_o, w_router, b_router,
              w_gate_up, b_gate_up, w_down, b_down):
    x_c, x_l = ctx, x
    for l in range(DEPTH):
        p = {
            'w_mod': w_mod[l], 'b_mod': b_mod[l], 'norm1_g': norm1_g[l], 'norm2_g': norm2_g[l],
            'w_in': w_in[l],
            'ssm_lam_re': ssm_lam_re[l], 'ssm_lam_im': ssm_lam_im[l], 'ssm_log_dt': ssm_log_dt[l],
            'ssm_b_re': ssm_b_re[l], 'ssm_b_im': ssm_b_im[l], 'ssm_c_re': ssm_c_re[l],
            'ssm_c_im': ssm_c_im[l], 'ssm_d': ssm_d[l], 'w_glu': w_glu[l], 'w_ssm_out': w_ssm_out[l],
            'conv_w': conv_w[l], 'w_conv_out': w_conv_out[l],
            'q_norm_g': q_norm_g[l], 'k_norm_g': k_norm_g[l], 'attn_sinks': attn_sinks[l],
            'w_attn_out': w_attn_out[l], 'w_o': w_o[l],
            'w_router': w_router[l], 'b_router': b_router[l], 'w_gate_up': w_gate_up[l],
            'b_gate_up': b_gate_up[l], 'w_down': w_down[l], 'b_down': b_down[l],
        }
        x_c, x_l = hybrid_layer(x_c, x_l, c, c_ctx, p, l < DEPTH - 1)
    return x_l
```

```python
import functools
import math

import jax
import jax.numpy as jnp
import numpy as np
from jax import lax
from jax.experimental import pallas as pl
from jax.experimental.pallas import tpu as pltpu

F32 = jnp.float32
BF16 = jnp.bfloat16
I32 = jnp.int32

D_MODEL = 1024
DEPTH = 4
N_CTX = 256
HEAD_DIM = 64
N_Q_HEADS = 8
N_KV_HEADS = 2
D_Q = N_Q_HEADS * HEAD_DIM
D_KV = N_KV_HEADS * HEAD_DIM
WINDOW = 128
ATTN_BLOCK = 128
ROPE_BASE = 10000.0
GRID_W = 64
D_SSM = 256
SSM_GROUP = 16
N_SSM_GROUPS = 16
SSM_STATE = 64
N_STATE = N_SSM_GROUPS * SSM_STATE
D_CONV = 256
N_BRANCH = 3
D_IN = D_SSM + 3 * D_CONV + D_Q + 2 * D_KV + N_BRANCH * D_MODEL
N_EXPERTS = 32
TOP_K = 4
D_EXPERT = 1024
SWIGLU_LIMIT = 7.0
SWIGLU_ALPHA = 1.702
EPS = 1e-6

TM = 256
MOE_BLOCK = 256
S5_CHUNK = 64
S5_COLS = 512
LANES = 128
NEG = -1e30
VMEM_LIMIT = 56 * 1024 * 1024


def _cparams(sem):
    return pltpu.CompilerParams(dimension_semantics=sem, vmem_limit_bytes=VMEM_LIMIT)


def _dot(a, b):
    return jnp.dot(a, b, preferred_element_type=F32)


def _sigmoid(x):
    return 1.0 / (1.0 + jnp.exp(-x))


def _rms(x, g):
    ms = jnp.mean(x * x, axis=-1, keepdims=True)
    return x * lax.rsqrt(ms + EPS) * g


def _mod_kernel(c_ref, w_ref, b_ref, o_ref):
    c = c_ref[...]
    s = (c * _sigmoid(c)).astype(BF16)
    o_ref[0] = _dot(s, w_ref[0].astype(BF16)) + b_ref[0]


def _modulation(cvec, w_mod, b_mod):
    rows = cvec.shape[0]
    nblk = 1536
    return pl.pallas_call(
        _mod_kernel,
        grid=(DEPTH, 6 * D_MODEL // nblk),
        in_specs=[
            pl.BlockSpec((rows, D_MODEL), lambda l, j: (0, 0)),
            pl.BlockSpec((1, D_MODEL, nblk), lambda l, j: (l, 0, j)),
            pl.BlockSpec((1, 1, nblk), lambda l, j: (l, 0, j)),
        ],
        out_specs=pl.BlockSpec((1, rows, nblk), lambda l, j: (l, 0, j)),
        out_shape=jax.ShapeDtypeStruct((DEPTH, rows, 6 * D_MODEL), F32),
        compiler_params=_cparams(("arbitrary", "arbitrary")),
        name="modulation",
    )(cvec, w_mod, b_mod.reshape(DEPTH, 1, 6 * D_MODEL))


def _rot_half(x, width):
    lane = lax.broadcasted_iota(I32, x.shape, 1)
    first = (lane % HEAD_DIM) < (HEAD_DIM // 2)
    return jnp.where(first, -pltpu.roll(x, width - HEAD_DIM // 2, axis=1), pltpu.roll(x, HEAD_DIM // 2, axis=1))


def _inproj_kernel(x_ref, mod_ref, g_ref, w_ref, cos_ref, sin_ref, qg_ref, kg_ref, hs_ref,
                   u_ref, cz_ref, q_ref, kv_ref, gt_ref):
    m = mod_ref[0]
    h = _rms(x_ref[...], g_ref[...])
    h = (h * (1.0 + m[:, D_MODEL:2 * D_MODEL]) + m[:, 0:D_MODEL]).astype(BF16)
    o_gate = D_SSM + 3 * D_CONV + D_Q + 2 * D_KV
    y = _dot(h, w_ref[:, 0:o_gate])
    u_ref[...] = y[:, 0:D_SSM].astype(BF16)
    cb = y[:, D_SSM:D_SSM + D_CONV]
    cc = y[:, D_SSM + D_CONV:D_SSM + 2 * D_CONV]
    cx = y[:, D_SSM + 2 * D_CONV:D_SSM + 3 * D_CONV]
    cz_ref[...] = jnp.concatenate([cb, cc * cx], axis=-1).astype(BF16)
    o_q = D_SSM + 3 * D_CONV
    q = y[:, o_q:o_q + D_Q]
    k = y[:, o_q + D_Q:o_q + D_Q + D_KV]
    v = y[:, o_q + D_Q + D_KV:o_gate]
    cos = cos_ref[...]
    sin = sin_ref[...]
    q_ms = _dot((q * q).astype(BF16), hs_ref[...])
    qn = q * lax.rsqrt(q_ms + EPS) * qg_ref[...]
    cos_q = jnp.concatenate([cos] * (D_Q // LANES), axis=-1)
    sin_q = jnp.concatenate([sin] * (D_Q // LANES), axis=-1)
    qr = qn * cos_q + _rot_half(qn, D_Q) * sin_q
    q_ref[...] = (qr * (HEAD_DIM ** -0.5)).astype(BF16)
    k_ms = _dot((k * k).astype(BF16), hs_ref[0:D_KV, 0:D_KV])
    kn = k * lax.rsqrt(k_ms + EPS) * kg_ref[...]
    kr = kn * cos + _rot_half(kn, D_KV) * sin
    kv_ref[...] = jnp.concatenate([kr, v], axis=-1).astype(BF16)
    gt = _dot(h, w_ref[:, o_gate:D_IN])
    gt_ref[...] = _sigmoid(gt).astype(BF16)


def _in_proj(x, mod_tiles, norm_g, w_in_bf, cos_t, sin_t, qg, kg, head_sum, n_batch, tiles_per_seq):
    tok = x.shape[0]
    nt = tok // TM
    t_len = tiles_per_seq * TM
    tile = lambda i: (i, 0)
    const = lambda i: (0, 0)
    seq_tile = lambda i: (i % tiles_per_seq, 0)
    return pl.pallas_call(
        _inproj_kernel,
        grid=(nt,),
        in_specs=[
            pl.BlockSpec((TM, D_MODEL), tile),
            pl.BlockSpec((1, 1, 6 * D_MODEL), lambda i: (i, 0, 0)),
            pl.BlockSpec((1, D_MODEL), const),
            pl.BlockSpec((D_MODEL, D_IN), const),
            pl.BlockSpec((TM, LANES), seq_tile),
            pl.BlockSpec((TM, LANES), seq_tile),
            pl.BlockSpec((1, D_Q), const),
            pl.BlockSpec((1, D_KV), const),
            pl.BlockSpec((D_Q, D_Q), const),
        ],
        out_specs=[
            pl.BlockSpec((TM, D_SSM), lambda i: (i % tiles_per_seq, i // tiles_per_seq)),
            pl.BlockSpec((TM, 2 * D_CONV), tile),
            pl.BlockSpec((TM, D_Q), tile),
            pl.BlockSpec((TM, 2 * D_KV), tile),
            pl.BlockSpec((TM, N_BRANCH * D_MODEL), tile),
        ],
        out_shape=[
            jax.ShapeDtypeStruct((t_len, n_batch * D_SSM), BF16),
            jax.ShapeDtypeStruct((tok, 2 * D_CONV), BF16),
            jax.ShapeDtypeStruct((tok, D_Q), BF16),
            jax.ShapeDtypeStruct((tok, 2 * D_KV), BF16),
            jax.ShapeDtypeStruct((tok, N_BRANCH * D_MODEL), BF16),
        ],
        compiler_params=_cparams(("parallel",)),
        name="in_proj",
    )(x, mod_tiles, norm_g, w_in_bf, cos_t, sin_t, qg, kg, head_sum)


def _s5_kernel(uf_ref, ub_ref, bd_ref, cd_ref, lam_ref, yf_ref, yb_ref, sf_ref, sb_ref, carry_ref, *, n_batch):
    i = pl.program_id(0)

    @pl.when(i == 0)
    def _():
        carry_ref[...] = jnp.zeros_like(carry_ref)

    for z, (u_ref, s_ref, y_ref) in enumerate(((uf_ref, sf_ref, yf_ref), (ub_ref, sb_ref, yb_ref))):
        s_ref[...] = _dot(u_ref[...], bd_ref[z])
        for j in range(N_STATE // S5_COLS):
            re_cols = pl.ds(j * S5_COLS, S5_COLS)
            im_cols = pl.ds(N_STATE + j * S5_COLS, S5_COLS)
            lr = jnp.broadcast_to(lam_ref[2 * z:2 * z + 1, re_cols], (n_batch, S5_COLS))
            li = jnp.broadcast_to(lam_ref[2 * z + 1:2 * z + 2, re_cols], (n_batch, S5_COLS))

            def step(s, c, s_ref=s_ref, z=z, re_cols=re_cols, im_cols=im_cols, lr=lr, li=li):
                t = s if z == 0 else S5_CHUNK - 1 - s
                rows = pl.ds(pl.multiple_of(t * n_batch, n_batch), n_batch)
                xr, xi = c
                nr = lr * xr - li * xi + s_ref[rows, re_cols]
                ni = lr * xi + li * xr + s_ref[rows, im_cols]
                s_ref[rows, re_cols] = nr
                s_ref[rows, im_cols] = ni
                return nr, ni

            c0 = (carry_ref[2 * z, :, re_cols], carry_ref[2 * z + 1, :, re_cols])
            fr, fi = lax.fori_loop(0, S5_CHUNK, step, c0, unroll=4)
            carry_ref[2 * z, :, re_cols] = fr
            carry_ref[2 * z + 1, :, re_cols] = fi
        y_ref[...] = _dot(s_ref[...].astype(BF16), cd_ref[z])


def _s5(u_rows, bd, cd, lam, n_batch, t_len):
    rows = S5_CHUNK * n_batch
    n_chunks = t_len // S5_CHUNK
    ctx_chunks = N_CTX // S5_CHUNK

    def bwd_block(i):
        return (jnp.where(i < ctx_chunks, ctx_chunks - 1 - i, n_chunks - 1 + ctx_chunks - i), 0)

    return pl.pallas_call(
        functools.partial(_s5_kernel, n_batch=n_batch),
        grid=(n_chunks,),
        in_specs=[
            pl.BlockSpec((rows, D_SSM), lambda i: (i, 0)),
            pl.BlockSpec((rows, D_SSM), bwd_block),
            pl.BlockSpec((2, D_SSM, 2 * N_STATE), lambda i: (0, 0, 0)),
            pl.BlockSpec((2, 2 * N_STATE, D_SSM), lambda i: (0, 0, 0)),
            pl.BlockSpec((4, N_STATE), lambda i: (0, 0)),
        ],
        out_specs=[
            pl.BlockSpec((rows, D_SSM), lambda i: (i, 0)),
            pl.BlockSpec((rows, D_SSM), bwd_block),
        ],
        out_shape=[jax.ShapeDtypeStruct((t_len * n_batch, D_SSM), F32)] * 2,
        scratch_shapes=[
            pltpu.VMEM((rows, 2 * N_STATE), F32),
            pltpu.VMEM((rows, 2 * N_STATE), F32),
            pltpu.VMEM((4, n_batch, N_STATE), F32),
        ],
        compiler_params=_cparams(("arbitrary",)),
        name="s5_scan",
    )(u_rows, u_rows, bd, cd, lam)


def _attn_kernel(sink_ref, q_ref, kvc_ref, kv0_ref, kv1_ref, kv2_ref, o_ref, *, n_lat):
    n = pl.program_id(1) - N_CTX // ATTN_BLOCK
    rep = N_Q_HEADS // N_KV_HEADS
    rows = rep * ATTN_BLOCK
    q = q_ref[0]
    kvc = kvc_ref[0]
    band = jnp.concatenate([kv0_ref[0], kv1_ref[0], kv2_ref[0]], axis=0)
    iq = lax.broadcasted_iota(I32, (rows, 3 * ATTN_BLOCK), 0) % ATTN_BLOCK
    ik = lax.broadcasted_iota(I32, (rows, 3 * ATTN_BLOCK), 1)
    qpos = n * ATTN_BLOCK + iq
    kpos = (n - 1) * ATTN_BLOCK + ik
    valid = (jnp.abs(qpos - kpos) <= WINDOW) & (kpos >= 0) & (kpos < n_lat) & (n >= 0)
    head_of_row = lax.broadcasted_iota(I32, (rows, 1), 0) // ATTN_BLOCK
    contract_last = (((1,), (1,)), ((), ()))
    outs = []
    for g in range(N_KV_HEADS):
        qg = jnp.concatenate([q[:, (g * rep + r) * HEAD_DIM:(g * rep + r + 1) * HEAD_DIM] for r in range(rep)], axis=0)
        sink = jnp.zeros((rows, 1), F32)
        for r in range(rep):
            sink = jnp.where(head_of_row == r, sink_ref[g * rep + r], sink)
        kc = kvc[:, g * HEAD_DIM:(g + 1) * HEAD_DIM]
        vc = kvc[:, D_KV + g * HEAD_DIM:D_KV + (g + 1) * HEAD_DIM]
        kb = band[:, g * HEAD_DIM:(g + 1) * HEAD_DIM]
        vb = band[:, D_KV + g * HEAD_DIM:D_KV + (g + 1) * HEAD_DIM]
        sc = lax.dot_general(qg, kc, contract_last, preferred_element_type=F32)
        sb = lax.dot_general(qg, kb, contract_last, preferred_element_type=F32)
        sb = jnp.where(valid, sb, NEG)
        mx = jnp.maximum(jnp.maximum(jnp.max(sc, axis=-1, keepdims=True), jnp.max(sb, axis=-1, keepdims=True)), sink)
        pc = jnp.exp(sc - mx)
        pb = jnp.exp(sb - mx)
        den = jnp.sum(pc, axis=-1, keepdims=True) + jnp.sum(pb, axis=-1, keepdims=True) + jnp.exp(sink - mx)
        o = (_dot(pc.astype(BF16), vc) + _dot(pb.astype(BF16), vb)) / den
        outs.extend(o[r * ATTN_BLOCK:(r + 1) * ATTN_BLOCK] for r in range(rep))
    o_ref[0] = jnp.concatenate(outs, axis=-1).astype(BF16)


def _attention(q, kv, sinks, n_batch, t_len):
    nqb = t_len // ATTN_BLOCK
    first = N_CTX // ATTN_BLOCK
    q3 = q.reshape(n_batch, t_len, D_Q)
    kv3 = kv.reshape(n_batch, t_len, 2 * D_KV)

    def band(off):
        return lambda b, j, s: (b, jnp.clip(j + off, first, nqb - 1), 0)

    out = pl.pallas_call(
        functools.partial(_attn_kernel, n_lat=t_len - N_CTX),
        grid_spec=pltpu.PrefetchScalarGridSpec(
            num_scalar_prefetch=1,
            grid=(n_batch, nqb),
            in_specs=[
                pl.BlockSpec((1, ATTN_BLOCK, D_Q), lambda b, j, s: (b, j, 0)),
                pl.BlockSpec((1, N_CTX, 2 * D_KV), lambda b, j, s: (b, 0, 0)),
                pl.BlockSpec((1, ATTN_BLOCK, 2 * D_KV), band(-1)),
                pl.BlockSpec((1, ATTN_BLOCK, 2 * D_KV), band(0)),
                pl.BlockSpec((1, ATTN_BLOCK, 2 * D_KV), band(1)),
            ],
            out_specs=pl.BlockSpec((1, ATTN_BLOCK, D_Q), lambda b, j, s: (b, j, 0)),
        ),
        out_shape=jax.ShapeDtypeStruct((n_batch, t_len, D_Q), BF16),
        compiler_params=_cparams(("parallel", "parallel")),
        name="attention",
    )(sinks, q3, kv3, kv3, kv3, kv3)
    return out.reshape(n_batch * t_len, D_Q)


def _gelu_tanh(x):
    return 0.5 * x * (1.0 + jnp.tanh(math.sqrt(2.0 / math.pi) * (x + 0.044715 * (x * x * x))))


def _merge_kernel(x_ref, yf_ref, yb_ref, u_ref, cz_ref, czp_ref, czn_ref, ya_ref, gt_ref, mod_ref,
                  d_ref, cw_ref, wglu_ref, wso_ref, wco_ref, wao_ref, wo_ref, n2g_ref,
                  wrh_ref, wrl_ref, br_ref, xo_ref, h2_ref, lg_ref, *, tiles_per_seq):
    r = pl.program_id(0) % tiles_per_seq
    ys = yf_ref[...] + yb_ref[...] + d_ref[...] * u_ref[...].astype(F32)
    z = _gelu_tanh(ys)
    glu = z * _sigmoid(_dot(z.astype(BF16), wglu_ref[...]))
    br_ssm = _dot(glu.astype(BF16), wso_ref[...])

    cz = cz_ref[...].astype(F32)
    cb = cz[:, 0:D_CONV]
    zz = cz[:, D_CONV:2 * D_CONV]
    seg_first = r <= 1
    seg_last = (r == 0) | (r == tiles_per_seq - 1)
    prev_row = jnp.where(seg_first, 0.0, czp_ref[7:8, D_CONV:2 * D_CONV].astype(F32))
    next_row = jnp.where(seg_last, 0.0, czn_ref[0:1, D_CONV:2 * D_CONV].astype(F32))
    row = lax.broadcasted_iota(I32, (TM, D_CONV), 0)
    z_dn = jnp.where(row == 0, prev_row, pltpu.roll(zz, 1, axis=0))
    z_up = jnp.where(row == TM - 1, next_row, pltpu.roll(zz, TM - 1, axis=0))
    y_conv = cb * (cw_ref[0:1, :] * z_dn + cw_ref[1:2, :] * zz + cw_ref[2:3, :] * z_up)
    br_conv = _dot(y_conv.astype(BF16), wco_ref[...])
    br_attn = _dot(ya_ref[...], wao_ref[...])

    merged = (gt_ref[:, 0:D_MODEL].astype(F32) * br_ssm
              + gt_ref[:, D_MODEL:2 * D_MODEL].astype(F32) * br_conv
              + gt_ref[:, 2 * D_MODEL:3 * D_MODEL].astype(F32) * br_attn)
    mix = _dot(merged.astype(BF16), wo_ref[...])
    m = mod_ref[0]
    xn = x_ref[...] + m[:, 2 * D_MODEL:3 * D_MODEL] * mix
    xo_ref[...] = xn
    h2 = _rms(xn, n2g_ref[...]) * (1.0 + m[:, 4 * D_MODEL:5 * D_MODEL]) + m[:, 3 * D_MODEL:4 * D_MODEL]
    h2_ref[...] = h2
    hi = h2.astype(BF16)
    lo = (h2 - hi.astype(F32)).astype(BF16)
    lg_ref[...] = _dot(hi, wrh_ref[...]) + _dot(lo, wrh_ref[...]) + _dot(hi, wrl_ref[...]) + br_ref[...]


def _merge(x, yf, yb, u_tb, cz, ya, gt, mod_tiles, d_skip, conv_w, wglu, wso, wco, wao, wo, n2g,
           wr_hi, wr_lo, b_r, tiles_per_seq):
    tok = x.shape[0]
    nt = tok // TM
    tile = lambda i: (i, 0)
    const = lambda i: (0, 0)
    tb = lambda i: (i % tiles_per_seq, i // tiles_per_seq)
    rows8 = TM // 8
    return pl.pallas_call(
        functools.partial(_merge_kernel, tiles_per_seq=tiles_per_seq),
        grid=(nt,),
        in_specs=[
            pl.BlockSpec((TM, D_MODEL), tile),
            pl.BlockSpec((TM, D_SSM), tb),
            pl.BlockSpec((TM, D_SSM), tb),
            pl.BlockSpec((TM, D_SSM), tb),
            pl.BlockSpec((TM, 2 * D_CONV), tile),
            pl.BlockSpec((8, 2 * D_CONV), lambda i: (jnp.maximum(i * rows8 - 1, 0), 0)),
            pl.BlockSpec((8, 2 * D_CONV), lambda i: (jnp.minimum((i + 1) * rows8, tok // 8 - 1), 0)),
            pl.BlockSpec((TM, D_Q), tile),
            pl.BlockSpec((TM, N_BRANCH * D_MODEL), tile),
            pl.BlockSpec((1, 1, 6 * D_MODEL), lambda i: (i, 0, 0)),
            pl.BlockSpec((1, D_SSM), const),
            pl.BlockSpec((3, D_CONV), const),
            pl.BlockSpec((D_SSM, D_SSM), const),
            pl.BlockSpec((D_SSM, D_MODEL), const),
            pl.BlockSpec((D_CONV, D_MODEL), const),
            pl.BlockSpec((D_Q, D_MODEL), const),
            pl.BlockSpec((D_MODEL, D_MODEL), const),
            pl.BlockSpec((1, D_MODEL), const),
            pl.BlockSpec((D_MODEL, LANES), const),
            pl.BlockSpec((D_MODEL, LANES), const),
            pl.BlockSpec((1, LANES), const),
        ],
        out_specs=[
            pl.BlockSpec((TM, D_MODEL), tile),
            pl.BlockSpec((TM, D_MODEL), tile),
            pl.BlockSpec((TM, LANES), tile),
        ],
        out_shape=[
            jax.ShapeDtypeStruct((tok, D_MODEL), F32),
            jax.ShapeDtypeStruct((tok, D_MODEL), F32),
            jax.ShapeDtypeStruct((tok, LANES), F32),
        ],
        compiler_params=_cparams(("parallel",)),
        name="branch_merge",
    )(x, yf, yb, u_tb, cz, cz, cz, ya, gt, mod_tiles, d_skip, conv_w, wglu, wso, wco, wao, wo, n2g,
      wr_hi, wr_lo, b_r)


def _route_kernel(lg_ref, tri_ref, out_ref, cnt_ref, carry_ref):
    @pl.when(pl.program_id(0) == 0)
    def _():
        carry_ref[...] = jnp.zeros_like(carry_ref)

    l = lg_ref[...]
    lane = lax.broadcasted_iota(I32, l.shape, 1)
    vals, idxs, hots = [], [], []
    for _ in range(TOP_K):
        mx = jnp.max(l, axis=-1, keepdims=True)
        idx = jnp.min(jnp.where(l == mx, lane, LANES), axis=-1, keepdims=True)
        hot = lane == idx
        l = jnp.where(hot, -3e38, l)
        vals.append(mx)
        idxs.append(idx)
        hots.append(hot)
    ex = [jnp.exp(v - vals[0]) for v in vals]
    den = ex[0] + ex[1] + ex[2] + ex[3]
    picked = jnp.zeros(l.shape, F32)
    for hot in hots:
        picked = picked + hot.astype(F32)
    cum = _dot(tri_ref[...], picked.astype(BF16)) + carry_ref[...]
    out = jnp.zeros(l.shape, F32)
    for k in range(TOP_K):
        rank = jnp.sum(jnp.where(hots[k], cum, 0.0), axis=-1, keepdims=True)
        out = jnp.where(lane == k, idxs[k].astype(F32), out)
        out = jnp.where(lane == TOP_K + k, rank, out)
        out = jnp.where(lane == 2 * TOP_K + k, ex[k] / den, out)
    out_ref[...] = out
    carry_ref[...] = carry_ref[...] + jnp.sum(picked, axis=0, keepdims=True)
    cnt_ref[...] = carry_ref[...]


def _route(logits, tri):
    tok = logits.shape[0]
    return pl.pallas_call(
        _route_kernel,
        grid=(tok // TM,),
        in_specs=[pl.BlockSpec((TM, LANES), lambda i: (i, 0)), pl.BlockSpec((TM, TM), lambda i: (0, 0))],
        out_specs=[pl.BlockSpec((TM, LANES), lambda i: (i, 0)), pl.BlockSpec((1, LANES), lambda i: (0, 0))],
        out_shape=[jax.ShapeDtypeStruct((tok, LANES), F32), jax.ShapeDtypeStruct((1, LANES), F32)],
        scratch_shapes=[pltpu.VMEM((1, LANES), F32)],
        compiler_params=_cparams(("arbitrary",)),
        name="route",
    )(logits, tri)


def _dispatch_kernel(zstart_ref, nu_ref, slot_hbm, h_ref, xs_hbm, slot_smem, zero_ref, sem_i, sem_z, sem_r,
                     *, n_blocks):
    i = pl.program_id(0)
    n_idx = TM * TOP_K

    @pl.when(i == 0)
    def _():
        zero_ref[...] = jnp.zeros_like(zero_ref)

        def zero_copy(start):
            rows = pl.ds(pl.multiple_of(start, MOE_BLOCK), MOE_BLOCK)
            return pltpu.make_async_copy(zero_ref, xs_hbm.at[rows, :], sem_z)

        def tail_start(b, c):
            zero_copy(b * MOE_BLOCK).start()
            return c

        def tail_wait(b, c):
            zero_copy(b * MOE_BLOCK).wait()
            return c

        for e in range(N_EXPERTS):
            @pl.when(zstart_ref[e] >= 0)
            def _():
                zero_copy(zstart_ref[e]).start()
        lax.fori_loop(nu_ref[0], n_blocks, tail_start, 0)
        for e in range(N_EXPERTS):
            @pl.when(zstart_ref[e] >= 0)
            def _():
                zero_copy(zstart_ref[e]).wait()
        lax.fori_loop(nu_ref[0], n_blocks, tail_wait, 0)

    cp = pltpu.make_async_copy(slot_hbm.at[pl.ds(pl.multiple_of(i * n_idx, n_idx), n_idx)], slot_smem, sem_i)
    cp.start()
    cp.wait()

    def row_copy(j):
        return pltpu.make_async_copy(h_ref.at[pl.ds(j // TOP_K, 1), :], xs_hbm.at[pl.ds(slot_smem[j], 1), :], sem_r)

    def issue(j, c):
        row_copy(j).start()
        return c

    def drain(j, c):
        row_copy(j).wait()
        return c

    lax.fori_loop(0, n_idx, issue, 0, unroll=8)
    lax.fori_loop(0, n_idx, drain, 0, unroll=8)


def _dispatch(h2, slots, zstart, n_used, n_slots):
    tok = h2.shape[0]
    return pl.pallas_call(
        functools.partial(_dispatch_kernel, n_blocks=n_slots // MOE_BLOCK),
        grid_spec=pltpu.PrefetchScalarGridSpec(
            num_scalar_prefetch=2,
            grid=(tok // TM,),
            in_specs=[
                pl.BlockSpec(memory_space=pl.ANY),
                pl.BlockSpec((TM, D_MODEL), lambda i, z, nu: (i, 0)),
            ],
            out_specs=pl.BlockSpec(memory_space=pl.ANY),
            scratch_shapes=[
                pltpu.SMEM((TM * TOP_K,), I32),
                pltpu.VMEM((MOE_BLOCK, D_MODEL), F32),
                pltpu.SemaphoreType.DMA,
                pltpu.SemaphoreType.DMA,
                pltpu.SemaphoreType.DMA,
            ],
        ),
        out_shape=jax.ShapeDtypeStruct((n_slots, D_MODEL), F32),
        compiler_params=_cparams(("arbitrary",)),
        name="dispatch",
    )(zstart, n_used, slots, h2)


def _expert_kernel(be_ref, nu_ref, x_ref, wg_ref, wu_ref, bg_ref, bu_ref, wd_ref, bd_ref, y_ref):
    i = pl.program_id(0)

    @pl.when(i < nu_ref[0])
    def _():
        x = x_ref[...].astype(BF16)
        g = _dot(x, wg_ref[0]) + bg_ref[0]
        u = _dot(x, wu_ref[0]) + bu_ref[0]
        glu = jnp.minimum(g, SWIGLU_LIMIT)
        up = jnp.clip(u, -SWIGLU_LIMIT, SWIGLU_LIMIT)
        act = glu * _sigmoid(SWIGLU_ALPHA * glu) * (up + 1.0)
        y_ref[...] = _dot(act.astype(BF16), wd_ref[0]) + bd_ref[0]

    @pl.when(i >= nu_ref[0])
    def _():
        y_ref[...] = jnp.zeros_like(y_ref)


def _experts(xs, block_expert, n_used, wg, wu, bg, bu, wd, bd):
    n_slots = xs.shape[0]
    n_blocks = n_slots // MOE_BLOCK
    wmap = lambda i, be, nu: (be[i], 0, 0)
    return pl.pallas_call(
        _expert_kernel,
        grid_spec=pltpu.PrefetchScalarGridSpec(
            num_scalar_prefetch=2,
            grid=(n_blocks,),
            in_specs=[
                pl.BlockSpec((MOE_BLOCK, D_MODEL), lambda i, be, nu: (jnp.minimum(i, nu[0] - 1), 0)),
                pl.BlockSpec((1, D_MODEL, D_EXPERT), wmap),
                pl.BlockSpec((1, D_MODEL, D_EXPERT), wmap),
                pl.BlockSpec((1, 1, D_EXPERT), wmap),
                pl.BlockSpec((1, 1, D_EXPERT), wmap),
                pl.BlockSpec((1, D_EXPERT, D_MODEL), wmap),
                pl.BlockSpec((1, 1, D_MODEL), wmap),
            ],
            out_specs=pl.BlockSpec((MOE_BLOCK, D_MODEL), lambda i, be, nu: (i, 0)),
        ),
        out_shape=jax.ShapeDtypeStruct((n_slots, D_MODEL), F32),
        compiler_params=_cparams(("arbitrary",)),
        name="experts",
    )(block_expert, n_used, xs, wg, wu, bg, bu, wd, bd)


def _combine_kernel(slot_hbm, ys_hbm, x_ref, rt_ref, mod_ref, o_ref, slot_smem, buf_ref, sem_i, sem_r):
    i = pl.program_id(0)
    n_idx = TM * TOP_K
    cp = pltpu.make_async_copy(slot_hbm.at[pl.ds(pl.multiple_of(i * n_idx, n_idx), n_idx)], slot_smem, sem_i)
    cp.start()
    cp.wait()

    def row_copy(j):
        return pltpu.make_async_copy(ys_hbm.at[pl.ds(slot_smem[j], 1), :],
                                     buf_ref.at[j % TOP_K, pl.ds(j // TOP_K, 1), :], sem_r)

    def issue(j, c):
        row_copy(j).start()
        return c

    def drain(j, c):
        row_copy(j).wait()
        return c

    lax.fori_loop(0, n_idx, issue, 0, unroll=8)
    lax.fori_loop(0, n_idx, drain, 0, unroll=8)
    rt = rt_ref[...]
    y = rt[:, 2 * TOP_K:2 * TOP_K + 1] * buf_ref[0]
    for k in range(1, TOP_K):
        y = y + rt[:, 2 * TOP_K + k:2 * TOP_K + k + 1] * buf_ref[k]
    o_ref[...] = x_ref[...] + mod_ref[0][:, 5 * D_MODEL:6 * D_MODEL] * y


def _combine(x, ys, slots, route, mod_tiles):
    tok = x.shape[0]
    return pl.pallas_call(
        _combine_kernel,
        grid=(tok // TM,),
        in_specs=[
            pl.BlockSpec(memory_space=pl.ANY),
            pl.BlockSpec(memory_space=pl.ANY),
            pl.BlockSpec((TM, D_MODEL), lambda i: (i, 0)),
            pl.BlockSpec((TM, LANES), lambda i: (i, 0)),
            pl.BlockSpec((1, 1, 6 * D_MODEL), lambda i: (i, 0, 0)),
        ],
        out_specs=pl.BlockSpec((TM, D_MODEL), lambda i: (i, 0)),
        out_shape=jax.ShapeDtypeStruct((tok, D_MODEL), F32),
        scratch_shapes=[
            pltpu.SMEM((TM * TOP_K,), I32),
            pltpu.VMEM((TOP_K, TM, D_MODEL), F32),
            pltpu.SemaphoreType.DMA,
            pltpu.SemaphoreType.DMA,
        ],
        compiler_params=_cparams(("arbitrary",)),
        name="combine",
    )(slots, ys, x, route, mod_tiles)


def _rope_tables(t_len):
    n_lat = t_len - N_CTX
    t = np.arange(n_lat)
    n_pairs = HEAD_DIM // 4
    inv_freq = jnp.asarray(ROPE_BASE, F32) ** (-jnp.arange(n_pairs, dtype=F32) / n_pairs)
    row = jnp.asarray(t // GRID_W, F32)
    col = jnp.asarray(t % GRID_W, F32)
    ang = jnp.concatenate([row[:, None] * inv_freq, col[:, None] * inv_freq], axis=-1)
    ang = jnp.concatenate([jnp.zeros((N_CTX, HEAD_DIM // 2), F32), ang], axis=0)
    cos = jnp.tile(jnp.cos(ang), (1, LANES // (HEAD_DIM // 2)))
    sin = jnp.tile(jnp.sin(ang), (1, LANES // (HEAD_DIM // 2)))
    return cos, sin


def _s5_operands(lam_re, lam_im, log_dt, b_re, b_im, c_re, c_im):
    lam = lax.complex(lam_re, lam_im)
    dt = jnp.exp(log_dt)[..., None]
    lam_bar = jnp.exp(lam * dt)
    b_bar = ((lam_bar - 1.0) / lam)[..., None] * lax.complex(b_re, b_im)
    eye = jnp.eye(N_SSM_GROUPS, dtype=F32)

    def in_block(w):
        return jnp.einsum("zgph,gk->zghkp", w, eye).reshape(2, D_SSM, N_STATE)

    def out_block(w):
        return jnp.einsum("zghp,gk->zgpkh", w, eye).reshape(2, N_STATE, D_SSM)

    bd = jnp.concatenate([in_block(b_bar.real), in_block(b_bar.imag)], axis=-1).astype(BF16)
    cd = jnp.concatenate([out_block(c_re), out_block(-c_im)], axis=1).astype(BF16)
    lam_rows = jnp.stack([lam_bar[0].real.reshape(-1), lam_bar[0].imag.reshape(-1),
                          lam_bar[1].real.reshape(-1), lam_bar[1].imag.reshape(-1)])
    return bd, cd, lam_rows


def kernel(x, c, ctx, c_ctx, w_mod, b_mod, norm1_g, norm2_g, w_in, ssm_lam_re, ssm_lam_im, ssm_log_dt, ssm_b_re, ssm_b_im, ssm_c_re, ssm_c_im, ssm_d, w_glu, w_ssm_out, conv_w, w_conv_out, q_norm_g, k_norm_g, attn_sinks, w_attn_out, w_o, w_router, b_router, w_gate_up, b_gate_up, w_down, b_down):
    n_batch, n_lat, _ = x.shape
    t_len = N_CTX + n_lat
    tiles_per_seq = t_len // TM
    tok = n_batch * t_len
    nt = tok // TM
    n_assign = tok * TOP_K
    n_blocks = (n_assign + N_EXPERTS * (MOE_BLOCK - 1)) // MOE_BLOCK
    n_slots = n_blocks * MOE_BLOCK

    xs = jnp.concatenate([ctx, x], axis=1).reshape(tok, D_MODEL)

    mod_rows = 8 * ((n_batch + 1 + 7) // 8)
    cvec = jnp.zeros((mod_rows, D_MODEL), F32).at[:n_batch].set(c).at[n_batch].set(c_ctx)
    mod_all = _modulation(cvec, w_mod, b_mod)
    tile_ids = np.arange(nt)
    tile_row = np.where(tile_ids % tiles_per_seq == 0, n_batch, tile_ids // tiles_per_seq)

    cos_t, sin_t = _rope_tables(t_len)
    head_sum = jnp.asarray(np.kron(np.eye(N_Q_HEADS), np.full((HEAD_DIM, HEAD_DIM), 1.0 / HEAD_DIM)), BF16)
    tri = jnp.asarray(np.tril(np.ones((TM, TM)), -1), BF16)

    for l in range(DEPTH):
        mod_tiles = mod_all[l][tile_row].reshape(nt, 1, 6 * D_MODEL)
        qg = jnp.tile(q_norm_g[l], N_Q_HEADS).reshape(1, D_Q)
        kg = jnp.tile(k_norm_g[l], N_KV_HEADS).reshape(1, D_KV)
        u_tb, cz, q, kv, gt = _in_proj(xs, mod_tiles, norm1_g[l].reshape(1, D_MODEL), w_in[l].astype(BF16),
                                       cos_t, sin_t, qg, kg, head_sum, n_batch, tiles_per_seq)

        bd, cd, lam_rows = _s5_operands(ssm_lam_re[l], ssm_lam_im[l], ssm_log_dt[l], ssm_b_re[l], ssm_b_im[l],
                                        ssm_c_re[l], ssm_c_im[l])
        yf, yb = _s5(u_tb.reshape(t_len * n_batch, D_SSM), bd, cd, lam_rows, n_batch, t_len)
        ya = _attention(q, kv, attn_sinks[l], n_batch, t_len)

        wr = jnp.zeros((D_MODEL, LANES), F32).at[:, :N_EXPERTS].set(w_router[l])
        wr_hi = wr.astype(BF16)
        wr_lo = (wr - wr_hi.astype(F32)).astype(BF16)
        b_r = jnp.full((1, LANES), NEG, F32).at[0, :N_EXPERTS].set(b_router[l])
        xs, h2, logits = _merge(xs, yf.reshape(t_len, n_batch * D_SSM), yb.reshape(t_len, n_batch * D_SSM), u_tb,
                                cz, ya, gt, mod_tiles, ssm_d[l].reshape(1, D_SSM), conv_w[l],
                                w_glu[l].astype(BF16), w_ssm_out[l].astype(BF16), w_conv_out[l].astype(BF16),
                                w_attn_out[l].astype(BF16), w_o[l].astype(BF16), norm2_g[l].reshape(1, D_MODEL),
                                wr_hi, wr_lo, b_r, tiles_per_seq)

        route, counts = _route(logits, tri)
        counts = counts[0, :N_EXPERTS].astype(I32)
        padded = (counts + MOE_BLOCK - 1) // MOE_BLOCK * MOE_BLOCK
        pad_end = jnp.cumsum(padded)
        pad_start = pad_end - padded
        sel = route[:, 0:TOP_K].astype(I32)
        rank = route[:, TOP_K:2 * TOP_K].astype(I32)
        slots = (pad_start[sel] + rank).reshape(n_assign)
        n_used = (pad_end[-1] // MOE_BLOCK).astype(I32)
        blk = jnp.minimum(jnp.arange(n_blocks, dtype=I32), n_used - 1) * MOE_BLOCK
        block_expert = jnp.minimum(jnp.searchsorted(pad_end, blk, side="right"), N_EXPERTS - 1).astype(I32)
        zstart = jnp.where(counts > 0, pad_end - MOE_BLOCK, -1).astype(I32)

        x_sorted = _dispatch(h2, slots, zstart, n_used.reshape(1), n_slots)
        wgu = w_gate_up[l]
        y_sorted = _experts(x_sorted, block_expert, n_used.reshape(1),
                            wgu[:, :, 0::2].astype(BF16), wgu[:, :, 1::2].astype(BF16),
                            b_gate_up[l][:, None, 0::2], b_gate_up[l][:, None, 1::2],
                            w_down[l].astype(BF16), b_down[l][:, None, :])
        xs = _combine(xs, y_sorted, slots, route, mod_tiles)

    return xs.reshape(n_batch, t_len, D_MODEL)[:, N_CTX:, :]
```

```python
import functools
import math

import jax
import jax.numpy as jnp
import numpy as np
from jax import lax
from jax.experimental import pallas as pl
from jax.experimental.pallas import tpu as pltpu

F32 = jnp.float32
BF16 = jnp.bfloat16
I32 = jnp.int32

D_MODEL = 1024
DEPTH = 4
N_CTX = 256
HEAD_DIM = 64
N_Q_HEADS = 8
N_KV_HEADS = 2
D_Q = N_Q_HEADS * HEAD_DIM
D_KV = N_KV_HEADS * HEAD_DIM
WINDOW = 128
ATTN_BLOCK = 128
ROPE_BASE = 10000.0
GRID_W = 64
D_SSM = 256
SSM_GROUP = 16
N_SSM_GROUPS = 16
SSM_STATE = 64
N_STATE = N_SSM_GROUPS * SSM_STATE
D_CONV = 256
N_BRANCH = 3
D_IN = D_SSM + 3 * D_CONV + D_Q + 2 * D_KV + N_BRANCH * D_MODEL
N_EXPERTS = 32
TOP_K = 4
D_EXPERT = 1024
SWIGLU_LIMIT = 7.0
SWIGLU_ALPHA = 1.702
EPS = 1e-6

TM = 256
MOE_BLOCK = 256
S5_CHUNK = 64
S5_COLS = 512
LANES = 128
NEG = -1e30
VMEM_LIMIT = 56 * 1024 * 1024


def _cparams(sem):
    return pltpu.CompilerParams(dimension_semantics=sem, vmem_limit_bytes=VMEM_LIMIT)


def _dot(a, b):
    return jnp.dot(a, b, preferred_element_type=F32)


def _sigmoid(x):
    return 1.0 / (1.0 + jnp.exp(-x))


def _rms(x, g):
    ms = jnp.mean(x * x, axis=-1, keepdims=True)
    return x * lax.rsqrt(ms + EPS) * g


def _mod_kernel(c_ref, w_ref, b_ref, o_ref):
    c = c_ref[...]
    s = (c * _sigmoid(c)).astype(BF16)
    o_ref[0] = _dot(s, w_ref[0].astype(BF16)) + b_ref[0]


def _modulation(cvec, w_mod, b_mod):
    rows = cvec.shape[0]
    nblk = 1536
    return pl.pallas_call(
        _mod_kernel,
        grid=(DEPTH, 6 * D_MODEL // nblk),
        in_specs=[
            pl.BlockSpec((rows, D_MODEL), lambda l, j: (0, 0)),
            pl.BlockSpec((1, D_MODEL, nblk), lambda l, j: (l, 0, j)),
            pl.BlockSpec((1, 1, nblk), lambda l, j: (l, 0, j)),
        ],
        out_specs=pl.BlockSpec((1, rows, nblk), lambda l, j: (l, 0, j)),
        out_shape=jax.ShapeDtypeStruct((DEPTH, rows, 6 * D_MODEL), F32),
        compiler_params=_cparams(("arbitrary", "arbitrary")),
        name="modulation",
    )(cvec, w_mod, b_mod.reshape(DEPTH, 1, 6 * D_MODEL))


def _rot_half(x, width):
    lane = lax.broadcasted_iota(I32, x.shape, 1)
    first = (lane % HEAD_DIM) < (HEAD_DIM // 2)
    return jnp.where(first, -pltpu.roll(x, width - HEAD_DIM // 2, axis=1), pltpu.roll(x, HEAD_DIM // 2, axis=1))


def _inproj_kernel(x_ref, mod_ref, g_ref, w_ref, cos_ref, sin_ref, qg_ref, kg_ref, hs_ref,
                   u_ref, cz_ref, q_ref, kv_ref, gt_ref):
    m = mod_ref[0]
    h = _rms(x_ref[...], g_ref[...])
    h = (h * (1.0 + m[:, D_MODEL:2 * D_MODEL]) + m[:, 0:D_MODEL]).astype(BF16)
    o_gate = D_SSM + 3 * D_CONV + D_Q + 2 * D_KV
    y = _dot(h, w_ref[:, 0:o_gate])
    u_ref[...] = y[:, 0:D_SSM].astype(BF16)
    cb = y[:, D_SSM:D_SSM + D_CONV]
    cc = y[:, D_SSM + D_CONV:D_SSM + 2 * D_CONV]
    cx = y[:, D_SSM + 2 * D_CONV:D_SSM + 3 * D_CONV]
    cz_ref[...] = jnp.concatenate([cb, cc * cx], axis=-1).astype(BF16)
    o_q = D_SSM + 3 * D_CONV
    q = y[:, o_q:o_q + D_Q]
    k = y[:, o_q + D_Q:o_q + D_Q + D_KV]
    v = y[:, o_q + D_Q + D_KV:o_gate]
    cos = cos_ref[...]
    sin = sin_ref[...]
    q_ms = _dot((q * q).astype(BF16), hs_ref[...])
    qn = q * lax.rsqrt(q_ms + EPS) * qg_ref[...]
    cos_q = jnp.concatenate([cos] * (D_Q // LANES), axis=-1)
    sin_q = jnp.concatenate([sin] * (D_Q // LANES), axis=-1)
    qr = qn * cos_q + _rot_half(qn, D_Q) * sin_q
    q_ref[...] = (qr * (HEAD_DIM ** -0.5)).astype(BF16)
    k_ms = _dot((k * k).astype(BF16), hs_ref[0:D_KV, 0:D_KV])
    kn = k * lax.rsqrt(k_ms + EPS) * kg_ref[...]
    kr = kn * cos + _rot_half(kn, D_KV) * sin
    kv_ref[...] = jnp.concatenate([kr, v], axis=-1).astype(BF16)
    gt = _dot(h, w_ref[:, o_gate:D_IN])
    gt_ref[...] = _sigmoid(gt).astype(BF16)


def _in_proj(x, mod_tiles, norm_g, w_in_bf, cos_t, sin_t, qg, kg, head_sum, n_batch, tiles_per_seq):
    tok = x.shape[0]
    nt = tok // TM
    t_len = tiles_per_seq * TM
    tile = lambda i: (i, 0)
    const = lambda i: (0, 0)
    seq_tile = lambda i: (i % tiles_per_seq, 0)
    return pl.pallas_call(
        _inproj_kernel,
        grid=(nt,),
        in_specs=[
            pl.BlockSpec((TM, D_MODEL), tile),
            pl.BlockSpec((1, 1, 6 * D_MODEL), lambda i: (i, 0, 0)),
            pl.BlockSpec((1, D_MODEL), const),
            pl.BlockSpec((D_MODEL, D_IN), const),
            pl.BlockSpec((TM, LANES), seq_tile),
            pl.BlockSpec((TM, LANES), seq_tile),
            pl.BlockSpec((1, D_Q), const),
            pl.BlockSpec((1, D_KV), const),
            pl.BlockSpec((D_Q, D_Q), const),
        ],
        out_specs=[
            pl.BlockSpec((TM, D_SSM), lambda i: (i % tiles_per_seq, i // tiles_per_seq)),
            pl.BlockSpec((TM, 2 * D_CONV), tile),
            pl.BlockSpec((TM, D_Q), tile),
            pl.BlockSpec((TM, 2 * D_KV), tile),
            pl.BlockSpec((TM, N_BRANCH * D_MODEL), tile),
        ],
        out_shape=[
            jax.ShapeDtypeStruct((t_len, n_batch * D_SSM), BF16),
            jax.ShapeDtypeStruct((tok, 2 * D_CONV), BF16),
            jax.ShapeDtypeStruct((tok, D_Q), BF16),
            jax.ShapeDtypeStruct((tok, 2 * D_KV), BF16),
            jax.ShapeDtypeStruct((tok, N_BRANCH * D_MODEL), BF16),
        ],
        compiler_params=_cparams(("parallel",)),
        name="in_proj",
    )(x, mod_tiles, norm_g, w_in_bf, cos_t, sin_t, qg, kg, head_sum)


def _s5_kernel(uf_ref, ub_ref, bd_ref, cd_ref, lam_ref, yf_ref, yb_ref, sf_ref, sb_ref, carry_ref, *, n_batch):
    i = pl.program_id(0)

    @pl.when(i == 0)
    def _():
        carry_ref[...] = jnp.zeros_like(carry_ref)

    for z, (u_ref, s_ref, y_ref) in enumerate(((uf_ref, sf_ref, yf_ref), (ub_ref, sb_ref, yb_ref))):
        s_ref[...] = _dot(u_ref[...], bd_ref[z])
        for j in range(N_STATE // S5_COLS):
            re_cols = pl.ds(j * S5_COLS, S5_COLS)
            im_cols = pl.ds(N_STATE + j * S5_COLS, S5_COLS)
            lr = jnp.broadcast_to(lam_ref[2 * z:2 * z + 1, re_cols], (n_batch, S5_COLS))
            li = jnp.broadcast_to(lam_ref[2 * z + 1:2 * z + 2, re_cols], (n_batch, S5_COLS))

            def step(s, c, s_ref=s_ref, z=z, re_cols=re_cols, im_cols=im_cols, lr=lr, li=li):
                t = s if z == 0 else S5_CHUNK - 1 - s
                rows = pl.ds(pl.multiple_of(t * n_batch, n_batch), n_batch)
                xr, xi = c
                nr = lr * xr - li * xi + s_ref[rows, re_cols]
                ni = lr * xi + li * xr + s_ref[rows, im_cols]
                s_ref[rows, re_cols] = nr
                s_ref[rows, im_cols] = ni
                return nr, ni

            c0 = (carry_ref[2 * z, :, re_cols], carry_ref[2 * z + 1, :, re_cols])
            fr, fi = lax.fori_loop(0, S5_CHUNK, step, c0, unroll=4)
            carry_ref[2 * z, :, re_cols] = fr
            carry_ref[2 * z + 1, :, re_cols] = fi
        y_ref[...] = _dot(s_ref[...].astype(BF16), cd_ref[z])


def _s5(u_rows, bd, cd, lam, n_batch, t_len):
    rows = S5_CHUNK * n_batch
    n_chunks = t_len // S5_CHUNK
    ctx_chunks = N_CTX // S5_CHUNK

    def bwd_block(i):
        return (jnp.where(i < ctx_chunks, ctx_chunks - 1 - i, n_chunks - 1 + ctx_chunks - i), 0)

    return pl.pallas_call(
        functools.partial(_s5_kernel, n_batch=n_batch),
        grid=(n_chunks,),
        in_specs=[
            pl.BlockSpec((rows, D_SSM), lambda i: (i, 0)),
            pl.BlockSpec((rows, D_SSM), bwd_block),
            pl.BlockSpec((2, D_SSM, 2 * N_STATE), lambda i: (0, 0, 0)),
            pl.BlockSpec((2, 2 * N_STATE, D_SSM), lambda i: (0, 0, 0)),
            pl.BlockSpec((4, N_STATE), lambda i: (0, 0)),
        ],
        out_specs=[
            pl.BlockSpec((rows, D_SSM), lambda i: (i, 0)),
            pl.BlockSpec((rows, D_SSM), bwd_block),
        ],
        out_shape=[jax.ShapeDtypeStruct((t_len * n_batch, D_SSM), F32)] * 2,
        scratch_shapes=[
            pltpu.VMEM((rows, 2 * N_STATE), F32),
            pltpu.VMEM((rows, 2 * N_STATE), F32),
            pltpu.VMEM((4, n_batch, N_STATE), F32),
        ],
        compiler_params=_cparams(("arbitrary",)),
        name="s5_scan",
    )(u_rows, u_rows, bd, cd, lam)


def _attn_kernel(sink_ref, q_ref, kvc_ref, kv0_ref, kv1_ref, kv2_ref, o_ref, *, n_lat):
    n = pl.program_id(1) - N_CTX // ATTN_BLOCK
    rep = N_Q_HEADS // N_KV_HEADS
    rows = rep * ATTN_BLOCK
    q = q_ref[0]
    kvc = kvc_ref[0]
    band = jnp.concatenate([kv0_ref[0], kv1_ref[0], kv2_ref[0]], axis=0)
    iq = lax.broadcasted_iota(I32, (rows, 3 * ATTN_BLOCK), 0) % ATTN_BLOCK
    ik = lax.broadcasted_iota(I32, (rows, 3 * ATTN_BLOCK), 1)
    qpos = n * ATTN_BLOCK + iq
    kpos = (n - 1) * ATTN_BLOCK + ik
    valid = (jnp.abs(qpos - kpos) <= WINDOW) & (kpos >= 0) & (kpos < n_lat) & (n >= 0)
    head_of_row = lax.broadcasted_iota(I32, (rows, 1), 0) // ATTN_BLOCK
    contract_last = (((1,), (1,)), ((), ()))
    outs = []
    for g in range(N_KV_HEADS):
        qg = jnp.concatenate([q[:, (g * rep + r) * HEAD_DIM:(g * rep + r + 1) * HEAD_DIM] for r in range(rep)], axis=0)
        sink = jnp.zeros((rows, 1), F32)
        for r in range(rep):
            sink = jnp.where(head_of_row == r, sink_ref[g * rep + r], sink)
        kc = kvc[:, g * HEAD_DIM:(g + 1) * HEAD_DIM]
        vc = kvc[:, D_KV + g * HEAD_DIM:D_KV + (g + 1) * HEAD_DIM]
        kb = band[:, g * HEAD_DIM:(g + 1) * HEAD_DIM]
        vb = band[:, D_KV + g * HEAD_DIM:D_KV + (g + 1) * HEAD_DIM]
        sc = lax.dot_general(qg, kc, contract_last, preferred_element_type=F32)
        sb = lax.dot_general(qg, kb, contract_last, preferred_element_type=F32)
        sb = jnp.where(valid, sb, NEG)
        mx = jnp.maximum(jnp.maximum(jnp.max(sc, axis=-1, keepdims=True), jnp.max(sb, axis=-1, keepdims=True)), sink)
        pc = jnp.exp(sc - mx)
        pb = jnp.exp(sb - mx)
        den = jnp.sum(pc, axis=-1, keepdims=True) + jnp.sum(pb, axis=-1, keepdims=True) + jnp.exp(sink - mx)
        o = (_dot(pc.astype(BF16), vc) + _dot(pb.astype(BF16), vb)) / den
        outs.extend(o[r * ATTN_BLOCK:(r + 1) * ATTN_BLOCK] for r in range(rep))
    o_ref[0] = jnp.concatenate(outs, axis=-1).astype(BF16)


def _attention(q, kv, sinks, n_batch, t_len):
    nqb = t_len // ATTN_BLOCK
    first = N_CTX // ATTN_BLOCK
    q3 = q.reshape(n_batch, t_len, D_Q)
    kv3 = kv.reshape(n_batch, t_len, 2 * D_KV)

    def band(off):
        return lambda b, j, s: (b, jnp.clip(j + off, first, nqb - 1), 0)

    out = pl.pallas_call(
        functools.partial(_attn_kernel, n_lat=t_len - N_CTX),
        grid_spec=pltpu.PrefetchScalarGridSpec(
            num_scalar_prefetch=1,
            grid=(n_batch, nqb),
            in_specs=[
                pl.BlockSpec((1, ATTN_BLOCK, D_Q), lambda b, j, s: (b, j, 0)),
                pl.BlockSpec((1, N_CTX, 2 * D_KV), lambda b, j, s: (b, 0, 0)),
                pl.BlockSpec((1, ATTN_BLOCK, 2 * D_KV), band(-1)),
                pl.BlockSpec((1, ATTN_BLOCK, 2 * D_KV), band(0)),
                pl.BlockSpec((1, ATTN_BLOCK, 2 * D_KV), band(1)),
            ],
            out_specs=pl.BlockSpec((1, ATTN_BLOCK, D_Q), lambda b, j, s: (b, j, 0)),
        ),
        out_shape=jax.ShapeDtypeStruct((n_batch, t_len, D_Q), BF16),
        compiler_params=_cparams(("parallel", "parallel")),
        name="attention",
    )(sinks, q3, kv3, kv3, kv3, kv3)
    return out.reshape(n_batch * t_len, D_Q)


def _gelu_tanh(x):
    return 0.5 * x * (1.0 + jnp.tanh(math.sqrt(2.0 / math.pi) * (x + 0.044715 * (x * x * x))))


def _merge_kernel(x_ref, yf_ref, yb_ref, u_ref, cz_ref, czp_ref, czn_ref, ya_ref, gt_ref, mod_ref,
                  d_ref, cw_ref, wglu_ref, wso_ref, wco_ref, wao_ref, wo_ref, n2g_ref,
                  wrh_ref, wrl_ref, br_ref, xo_ref, h2_ref, lg_ref, *, tiles_per_seq):
    r = pl.program_id(0) % tiles_per_seq
    ys = yf_ref[...] + yb_ref[...] + d_ref[...] * u_ref[...].astype(F32)
    z = _gelu_tanh(ys)
    glu = z * _sigmoid(_dot(z.astype(BF16), wglu_ref[...]))
    br_ssm = _dot(glu.astype(BF16), wso_ref[...])

    cz = cz_ref[...].astype(F32)
    cb = cz[:, 0:D_CONV]
    zz = cz[:, D_CONV:2 * D_CONV]
    seg_first = r <= 1
    seg_last = (r == 0) | (r == tiles_per_seq - 1)
    prev_row = jnp.where(seg_first, 0.0, czp_ref[7:8, D_CONV:2 * D_CONV].astype(F32))
    next_row = jnp.where(seg_last, 0.0, czn_ref[0:1, D_CONV:2 * D_CONV].astype(F32))
    row = lax.broadcasted_iota(I32, (TM, D_CONV), 0)
    z_dn = jnp.where(row == 0, prev_row, pltpu.roll(zz, 1, axis=0))
    z_up = jnp.where(row == TM - 1, next_row, pltpu.roll(zz, TM - 1, axis=0))
    y_conv = cb * (cw_ref[0:1, :] * z_dn + cw_ref[1:2, :] * zz + cw_ref[2:3, :] * z_up)
    br_conv = _dot(y_conv.astype(BF16), wco_ref[...])
    br_attn = _dot(ya_ref[...], wao_ref[...])

    merged = (gt_ref[:, 0:D_MODEL].astype(F32) * br_ssm
              + gt_ref[:, D_MODEL:2 * D_MODEL].astype(F32) * br_conv
              + gt_ref[:, 2 * D_MODEL:3 * D_MODEL].astype(F32) * br_attn)
    mix = _dot(merged.astype(BF16), wo_ref[...])
    m = mod_ref[0]
    xn = x_ref[...] + m[:, 2 * D_MODEL:3 * D_MODEL] * mix
    xo_ref[...] = xn
    h2 = _rms(xn, n2g_ref[...]) * (1.0 + m[:, 4 * D_MODEL:5 * D_MODEL]) + m[:, 3 * D_MODEL:4 * D_MODEL]
    h2_ref[...] = h2
    hi = h2.astype(BF16)
    lo = (h2 - hi.astype(F32)).astype(BF16)
    lg_ref[...] = _dot(hi, wrh_ref[...]) + _dot(lo, wrh_ref[...]) + _dot(hi, wrl_ref[...]) + br_ref[...]


def _merge(x, yf, yb, u_tb, cz, ya, gt, mod_tiles, d_skip, conv_w, wglu, wso, wco, wao, wo, n2g,
           wr_hi, wr_lo, b_r, tiles_per_seq):
    tok = x.shape[0]
    nt = tok // TM
    tile = lambda i: (i, 0)
    const = lambda i: (0, 0)
    tb = lambda i: (i % tiles_per_seq, i // tiles_per_seq)
    rows8 = TM // 8
    return pl.pallas_call(
        functools.partial(_merge_kernel, tiles_per_seq=tiles_per_seq),
        grid=(nt,),
        in_specs=[
            pl.BlockSpec((TM, D_MODEL), tile),
            pl.BlockSpec((TM, D_SSM), tb),
            pl.BlockSpec((TM, D_SSM), tb),
            pl.BlockSpec((TM, D_SSM), tb),
            pl.BlockSpec((TM, 2 * D_CONV), tile),
            pl.BlockSpec((8, 2 * D_CONV), lambda i: (jnp.maximum(i * rows8 - 1, 0), 0)),
            pl.BlockSpec((8, 2 * D_CONV), lambda i: (jnp.minimum((i + 1) * rows8, tok // 8 - 1), 0)),
            pl.BlockSpec((TM, D_Q), tile),
            pl.BlockSpec((TM, N_BRANCH * D_MODEL), tile),
            pl.BlockSpec((1, 1, 6 * D_MODEL), lambda i: (i, 0, 0)),
            pl.BlockSpec((1, D_SSM), const),
            pl.BlockSpec((3, D_CONV), const),
            pl.BlockSpec((D_SSM, D_SSM), const),
            pl.BlockSpec((D_SSM, D_MODEL), const),
            pl.BlockSpec((D_CONV, D_MODEL), const),
            pl.BlockSpec((D_Q, D_MODEL), const),
            pl.BlockSpec((D_MODEL, D_MODEL), const),
            pl.BlockSpec((1, D_MODEL), const),
            pl.BlockSpec((D_MODEL, LANES), const),
            pl.BlockSpec((D_MODEL, LANES), const),
            pl.BlockSpec((1, LANES), const),
        ],
        out_specs=[
            pl.BlockSpec((TM, D_MODEL), tile),
            pl.BlockSpec((TM, D_MODEL), tile),
            pl.BlockSpec((TM, LANES), tile),
        ],
        out_shape=[
            jax.ShapeDtypeStruct((tok, D_MODEL), F32),
            jax.ShapeDtypeStruct((tok, D_MODEL), F32),
            jax.ShapeDtypeStruct((tok, LANES), F32),
        ],
        compiler_params=_cparams(("parallel",)),
        name="branch_merge",
    )(x, yf, yb, u_tb, cz, cz, cz, ya, gt, mod_tiles, d_skip, conv_w, wglu, wso, wco, wao, wo, n2g,
      wr_hi, wr_lo, b_r)


def _route_kernel(lg_ref, tri_ref, out_ref, cnt_ref, carry_ref):
    @pl.when(pl.program_id(0) == 0)
    def _():
        carry_ref[...] = jnp.zeros_like(carry_ref)

    l = lg_ref[...]
    lane = lax.broadcasted_iota(I32, l.shape, 1)
    vals, idxs, hots = [], [], []
    for _ in range(TOP_K):
        mx = jnp.max(l, axis=-1, keepdims=True)
        idx = jnp.min(jnp.where(l == mx, lane, LANES), axis=-1, keepdims=True)
        hot = lane == idx
        l = jnp.where(hot, -3e38, l)
        vals.append(mx)
        idxs.append(idx)
        hots.append(hot)
    ex = [jnp.exp(v - vals[0]) for v in vals]
    den = ex[0] + ex[1] + ex[2] + ex[3]
    picked = jnp.zeros(l.shape, F32)
    for hot in hots:
        picked = picked + hot.astype(F32)
    cum = _dot(tri_ref[...], picked.astype(BF16)) + carry_ref[...]
    out = jnp.zeros(l.shape, F32)
    for k in range(TOP_K):
        rank = jnp.sum(jnp.where(hots[k], cum, 0.0), axis=-1, keepdims=True)
        out = jnp.where(lane == k, idxs[k].astype(F32), out)
        out = jnp.where(lane == TOP_K + k, rank, out)
        out = jnp.where(lane == 2 * TOP_K + k, ex[k] / den, out)
    out_ref[...] = out
    carry_ref[...] = carry_ref[...] + jnp.sum(picked, axis=0, keepdims=True)
    cnt_ref[...] = carry_ref[...]


def _route(logits, tri):
    tok = logits.shape[0]
    return pl.pallas_call(
        _route_kernel,
        grid=(tok // TM,),
        in_specs=[pl.BlockSpec((TM, LANES), lambda i: (i, 0)), pl.BlockSpec((TM, TM), lambda i: (0, 0))],
        out_specs=[pl.BlockSpec((TM, LANES), lambda i: (i, 0)), pl.BlockSpec((1, LANES), lambda i: (0, 0))],
        out_shape=[jax.ShapeDtypeStruct((tok, LANES), F32), jax.ShapeDtypeStruct((1, LANES), F32)],
        scratch_shapes=[pltpu.VMEM((1, LANES), F32)],
        compiler_params=_cparams(("arbitrary",)),
        name="route",
    )(logits, tri)


def _dispatch_kernel(zstart_ref, nu_ref, slot_hbm, h_ref, xs_hbm, slot_smem, zero_ref, sem_i, sem_z, sem_r,
                     *, n_blocks):
    i = pl.program_id(0)
    n_idx = TM * TOP_K

    @pl.when(i == 0)
    def _():
        zero_ref[...] = jnp.zeros_like(zero_ref)

        def zero_copy(start):
            rows = pl.ds(pl.multiple_of(start, MOE_BLOCK), MOE_BLOCK)
            return pltpu.make_async_copy(zero_ref, xs_hbm.at[rows, :], sem_z)

        def tail_start(b, c):
            zero_copy(b * MOE_BLOCK).start()
            return c

        def tail_wait(b, c):
            zero_copy(b * MOE_BLOCK).wait()
            return c

        for e in range(N_EXPERTS):
            @pl.when(zstart_ref[e] >= 0)
            def _():
                zero_copy(zstart_ref[e]).start()
        lax.fori_loop(nu_ref[0], n_blocks, tail_start, 0)
        for e in range(N_EXPERTS):
            @pl.when(zstart_ref[e] >= 0)
            def _():
                zero_copy(zstart_ref[e]).wait()
        lax.fori_loop(nu_ref[0], n_blocks, tail_wait, 0)

    cp = pltpu.make_async_copy(slot_hbm.at[pl.ds(pl.multiple_of(i * n_idx, n_idx), n_idx)], slot_smem, sem_i)
    cp.start()
    cp.wait()

    def row_copy(r, k):
        slot = slot_smem[r * TOP_K + k]
        return pltpu.make_async_copy(h_ref.at[pl.ds(r, 1), :], xs_hbm.at[pl.ds(slot, 1), :], sem_r)

    def issue(r, c):
        for k in range(TOP_K):
            row_copy(r, k).start()
        return c

    def drain(r, c):
        for k in range(TOP_K):
            row_copy(r, k).wait()
        return c

    lax.fori_loop(0, TM, issue, 0, unroll=2)
    lax.fori_loop(0, TM, drain, 0, unroll=2)


def _dispatch(h2, slots, zstart, n_used, n_slots):
    tok = h2.shape[0]
    return pl.pallas_call(
        functools.partial(_dispatch_kernel, n_blocks=n_slots // MOE_BLOCK),
        grid_spec=pltpu.PrefetchScalarGridSpec(
            num_scalar_prefetch=2,
            grid=(tok // TM,),
            in_specs=[
                pl.BlockSpec(memory_space=pl.ANY),
                pl.BlockSpec((TM, D_MODEL), lambda i, z, nu: (i, 0)),
            ],
            out_specs=pl.BlockSpec(memory_space=pl.ANY),
            scratch_shapes=[
                pltpu.SMEM((TM * TOP_K,), I32),
                pltpu.VMEM((MOE_BLOCK, D_MODEL), F32),
                pltpu.SemaphoreType.DMA,
                pltpu.SemaphoreType.DMA,
                pltpu.SemaphoreType.DMA,
            ],
        ),
        out_shape=jax.ShapeDtypeStruct((n_slots, D_MODEL), F32),
        compiler_params=_cparams(("arbitrary",)),
        name="dispatch",
    )(zstart, n_used, slots, h2)


DEINT = 256


def _expert_kernel(be_ref, nu_ref, x_ref, wgu_ref, bg_ref, bu_ref, wd_ref, bd_ref, sg_ref, su_ref, y_ref,
                   wg_s, wu_s, wd_s):
    i = pl.program_id(0)
    prev = be_ref[jnp.maximum(i - 1, 0)]

    @pl.when((i == 0) | (be_ref[i] != prev))
    def _():
        for c in range(2 * D_EXPERT // DEINT):
            w = wgu_ref[0, :, c * DEINT:(c + 1) * DEINT].astype(BF16)
            cols = pl.ds(c * (DEINT // 2), DEINT // 2)
            wg_s[:, cols] = _dot(w, sg_ref[...]).astype(BF16)
            wu_s[:, cols] = _dot(w, su_ref[...]).astype(BF16)
        wd_s[...] = wd_ref[0].astype(BF16)

    @pl.when(i < nu_ref[0])
    def _():
        x = x_ref[...].astype(BF16)
        g = _dot(x, wg_s[...]) + bg_ref[0]
        u = _dot(x, wu_s[...]) + bu_ref[0]
        glu = jnp.minimum(g, SWIGLU_LIMIT)
        up = jnp.clip(u, -SWIGLU_LIMIT, SWIGLU_LIMIT)
        act = glu * _sigmoid(SWIGLU_ALPHA * glu) * (up + 1.0)
        y_ref[...] = _dot(act.astype(BF16), wd_s[...]) + bd_ref[0]

    @pl.when(i >= nu_ref[0])
    def _():
        y_ref[...] = jnp.zeros_like(y_ref)


def _experts(xs, block_expert, n_used, wgu, bg, bu, wd, bd, sel_g, sel_u):
    n_slots = xs.shape[0]
    n_blocks = n_slots // MOE_BLOCK
    wmap = lambda i, be, nu: (be[i], 0, 0)
    const = lambda i, be, nu: (0, 0)
    return pl.pallas_call(
        _expert_kernel,
        grid_spec=pltpu.PrefetchScalarGridSpec(
            num_scalar_prefetch=2,
            grid=(n_blocks,),
            in_specs=[
                pl.BlockSpec((MOE_BLOCK, D_MODEL), lambda i, be, nu: (jnp.minimum(i, nu[0] - 1), 0)),
                pl.BlockSpec((1, D_MODEL, 2 * D_EXPERT), wmap),
                pl.BlockSpec((1, 1, D_EXPERT), wmap),
                pl.BlockSpec((1, 1, D_EXPERT), wmap),
                pl.BlockSpec((1, D_EXPERT, D_MODEL), wmap),
                pl.BlockSpec((1, 1, D_MODEL), wmap),
                pl.BlockSpec((DEINT, DEINT // 2), const),
                pl.BlockSpec((DEINT, DEINT // 2), const),
            ],
            out_specs=pl.BlockSpec((MOE_BLOCK, D_MODEL), lambda i, be, nu: (i, 0)),
            scratch_shapes=[
                pltpu.VMEM((D_MODEL, D_EXPERT), BF16),
                pltpu.VMEM((D_MODEL, D_EXPERT), BF16),
                pltpu.VMEM((D_EXPERT, D_MODEL), BF16),
            ],
        ),
        out_shape=jax.ShapeDtypeStruct((n_slots, D_MODEL), F32),
        compiler_params=_cparams(("arbitrary",)),
        name="experts",
    )(block_expert, n_used, xs, wgu, bg, bu, wd, bd, sel_g, sel_u)


def _combine_kernel(slot_hbm, ys_hbm, x_ref, rt_ref, mod_ref, o_ref, slot_smem, buf_ref, sem_i, sem_r):
    i = pl.program_id(0)
    n_idx = TM * TOP_K
    cp = pltpu.make_async_copy(slot_hbm.at[pl.ds(pl.multiple_of(i * n_idx, n_idx), n_idx)], slot_smem, sem_i)
    cp.start()
    cp.wait()

    def row_copy(r, k):
        slot = slot_smem[r * TOP_K + k]
        return pltpu.make_async_copy(ys_hbm.at[pl.ds(slot, 1), :], buf_ref.at[k, pl.ds(r, 1), :], sem_r)

    def issue(r, c):
        for k in range(TOP_K):
            row_copy(r, k).start()
        return c

    def drain(r, c):
        for k in range(TOP_K):
            row_copy(r, k).wait()
        return c

    lax.fori_loop(0, TM, issue, 0, unroll=2)
    lax.fori_loop(0, TM, drain, 0, unroll=2)
    rt = rt_ref[...]
    y = rt[:, 2 * TOP_K:2 * TOP_K + 1] * buf_ref[0]
    for k in range(1, TOP_K):
        y = y + rt[:, 2 * TOP_K + k:2 * TOP_K + k + 1] * buf_ref[k]
    o_ref[...] = x_ref[...] + mod_ref[0][:, 5 * D_MODEL:6 * D_MODEL] * y


def _combine(x, ys, slots, route, mod_tiles):
    tok = x.shape[0]
    return pl.pallas_call(
        _combine_kernel,
        grid=(tok // TM,),
        in_specs=[
            pl.BlockSpec(memory_space=pl.ANY),
            pl.BlockSpec(memory_space=pl.ANY),
            pl.BlockSpec((TM, D_MODEL), lambda i: (i, 0)),
            pl.BlockSpec((TM, LANES), lambda i: (i, 0)),
            pl.BlockSpec((1, 1, 6 * D_MODEL), lambda i: (i, 0, 0)),
        ],
        out_specs=pl.BlockSpec((TM, D_MODEL), lambda i: (i, 0)),
        out_shape=jax.ShapeDtypeStruct((tok, D_MODEL), F32),
        scratch_shapes=[
            pltpu.SMEM((TM * TOP_K,), I32),
            pltpu.VMEM((TOP_K, TM, D_MODEL), F32),
            pltpu.SemaphoreType.DMA,
            pltpu.SemaphoreType.DMA,
        ],
        compiler_params=_cparams(("arbitrary",)),
        name="combine",
    )(slots, ys, x, route, mod_tiles)


def _rope_tables(t_len):
    n_lat = t_len - N_CTX
    t = np.arange(n_lat)
    n_pairs = HEAD_DIM // 4
    inv_freq = jnp.asarray(ROPE_BASE, F32) ** (-jnp.arange(n_pairs, dtype=F32) / n_pairs)
    row = jnp.asarray(t // GRID_W, F32)
    col = jnp.asarray(t % GRID_W, F32)
    ang = jnp.concatenate([row[:, None] * inv_freq, col[:, None] * inv_freq], axis=-1)
    ang = jnp.concatenate([jnp.zeros((N_CTX, HEAD_DIM // 2), F32), ang], axis=0)
    cos = jnp.tile(jnp.cos(ang), (1, LANES // (HEAD_DIM // 2)))
    sin = jnp.tile(jnp.sin(ang), (1, LANES // (HEAD_DIM // 2)))
    return cos, sin


def _s5_operands(lam_re, lam_im, log_dt, b_re, b_im, c_re, c_im):
    lam = lax.complex(lam_re, lam_im)
    dt = jnp.exp(log_dt)[..., None]
    lam_bar = jnp.exp(lam * dt)
    b_bar = ((lam_bar - 1.0) / lam)[..., None] * lax.complex(b_re, b_im)
    eye = jnp.eye(N_SSM_GROUPS, dtype=F32)

    def in_block(w):
        return jnp.einsum("zgph,gk->zghkp", w, eye).reshape(2, D_SSM, N_STATE)

    def out_block(w):
        return jnp.einsum("zghp,gk->zgpkh", w, eye).reshape(2, N_STATE, D_SSM)

    bd = jnp.concatenate([in_block(b_bar.real), in_block(b_bar.imag)], axis=-1).astype(BF16)
    cd = jnp.concatenate([out_block(c_re), out_block(-c_im)], axis=1).astype(BF16)
    lam_rows = jnp.stack([lam_bar[0].real.reshape(-1), lam_bar[0].imag.reshape(-1),
                          lam_bar[1].real.reshape(-1), lam_bar[1].imag.reshape(-1)])
    return bd, cd, lam_rows


def kernel(x, c, ctx, c_ctx, w_mod, b_mod, norm1_g, norm2_g, w_in, ssm_lam_re, ssm_lam_im, ssm_log_dt, ssm_b_re, ssm_b_im, ssm_c_re, ssm_c_im, ssm_d, w_glu, w_ssm_out, conv_w, w_conv_out, q_norm_g, k_norm_g, attn_sinks, w_attn_out, w_o, w_router, b_router, w_gate_up, b_gate_up, w_down, b_down):
    n_batch, n_lat, _ = x.shape
    t_len = N_CTX + n_lat
    tiles_per_seq = t_len // TM
    tok = n_batch * t_len
    nt = tok // TM
    n_assign = tok * TOP_K
    n_blocks = (n_assign + N_EXPERTS * (MOE_BLOCK - 1)) // MOE_BLOCK
    n_slots = n_blocks * MOE_BLOCK

    xs = jnp.concatenate([ctx, x], axis=1).reshape(tok, D_MODEL)

    mod_rows = 8 * ((n_batch + 1 + 7) // 8)
    cvec = jnp.zeros((mod_rows, D_MODEL), F32).at[:n_batch].set(c).at[n_batch].set(c_ctx)
    mod_all = _modulation(cvec, w_mod, b_mod)
    tile_ids = np.arange(nt)
    tile_row = np.where(tile_ids % tiles_per_seq == 0, n_batch, tile_ids // tiles_per_seq)

    cos_t, sin_t = _rope_tables(t_len)
    head_sum = jnp.asarray(np.kron(np.eye(N_Q_HEADS), np.full((HEAD_DIM, HEAD_DIM), 1.0 / HEAD_DIM)), BF16)
    tri = jnp.asarray(np.tril(np.ones((TM, TM)), -1), BF16)
    pick = np.arange(DEINT)[:, None] == 2 * np.arange(DEINT // 2)[None, :]
    sel_g = jnp.asarray(pick, BF16)
    sel_u = jnp.asarray(np.roll(pick, 1, axis=0), BF16)

    for l in range(DEPTH):
        mod_tiles = mod_all[l][tile_row].reshape(nt, 1, 6 * D_MODEL)
        qg = jnp.tile(q_norm_g[l], N_Q_HEADS).reshape(1, D_Q)
        kg = jnp.tile(k_norm_g[l], N_KV_HEADS).reshape(1, D_KV)
        u_tb, cz, q, kv, gt = _in_proj(xs, mod_tiles, norm1_g[l].reshape(1, D_MODEL), w_in[l].astype(BF16),
                                       cos_t, sin_t, qg, kg, head_sum, n_batch, tiles_per_seq)

        bd, cd, lam_rows = _s5_operands(ssm_lam_re[l], ssm_lam_im[l], ssm_log_dt[l], ssm_b_re[l], ssm_b_im[l],
                                        ssm_c_re[l], ssm_c_im[l])
        yf, yb = _s5(u_tb.reshape(t_len * n_batch, D_SSM), bd, cd, lam_rows, n_batch, t_len)
        ya = _attention(q, kv, attn_sinks[l], n_batch, t_len)

        wr = jnp.zeros((D_MODEL, LANES), F32).at[:, :N_EXPERTS].set(w_router[l])
        wr_hi = wr.astype(BF16)
        wr_lo = (wr - wr_hi.astype(F32)).astype(BF16)
        b_r = jnp.full((1, LANES), NEG, F32).at[0, :N_EXPERTS].set(b_router[l])
        xs, h2, logits = _merge(xs, yf.reshape(t_len, n_batch * D_SSM), yb.reshape(t_len, n_batch * D_SSM), u_tb,
                                cz, ya, gt, mod_tiles, ssm_d[l].reshape(1, D_SSM), conv_w[l],
                                w_glu[l].astype(BF16), w_ssm_out[l].astype(BF16), w_conv_out[l].astype(BF16),
                                w_attn_out[l].astype(BF16), w_o[l].astype(BF16), norm2_g[l].reshape(1, D_MODEL),
                                wr_hi, wr_lo, b_r, tiles_per_seq)

        route, counts = _route(logits, tri)
        counts = counts[0, :N_EXPERTS].astype(I32)
        padded = (counts + MOE_BLOCK - 1) // MOE_BLOCK * MOE_BLOCK
        pad_end = jnp.cumsum(padded)
        pad_start = pad_end - padded
        sel = route[:, 0:TOP_K].astype(I32)
        rank = route[:, TOP_K:2 * TOP_K].astype(I32)
        slots = (pad_start[sel] + rank).reshape(n_assign)
        n_used = (pad_end[-1] // MOE_BLOCK).astype(I32)
        blk = jnp.minimum(jnp.arange(n_blocks, dtype=I32), n_used - 1) * MOE_BLOCK
        block_expert = jnp.minimum(jnp.sum(blk[:, None] >= pad_end[None, :], axis=1), N_EXPERTS - 1).astype(I32)
        zstart = jnp.where(counts > 0, pad_end - MOE_BLOCK, -1).astype(I32)

        x_sorted = _dispatch(h2, slots, zstart, n_used.reshape(1), n_slots)
        y_sorted = _experts(x_sorted, block_expert, n_used.reshape(1), w_gate_up[l],
                            b_gate_up[l][:, None, 0::2], b_gate_up[l][:, None, 1::2],
                            w_down[l], b_down[l][:, None, :], sel_g, sel_u)
        xs = _combine(xs, y_sorted, slots, route, mod_tiles)

    return xs.reshape(n_batch, t_len, D_MODEL)[:, N_CTX:, :]
```

```python
import functools
import math

import jax
import jax.numpy as jnp
import numpy as np
from jax import lax
from jax.experimental import pallas as pl
from jax.experimental.pallas import tpu as pltpu

F32 = jnp.float32
BF16 = jnp.bfloat16
I32 = jnp.int32

D_MODEL = 1024
DEPTH = 4
N_CTX = 256
HEAD_DIM = 64
N_Q_HEADS = 8
N_KV_HEADS = 2
D_Q = N_Q_HEADS * HEAD_DIM
D_KV = N_KV_HEADS * HEAD_DIM
WINDOW = 128
ATTN_BLOCK = 128
ROPE_BASE = 10000.0
GRID_W = 64
D_SSM = 256
SSM_GROUP = 16
N_SSM_GROUPS = 16
SSM_STATE = 64
N_STATE = N_SSM_GROUPS * SSM_STATE
D_CONV = 256
N_BRANCH = 3
D_IN = D_SSM + 3 * D_CONV + D_Q + 2 * D_KV + N_BRANCH * D_MODEL
N_EXPERTS = 32
TOP_K = 4
D_EXPERT = 1024
SWIGLU_LIMIT = 7.0
SWIGLU_ALPHA = 1.702
EPS = 1e-6

LANES = 128
TM = 256
MOE_BLOCK = 256
SEG = 8
LOCAL_ROWS = -(-(TM * TOP_K + N_EXPERTS * (SEG - 1)) // LANES) * LANES
S5_CHUNK = 64
S5_COLS = 512
NEG = -1e30
VMEM_LIMIT = 56 * 1024 * 1024


def _cparams(sem):
    return pltpu.CompilerParams(dimension_semantics=sem, vmem_limit_bytes=VMEM_LIMIT)


def _dot(a, b):
    return jnp.dot(a, b, preferred_element_type=F32)


def _sigmoid(x):
    return 1.0 / (1.0 + jnp.exp(-x))


def _rms(x, g):
    ms = jnp.mean(x * x, axis=-1, keepdims=True)
    return x * lax.rsqrt(ms + EPS) * g


def _mod_kernel(c_ref, w_ref, b_ref, o_ref):
    c = c_ref[...]
    s = (c * _sigmoid(c)).astype(BF16)
    o_ref[0] = _dot(s, w_ref[0].astype(BF16)) + b_ref[0]


def _modulation(cvec, w_mod, b_mod):
    rows = cvec.shape[0]
    nblk = 1536
    return pl.pallas_call(
        _mod_kernel,
        grid=(DEPTH, 6 * D_MODEL // nblk),
        in_specs=[
            pl.BlockSpec((rows, D_MODEL), lambda l, j: (0, 0)),
            pl.BlockSpec((1, D_MODEL, nblk), lambda l, j: (l, 0, j)),
            pl.BlockSpec((1, 1, nblk), lambda l, j: (l, 0, j)),
        ],
        out_specs=pl.BlockSpec((1, rows, nblk), lambda l, j: (l, 0, j)),
        out_shape=jax.ShapeDtypeStruct((DEPTH, rows, 6 * D_MODEL), F32),
        compiler_params=_cparams(("arbitrary", "arbitrary")),
        name="modulation",
    )(cvec, w_mod, b_mod.reshape(DEPTH, 1, 6 * D_MODEL))


def _rot_half(x, width):
    lane = lax.broadcasted_iota(I32, x.shape, 1)
    first = (lane % HEAD_DIM) < (HEAD_DIM // 2)
    return jnp.where(first, -pltpu.roll(x, width - HEAD_DIM // 2, axis=1), pltpu.roll(x, HEAD_DIM // 2, axis=1))


def _inproj_kernel(x_ref, mod_ref, g_ref, w_ref, cos_ref, sin_ref, qg_ref, kg_ref, hs_ref,
                   u_ref, cz_ref, q_ref, kv_ref, gt_ref):
    m = mod_ref[0]
    h = _rms(x_ref[...], g_ref[...])
    h = (h * (1.0 + m[:, D_MODEL:2 * D_MODEL]) + m[:, 0:D_MODEL]).astype(BF16)
    o_gate = D_SSM + 3 * D_CONV + D_Q + 2 * D_KV
    y = _dot(h, w_ref[:, 0:o_gate])
    u_ref[...] = y[:, 0:D_SSM].astype(BF16)
    cb = y[:, D_SSM:D_SSM + D_CONV]
    cc = y[:, D_SSM + D_CONV:D_SSM + 2 * D_CONV]
    cx = y[:, D_SSM + 2 * D_CONV:D_SSM + 3 * D_CONV]
    cz_ref[...] = jnp.concatenate([cb, cc * cx], axis=-1).astype(BF16)
    o_q = D_SSM + 3 * D_CONV
    q = y[:, o_q:o_q + D_Q]
    k = y[:, o_q + D_Q:o_q + D_Q + D_KV]
    v = y[:, o_q + D_Q + D_KV:o_gate]
    cos = cos_ref[...]
    sin = sin_ref[...]
    q_ms = _dot((q * q).astype(BF16), hs_ref[...])
    qn = q * lax.rsqrt(q_ms + EPS) * qg_ref[...]
    cos_q = jnp.concatenate([cos] * (D_Q // LANES), axis=-1)
    sin_q = jnp.concatenate([sin] * (D_Q // LANES), axis=-1)
    qr = qn * cos_q + _rot_half(qn, D_Q) * sin_q
    q_ref[...] = (qr * (HEAD_DIM ** -0.5)).astype(BF16)
    k_ms = _dot((k * k).astype(BF16), hs_ref[0:D_KV, 0:D_KV])
    kn = k * lax.rsqrt(k_ms + EPS) * kg_ref[...]
    kr = kn * cos + _rot_half(kn, D_KV) * sin
    kv_ref[...] = jnp.concatenate([kr, v], axis=-1).astype(BF16)
    gt = _dot(h, w_ref[:, o_gate:D_IN])
    gt_ref[...] = _sigmoid(gt).astype(BF16)


def _in_proj(x, mod_tiles, norm_g, w_in_bf, cos_t, sin_t, qg, kg, head_sum, n_batch, tiles_per_seq):
    tok = x.shape[0]
    nt = tok // TM
    t_len = tiles_per_seq * TM
    tile = lambda i: (i, 0)
    const = lambda i: (0, 0)
    seq_tile = lambda i: (i % tiles_per_seq, 0)
    return pl.pallas_call(
        _inproj_kernel,
        grid=(nt,),
        in_specs=[
            pl.BlockSpec((TM, D_MODEL), tile),
            pl.BlockSpec((1, 1, 6 * D_MODEL), lambda i: (i, 0, 0)),
            pl.BlockSpec((1, D_MODEL), const),
            pl.BlockSpec((D_MODEL, D_IN), const),
            pl.BlockSpec((TM, LANES), seq_tile),
            pl.BlockSpec((TM, LANES), seq_tile),
            pl.BlockSpec((1, D_Q), const),
            pl.BlockSpec((1, D_KV), const),
            pl.BlockSpec((D_Q, D_Q), const),
        ],
        out_specs=[
            pl.BlockSpec((TM, D_SSM), lambda i: (i % tiles_per_seq, i // tiles_per_seq)),
            pl.BlockSpec((TM, 2 * D_CONV), tile),
            pl.BlockSpec((TM, D_Q), tile),
            pl.BlockSpec((TM, 2 * D_KV), tile),
            pl.BlockSpec((TM, N_BRANCH * D_MODEL), tile),
        ],
        out_shape=[
            jax.ShapeDtypeStruct((t_len, n_batch * D_SSM), BF16),
            jax.ShapeDtypeStruct((tok, 2 * D_CONV), BF16),
            jax.ShapeDtypeStruct((tok, D_Q), BF16),
            jax.ShapeDtypeStruct((tok, 2 * D_KV), BF16),
            jax.ShapeDtypeStruct((tok, N_BRANCH * D_MODEL), BF16),
        ],
        compiler_params=_cparams(("parallel",)),
        name="in_proj",
    )(x, mod_tiles, norm_g, w_in_bf, cos_t, sin_t, qg, kg, head_sum)


def _s5_kernel(uf_ref, ub_ref, bd_ref, cd_ref, lam_ref, yf_ref, yb_ref, sf_ref, sb_ref, carry_ref, *, n_batch):
    i = pl.program_id(0)

    @pl.when(i == 0)
    def _():
        carry_ref[...] = jnp.zeros_like(carry_ref)

    for z, (u_ref, s_ref, y_ref) in enumerate(((uf_ref, sf_ref, yf_ref), (ub_ref, sb_ref, yb_ref))):
        s_ref[...] = _dot(u_ref[...], bd_ref[z])
        for j in range(N_STATE // S5_COLS):
            re_cols = pl.ds(j * S5_COLS, S5_COLS)
            im_cols = pl.ds(N_STATE + j * S5_COLS, S5_COLS)
            lr = jnp.broadcast_to(lam_ref[2 * z:2 * z + 1, re_cols], (n_batch, S5_COLS))
            li = jnp.broadcast_to(lam_ref[2 * z + 1:2 * z + 2, re_cols], (n_batch, S5_COLS))

            def step(s, c, s_ref=s_ref, z=z, re_cols=re_cols, im_cols=im_cols, lr=lr, li=li):
                t = s if z == 0 else S5_CHUNK - 1 - s
                rows = pl.ds(pl.multiple_of(t * n_batch, n_batch), n_batch)
                xr, xi = c
                nr = lr * xr - li * xi + s_ref[rows, re_cols]
                ni = lr * xi + li * xr + s_ref[rows, im_cols]
                s_ref[rows, re_cols] = nr
                s_ref[rows, im_cols] = ni
                return nr, ni

            c0 = (carry_ref[2 * z, :, re_cols], carry_ref[2 * z + 1, :, re_cols])
            fr, fi = lax.fori_loop(0, S5_CHUNK, step, c0, unroll=4)
            carry_ref[2 * z, :, re_cols] = fr
            carry_ref[2 * z + 1, :, re_cols] = fi
        y_ref[...] = _dot(s_ref[...].astype(BF16), cd_ref[z])


def _s5(u_rows, bd, cd, lam, n_batch, t_len):
    rows = S5_CHUNK * n_batch
    n_chunks = t_len // S5_CHUNK
    ctx_chunks = N_CTX // S5_CHUNK

    def bwd_block(i):
        return (jnp.where(i < ctx_chunks, ctx_chunks - 1 - i, n_chunks - 1 + ctx_chunks - i), 0)

    return pl.pallas_call(
        functools.partial(_s5_kernel, n_batch=n_batch),
        grid=(n_chunks,),
        in_specs=[
            pl.BlockSpec((rows, D_SSM), lambda i: (i, 0)),
            pl.BlockSpec((rows, D_SSM), bwd_block),
            pl.BlockSpec((2, D_SSM, 2 * N_STATE), lambda i: (0, 0, 0)),
            pl.BlockSpec((2, 2 * N_STATE, D_SSM), lambda i: (0, 0, 0)),
            pl.BlockSpec((4, N_STATE), lambda i: (0, 0)),
        ],
        out_specs=[
            pl.BlockSpec((rows, D_SSM), lambda i: (i, 0)),
            pl.BlockSpec((rows, D_SSM), bwd_block),
        ],
        out_shape=[jax.ShapeDtypeStruct((t_len * n_batch, D_SSM), F32)] * 2,
        scratch_shapes=[
            pltpu.VMEM((rows, 2 * N_STATE), F32),
            pltpu.VMEM((rows, 2 * N_STATE), F32),
            pltpu.VMEM((4, n_batch, N_STATE), F32),
        ],
        compiler_params=_cparams(("arbitrary",)),
        name="s5_scan",
    )(u_rows, u_rows, bd, cd, lam)


def _attn_kernel(sink_ref, q_ref, kvc_ref, kv0_ref, kv1_ref, kv2_ref, o_ref, *, n_lat):
    n = pl.program_id(1) - N_CTX // ATTN_BLOCK
    rep = N_Q_HEADS // N_KV_HEADS
    rows = rep * ATTN_BLOCK
    q = q_ref[0]
    kvc = kvc_ref[0]
    band = jnp.concatenate([kv0_ref[0], kv1_ref[0], kv2_ref[0]], axis=0)
    iq = lax.broadcasted_iota(I32, (rows, 3 * ATTN_BLOCK), 0) % ATTN_BLOCK
    ik = lax.broadcasted_iota(I32, (rows, 3 * ATTN_BLOCK), 1)
    qpos = n * ATTN_BLOCK + iq
    kpos = (n - 1) * ATTN_BLOCK + ik
    valid = (jnp.abs(qpos - kpos) <= WINDOW) & (kpos >= 0) & (kpos < n_lat) & (n >= 0)
    head_of_row = lax.broadcasted_iota(I32, (rows, 1), 0) // ATTN_BLOCK
    contract_last = (((1,), (1,)), ((), ()))
    outs = []
    for g in range(N_KV_HEADS):
        qg = jnp.concatenate([q[:, (g * rep + r) * HEAD_DIM:(g * rep + r + 1) * HEAD_DIM] for r in range(rep)], axis=0)
        sink = jnp.zeros((rows, 1), F32)
        for r in range(rep):
            sink = jnp.where(head_of_row == r, sink_ref[g * rep + r], sink)
        kc = kvc[:, g * HEAD_DIM:(g + 1) * HEAD_DIM]
        vc = kvc[:, D_KV + g * HEAD_DIM:D_KV + (g + 1) * HEAD_DIM]
        kb = band[:, g * HEAD_DIM:(g + 1) * HEAD_DIM]
        vb = band[:, D_KV + g * HEAD_DIM:D_KV + (g + 1) * HEAD_DIM]
        sc = lax.dot_general(qg, kc, contract_last, preferred_element_type=F32)
        sb = lax.dot_general(qg, kb, contract_last, preferred_element_type=F32)
        sb = jnp.where(valid, sb, NEG)
        mx = jnp.maximum(jnp.maximum(jnp.max(sc, axis=-1, keepdims=True), jnp.max(sb, axis=-1, keepdims=True)), sink)
        pc = jnp.exp(sc - mx)
        pb = jnp.exp(sb - mx)
        den = jnp.sum(pc, axis=-1, keepdims=True) + jnp.sum(pb, axis=-1, keepdims=True) + jnp.exp(sink - mx)
        o = (_dot(pc.astype(BF16), vc) + _dot(pb.astype(BF16), vb)) / den
        outs.extend(o[r * ATTN_BLOCK:(r + 1) * ATTN_BLOCK] for r in range(rep))
    o_ref[0] = jnp.concatenate(outs, axis=-1).astype(BF16)


def _attention(q, kv, sinks, n_batch, t_len):
    nqb = t_len // ATTN_BLOCK
    first = N_CTX // ATTN_BLOCK
    q3 = q.reshape(n_batch, t_len, D_Q)
    kv3 = kv.reshape(n_batch, t_len, 2 * D_KV)

    def band(off):
        return lambda b, j, s: (b, jnp.clip(j + off, first, nqb - 1), 0)

    out = pl.pallas_call(
        functools.partial(_attn_kernel, n_lat=t_len - N_CTX),
        grid_spec=pltpu.PrefetchScalarGridSpec(
            num_scalar_prefetch=1,
            grid=(n_batch, nqb),
            in_specs=[
                pl.BlockSpec((1, ATTN_BLOCK, D_Q), lambda b, j, s: (b, j, 0)),
                pl.BlockSpec((1, N_CTX, 2 * D_KV), lambda b, j, s: (b, 0, 0)),
                pl.BlockSpec((1, ATTN_BLOCK, 2 * D_KV), band(-1)),
                pl.BlockSpec((1, ATTN_BLOCK, 2 * D_KV), band(0)),
                pl.BlockSpec((1, ATTN_BLOCK, 2 * D_KV), band(1)),
            ],
            out_specs=pl.BlockSpec((1, ATTN_BLOCK, D_Q), lambda b, j, s: (b, j, 0)),
        ),
        out_shape=jax.ShapeDtypeStruct((n_batch, t_len, D_Q), BF16),
        compiler_params=_cparams(("parallel", "parallel")),
        name="attention",
    )(sinks, q3, kv3, kv3, kv3, kv3)
    return out.reshape(n_batch * t_len, D_Q)


def _gelu_tanh(x):
    return 0.5 * x * (1.0 + jnp.tanh(math.sqrt(2.0 / math.pi) * (x + 0.044715 * (x * x * x))))


def _merge_kernel(x_ref, yf_ref, yb_ref, u_ref, cz_ref, czp_ref, czn_ref, ya_ref, gt_ref, mod_ref,
                  d_ref, cw_ref, wglu_ref, wso_ref, wco_ref, wao_ref, wo_ref, n2g_ref,
                  wrh_ref, wrl_ref, br_ref, xo_ref, h2_ref, lg_ref, *, tiles_per_seq):
    r = pl.program_id(0) % tiles_per_seq
    ys = yf_ref[...] + yb_ref[...] + d_ref[...] * u_ref[...].astype(F32)
    z = _gelu_tanh(ys)
    glu = z * _sigmoid(_dot(z.astype(BF16), wglu_ref[...]))
    br_ssm = _dot(glu.astype(BF16), wso_ref[...])

    cz = cz_ref[...].astype(F32)
    cb = cz[:, 0:D_CONV]
    zz = cz[:, D_CONV:2 * D_CONV]
    seg_first = r <= 1
    seg_last = (r == 0) | (r == tiles_per_seq - 1)
    prev_row = jnp.where(seg_first, 0.0, czp_ref[7:8, D_CONV:2 * D_CONV].astype(F32))
    next_row = jnp.where(seg_last, 0.0, czn_ref[0:1, D_CONV:2 * D_CONV].astype(F32))
    row = lax.broadcasted_iota(I32, (TM, D_CONV), 0)
    z_dn = jnp.where(row == 0, prev_row, pltpu.roll(zz, 1, axis=0))
    z_up = jnp.where(row == TM - 1, next_row, pltpu.roll(zz, TM - 1, axis=0))
    y_conv = cb * (cw_ref[0:1, :] * z_dn + cw_ref[1:2, :] * zz + cw_ref[2:3, :] * z_up)
    br_conv = _dot(y_conv.astype(BF16), wco_ref[...])
    br_attn = _dot(ya_ref[...], wao_ref[...])

    merged = (gt_ref[:, 0:D_MODEL].astype(F32) * br_ssm
              + gt_ref[:, D_MODEL:2 * D_MODEL].astype(F32) * br_conv
              + gt_ref[:, 2 * D_MODEL:3 * D_MODEL].astype(F32) * br_attn)
    mix = _dot(merged.astype(BF16), wo_ref[...])
    m = mod_ref[0]
    xn = x_ref[...] + m[:, 2 * D_MODEL:3 * D_MODEL] * mix
    xo_ref[...] = xn
    h2 = _rms(xn, n2g_ref[...]) * (1.0 + m[:, 4 * D_MODEL:5 * D_MODEL]) + m[:, 3 * D_MODEL:4 * D_MODEL]
    hi = h2.astype(BF16)
    h2_ref[...] = hi
    lo = (h2 - hi.astype(F32)).astype(BF16)
    lg_ref[...] = _dot(hi, wrh_ref[...]) + _dot(lo, wrh_ref[...]) + _dot(hi, wrl_ref[...]) + br_ref[...]


def _merge(x, yf, yb, u_tb, cz, ya, gt, mod_tiles, d_skip, conv_w, wglu, wso, wco, wao, wo, n2g,
           wr_hi, wr_lo, b_r, tiles_per_seq):
    tok = x.shape[0]
    nt = tok // TM
    tile = lambda i: (i, 0)
    const = lambda i: (0, 0)
    tb = lambda i: (i % tiles_per_seq, i // tiles_per_seq)
    rows8 = TM // 8
    return pl.pallas_call(
        functools.partial(_merge_kernel, tiles_per_seq=tiles_per_seq),
        grid=(nt,),
        in_specs=[
            pl.BlockSpec((TM, D_MODEL), tile),
            pl.BlockSpec((TM, D_SSM), tb),
            pl.BlockSpec((TM, D_SSM), tb),
            pl.BlockSpec((TM, D_SSM), tb),
            pl.BlockSpec((TM, 2 * D_CONV), tile),
            pl.BlockSpec((8, 2 * D_CONV), lambda i: (jnp.maximum(i * rows8 - 1, 0), 0)),
            pl.BlockSpec((8, 2 * D_CONV), lambda i: (jnp.minimum((i + 1) * rows8, tok // 8 - 1), 0)),
            pl.BlockSpec((TM, D_Q), tile),
            pl.BlockSpec((TM, N_BRANCH * D_MODEL), tile),
            pl.BlockSpec((1, 1, 6 * D_MODEL), lambda i: (i, 0, 0)),
            pl.BlockSpec((1, D_SSM), const),
            pl.BlockSpec((3, D_CONV), const),
            pl.BlockSpec((D_SSM, D_SSM), const),
            pl.BlockSpec((D_SSM, D_MODEL), const),
            pl.BlockSpec((D_CONV, D_MODEL), const),
            pl.BlockSpec((D_Q, D_MODEL), const),
            pl.BlockSpec((D_MODEL, D_MODEL), const),
            pl.BlockSpec((1, D_MODEL), const),
            pl.BlockSpec((D_MODEL, LANES), const),
            pl.BlockSpec((D_MODEL, LANES), const),
            pl.BlockSpec((1, LANES), const),
        ],
        out_specs=[
            pl.BlockSpec((TM, D_MODEL), tile),
            pl.BlockSpec((TM, D_MODEL), tile),
            pl.BlockSpec((TM, LANES), tile),
        ],
        out_shape=[
            jax.ShapeDtypeStruct((tok, D_MODEL), F32),
            jax.ShapeDtypeStruct((tok, D_MODEL), BF16),
            jax.ShapeDtypeStruct((tok, LANES), F32),
        ],
        compiler_params=_cparams(("parallel",)),
        name="branch_merge",
    )(x, yf, yb, u_tb, cz, cz, cz, ya, gt, mod_tiles, d_skip, conv_w, wglu, wso, wco, wao, wo, n2g,
      wr_hi, wr_lo, b_r)


def _route_kernel(lg_ref, tri_ref, triu_ref, out_ref, cnt_ref):
    l = lg_ref[...]
    lane = lax.broadcasted_iota(I32, l.shape, 1)
    vals, idxs, hots = [], [], []
    for _ in range(TOP_K):
        mx = jnp.max(l, axis=-1, keepdims=True)
        idx = jnp.min(jnp.where(l == mx, lane, LANES), axis=-1, keepdims=True)
        hot = lane == idx
        l = jnp.where(hot, -3e38, l)
        vals.append(mx)
        idxs.append(idx)
        hots.append(hot)
    ex = [jnp.exp(v - vals[0]) for v in vals]
    den = ex[0] + ex[1] + ex[2] + ex[3]
    picked = jnp.zeros(l.shape, F32)
    for hot in hots:
        picked = picked + hot.astype(F32)
    cum = _dot(tri_ref[...], picked.astype(BF16))
    cnt = jnp.sum(picked, axis=0, keepdims=True)
    seg = jnp.floor((cnt + (SEG - 1.0)) * (1.0 / SEG)) * SEG
    run_start = _dot(jnp.broadcast_to(seg, (8, LANES)).astype(BF16), triu_ref[...])[0:1]
    pos = cum + run_start
    out = jnp.zeros(l.shape, F32)
    for k in range(TOP_K):
        row = jnp.sum(jnp.where(hots[k], pos, 0.0), axis=-1, keepdims=True)
        out = jnp.where(lane == k, idxs[k].astype(F32), out)
        out = jnp.where(lane == TOP_K + k, row, out)
        out = jnp.where(lane == 2 * TOP_K + k, ex[k] / den, out)
    out_ref[...] = out
    cnt_ref[...] = jnp.broadcast_to(cnt, (8, LANES))


def _route(logits, tri, triu):
    tok = logits.shape[0]
    nt = tok // TM
    return pl.pallas_call(
        _route_kernel,
        grid=(nt,),
        in_specs=[pl.BlockSpec((TM, LANES), lambda i: (i, 0)), pl.BlockSpec((TM, TM), lambda i: (0, 0)),
                  pl.BlockSpec((LANES, LANES), lambda i: (0, 0))],
        out_specs=[pl.BlockSpec((TM, LANES), lambda i: (i, 0)), pl.BlockSpec((8, LANES), lambda i: (i, 0))],
        out_shape=[jax.ShapeDtypeStruct((tok, LANES), F32), jax.ShapeDtypeStruct((nt * 8, LANES), F32)],
        compiler_params=_cparams(("parallel",)),
        name="route",
    )(logits, tri, triu)


def _for_each_segment(tile, ng_ref, loc_ref, gs_ref, fn):
    def per_expert(e, c):
        m = tile * N_EXPERTS + e
        loc = loc_ref[m]
        gs = gs_ref[m]

        def per_piece(g, c2):
            fn(pl.multiple_of(loc + g * SEG, SEG), pl.multiple_of(gs + g * SEG, SEG))
            return c2

        lax.fori_loop(0, ng_ref[m], per_piece, 0)
        return c

    lax.fori_loop(0, N_EXPERTS, per_expert, 0)


def _local_rows(rt):
    rows = lax.broadcasted_iota(I32, (TM, LOCAL_ROWS), 1)
    return [rows == rt[:, TOP_K + k:TOP_K + k + 1].astype(I32) for k in range(TOP_K)]


def _dispatch_kernel(ng_ref, loc_ref, gs_ref, tot_ref, zstart_ref, nu_ref, h_ref, rt_ref, xs_hbm,
                     xl_ref, zero_ref, sem_ref, sem_z, *, n_blocks, n_tiles):
    i = pl.program_id(0)
    slot = i % 2

    @pl.when(i == 0)
    def _():
        zero_ref[...] = jnp.zeros_like(zero_ref)

        def zero_copy(start):
            rows = pl.ds(pl.multiple_of(start, MOE_BLOCK), MOE_BLOCK)
            return pltpu.make_async_copy(zero_ref, xs_hbm.at[rows, :], sem_z)

        def tail_start(b, c):
            zero_copy(b * MOE_BLOCK).start()
            return c

        def tail_wait(b, c):
            zero_copy(b * MOE_BLOCK).wait()
            return c

        for e in range(N_EXPERTS):
            @pl.when(zstart_ref[e] >= 0)
            def _():
                zero_copy(zstart_ref[e]).start()
        lax.fori_loop(nu_ref[0], n_blocks, tail_start, 0)
        for e in range(N_EXPERTS):
            @pl.when(zstart_ref[e] >= 0)
            def _():
                zero_copy(zstart_ref[e]).wait()
        lax.fori_loop(nu_ref[0], n_blocks, tail_wait, 0)

    place = jnp.zeros((TM, LOCAL_ROWS), F32)
    for hit in _local_rows(rt_ref[...]):
        place = place + hit.astype(F32)
    xl_ref[slot] = lax.dot_general(place.astype(BF16), h_ref[...], (((0,), (0,)), ((), ())),
                                   preferred_element_type=F32)

    def seg_copy(local_row, global_row, s):
        return pltpu.make_async_copy(xl_ref.at[s, pl.ds(local_row, SEG), :],
                                     xs_hbm.at[pl.ds(global_row, SEG), :], sem_ref.at[s])

    def wait_tile(tile, s):
        def one(g, c):
            seg_copy(0, 0, s).wait()
            return c

        lax.fori_loop(0, tot_ref[tile], one, 0)

    _for_each_segment(i, ng_ref, loc_ref, gs_ref, lambda lr, gr: seg_copy(lr, gr, slot).start())

    @pl.when(i > 0)
    def _():
        wait_tile(i - 1, 1 - slot)

    @pl.when(i == n_tiles - 1)
    def _():
        wait_tile(i, slot)


def _dispatch(h2, route, seg_meta, zstart, n_used, n_slots):
    tok = h2.shape[0]
    nt = tok // TM
    tile = lambda i, *_: (i, 0)
    return pl.pallas_call(
        functools.partial(_dispatch_kernel, n_blocks=n_slots // MOE_BLOCK, n_tiles=nt),
        grid_spec=pltpu.PrefetchScalarGridSpec(
            num_scalar_prefetch=6,
            grid=(nt,),
            in_specs=[
                pl.BlockSpec((TM, D_MODEL), tile),
                pl.BlockSpec((TM, LANES), tile),
            ],
            out_specs=pl.BlockSpec(memory_space=pl.ANY),
            scratch_shapes=[
                pltpu.VMEM((2, LOCAL_ROWS, D_MODEL), F32),
                pltpu.VMEM((MOE_BLOCK, D_MODEL), F32),
                pltpu.SemaphoreType.DMA((2,)),
                pltpu.SemaphoreType.DMA,
            ],
        ),
        out_shape=jax.ShapeDtypeStruct((n_slots, D_MODEL), F32),
        compiler_params=_cparams(("arbitrary",)),
        name="dispatch",
    )(*seg_meta, zstart, n_used, h2, route)


DEINT = 256


def _expert_kernel(be_ref, nu_ref, x_ref, wgu_ref, bg_ref, bu_ref, wd_ref, bd_ref, sg_ref, su_ref, y_ref,
                   wg_s, wu_s, wd_s):
    i = pl.program_id(0)
    prev = be_ref[jnp.maximum(i - 1, 0)]

    @pl.when((i == 0) | (be_ref[i] != prev))
    def _():
        for c in range(2 * D_EXPERT // DEINT):
            w = wgu_ref[0, :, c * DEINT:(c + 1) * DEINT].astype(BF16)
            cols = pl.ds(c * (DEINT // 2), DEINT // 2)
            wg_s[:, cols] = _dot(w, sg_ref[...]).astype(BF16)
            wu_s[:, cols] = _dot(w, su_ref[...]).astype(BF16)
        wd_s[...] = wd_ref[0].astype(BF16)

    @pl.when(i < nu_ref[0])
    def _():
        x = x_ref[...].astype(BF16)
        g = _dot(x, wg_s[...]) + bg_ref[0]
        u = _dot(x, wu_s[...]) + bu_ref[0]
        glu = jnp.minimum(g, SWIGLU_LIMIT)
        up = jnp.clip(u, -SWIGLU_LIMIT, SWIGLU_LIMIT)
        act = glu * _sigmoid(SWIGLU_ALPHA * glu) * (up + 1.0)
        y_ref[...] = _dot(act.astype(BF16), wd_s[...]) + bd_ref[0]

    @pl.when(i >= nu_ref[0])
    def _():
        y_ref[...] = jnp.zeros_like(y_ref)


def _experts(xs, block_expert, n_used, wgu, bg, bu, wd, bd, sel_g, sel_u):
    n_slots = xs.shape[0]
    n_blocks = n_slots // MOE_BLOCK
    wmap = lambda i, be, nu: (be[i], 0, 0)
    const = lambda i, be, nu: (0, 0)
    return pl.pallas_call(
        _expert_kernel,
        grid_spec=pltpu.PrefetchScalarGridSpec(
            num_scalar_prefetch=2,
            grid=(n_blocks,),
            in_specs=[
                pl.BlockSpec((MOE_BLOCK, D_MODEL), lambda i, be, nu: (jnp.minimum(i, nu[0] - 1), 0)),
                pl.BlockSpec((1, D_MODEL, 2 * D_EXPERT), wmap),
                pl.BlockSpec((1, 1, D_EXPERT), wmap),
                pl.BlockSpec((1, 1, D_EXPERT), wmap),
                pl.BlockSpec((1, D_EXPERT, D_MODEL), wmap),
                pl.BlockSpec((1, 1, D_MODEL), wmap),
                pl.BlockSpec((DEINT, DEINT // 2), const),
                pl.BlockSpec((DEINT, DEINT // 2), const),
            ],
            out_specs=pl.BlockSpec((MOE_BLOCK, D_MODEL), lambda i, be, nu: (i, 0)),
            scratch_shapes=[
                pltpu.VMEM((D_MODEL, D_EXPERT), BF16),
                pltpu.VMEM((D_MODEL, D_EXPERT), BF16),
                pltpu.VMEM((D_EXPERT, D_MODEL), BF16),
            ],
        ),
        out_shape=jax.ShapeDtypeStruct((n_slots, D_MODEL), F32),
        compiler_params=_cparams(("arbitrary",)),
        name="experts",
    )(block_expert, n_used, xs, wgu, bg, bu, wd, bd, sel_g, sel_u)


def _combine_kernel(ng_ref, loc_ref, gs_ref, tot_ref, ys_hbm, x_ref, rt_ref, mod_ref, o_ref, yl_ref, sem_ref,
                    *, n_tiles):
    i = pl.program_id(0)
    slot = i % 2

    def seg_copy(local_row, global_row, s):
        return pltpu.make_async_copy(ys_hbm.at[pl.ds(global_row, SEG), :],
                                     yl_ref.at[s, pl.ds(local_row, SEG), :], sem_ref.at[s])

    def fetch_tile(tile, s):
        _for_each_segment(tile, ng_ref, loc_ref, gs_ref, lambda lr, gr: seg_copy(lr, gr, s).start())

    @pl.when(i == 0)
    def _():
        yl_ref[...] = jnp.zeros_like(yl_ref)
        fetch_tile(0, 0)

    @pl.when(i + 1 < n_tiles)
    def _():
        fetch_tile(i + 1, 1 - slot)

    def one(g, c):
        seg_copy(0, 0, slot).wait()
        return c

    lax.fori_loop(0, tot_ref[i], one, 0)

    rt = rt_ref[...]
    weights = jnp.zeros((TM, LOCAL_ROWS), F32)
    for k, hit in enumerate(_local_rows(rt)):
        weights = weights + jnp.where(hit, rt[:, 2 * TOP_K + k:2 * TOP_K + k + 1], 0.0)
    w_hi = weights.astype(BF16)
    w_lo = (weights - w_hi.astype(F32)).astype(BF16)
    yl = yl_ref[slot].astype(BF16)
    y = _dot(w_hi, yl) + _dot(w_lo, yl)
    o_ref[...] = x_ref[...] + mod_ref[0][:, 5 * D_MODEL:6 * D_MODEL] * y


def _combine(x, ys, route, seg_meta, mod_tiles):
    tok = x.shape[0]
    nt = tok // TM
    tile = lambda i, *_: (i, 0)
    return pl.pallas_call(
        functools.partial(_combine_kernel, n_tiles=nt),
        grid_spec=pltpu.PrefetchScalarGridSpec(
            num_scalar_prefetch=4,
            grid=(nt,),
            in_specs=[
                pl.BlockSpec(memory_space=pl.ANY),
                pl.BlockSpec((TM, D_MODEL), tile),
                pl.BlockSpec((TM, LANES), tile),
                pl.BlockSpec((1, 1, 6 * D_MODEL), lambda i, *_: (i, 0, 0)),
            ],
            out_specs=pl.BlockSpec((TM, D_MODEL), tile),
            scratch_shapes=[
                pltpu.VMEM((2, LOCAL_ROWS, D_MODEL), F32),
                pltpu.SemaphoreType.DMA((2,)),
            ],
        ),
        out_shape=jax.ShapeDtypeStruct((tok, D_MODEL), F32),
        compiler_params=_cparams(("arbitrary",)),
        name="combine",
    )(*seg_meta, ys, x, route, mod_tiles)


def _rope_tables(t_len):
    n_lat = t_len - N_CTX
    t = np.arange(n_lat)
    n_pairs = HEAD_DIM // 4
    inv_freq = jnp.asarray(ROPE_BASE, F32) ** (-jnp.arange(n_pairs, dtype=F32) / n_pairs)
    row = jnp.asarray(t // GRID_W, F32)
    col = jnp.asarray(t % GRID_W, F32)
    ang = jnp.concatenate([row[:, None] * inv_freq, col[:, None] * inv_freq], axis=-1)
    ang = jnp.concatenate([jnp.zeros((N_CTX, HEAD_DIM // 2), F32), ang], axis=0)
    cos = jnp.tile(jnp.cos(ang), (1, LANES // (HEAD_DIM // 2)))
    sin = jnp.tile(jnp.sin(ang), (1, LANES // (HEAD_DIM // 2)))
    return cos, sin


def _s5_operands(lam_re, lam_im, log_dt, b_re, b_im, c_re, c_im):
    lam = lax.complex(lam_re, lam_im)
    dt = jnp.exp(log_dt)[..., None]
    lam_bar = jnp.exp(lam * dt)
    b_bar = ((lam_bar - 1.0) / lam)[..., None] * lax.complex(b_re, b_im)
    eye = jnp.eye(N_SSM_GROUPS, dtype=F32)

    def in_block(w):
        return jnp.einsum("zgph,gk->zghkp", w, eye).reshape(2, D_SSM, N_STATE)

    def out_block(w):
        return jnp.einsum("zghp,gk->zgpkh", w, eye).reshape(2, N_STATE, D_SSM)

    bd = jnp.concatenate([in_block(b_bar.real), in_block(b_bar.imag)], axis=-1).astype(BF16)
    cd = jnp.concatenate([out_block(c_re), out_block(-c_im)], axis=1).astype(BF16)
    lam_rows = jnp.stack([lam_bar[0].real.reshape(-1), lam_bar[0].imag.reshape(-1),
                          lam_bar[1].real.reshape(-1), lam_bar[1].imag.reshape(-1)])
    return bd, cd, lam_rows


def _moe_constants():
    tri = jnp.asarray(np.tril(np.ones((TM, TM)), -1), BF16)
    triu = jnp.asarray(np.triu(np.ones((LANES, LANES)), 1), BF16)
    pick = np.arange(DEINT)[:, None] == 2 * np.arange(DEINT // 2)[None, :]
    sel_g = jnp.asarray(pick, BF16)
    sel_u = jnp.asarray(np.roll(pick, 1, axis=0), BF16)
    return tri, triu, sel_g, sel_u


def _moe(xs, h2, logits, mod_tiles, w_gate_up, b_gate_up, w_down, b_down, consts):
    tri, triu, sel_g, sel_u = consts
    tok = xs.shape[0]
    nt = tok // TM
    n_blocks = (tok * TOP_K + nt * N_EXPERTS * (SEG - 1) + N_EXPERTS * (MOE_BLOCK - 1)) // MOE_BLOCK
    n_slots = n_blocks * MOE_BLOCK

    route, counts = _route(logits, tri, triu)
    counts = counts.reshape(nt, 8, LANES)[:, 0, :N_EXPERTS].astype(I32)
    seg = (counts + SEG - 1) // SEG * SEG
    region = jnp.sum(seg, axis=0)
    padded = (region + MOE_BLOCK - 1) // MOE_BLOCK * MOE_BLOCK
    pad_end = jnp.cumsum(padded)
    pad_start = pad_end - padded
    run_global = pad_start[None, :] + jnp.cumsum(seg, axis=0) - seg
    run_local = jnp.cumsum(seg, axis=1) - seg
    pieces = seg // SEG
    seg_meta = (pieces.reshape(-1), run_local.reshape(-1), run_global.reshape(-1), jnp.sum(pieces, axis=1))
    n_used = (pad_end[-1] // MOE_BLOCK).astype(I32)
    blk = jnp.minimum(jnp.arange(n_blocks, dtype=I32), n_used - 1) * MOE_BLOCK
    block_expert = jnp.minimum(jnp.sum(blk[:, None] >= pad_end[None, :], axis=1), N_EXPERTS - 1).astype(I32)
    zstart = jnp.where(region > 0, pad_end - MOE_BLOCK, -1).astype(I32)

    x_sorted = _dispatch(h2, route, seg_meta, zstart, n_used.reshape(1), n_slots)
    y_sorted = _experts(x_sorted, block_expert, n_used.reshape(1), w_gate_up,
                        b_gate_up[:, None, 0::2], b_gate_up[:, None, 1::2],
                        w_down, b_down[:, None, :], sel_g, sel_u)
    return _combine(xs, y_sorted, route, seg_meta, mod_tiles)


def kernel(x, c, ctx, c_ctx, w_mod, b_mod, norm1_g, norm2_g, w_in, ssm_lam_re, ssm_lam_im, ssm_log_dt, ssm_b_re, ssm_b_im, ssm_c_re, ssm_c_im, ssm_d, w_glu, w_ssm_out, conv_w, w_conv_out, q_norm_g, k_norm_g, attn_sinks, w_attn_out, w_o, w_router, b_router, w_gate_up, b_gate_up, w_down, b_down):
    n_batch, n_lat, _ = x.shape
    t_len = N_CTX + n_lat
    tiles_per_seq = t_len // TM
    tok = n_batch * t_len
    nt = tok // TM

    xs = jnp.concatenate([ctx, x], axis=1).reshape(tok, D_MODEL)

    mod_rows = 8 * ((n_batch + 1 + 7) // 8)
    cvec = jnp.zeros((mod_rows, D_MODEL), F32).at[:n_batch].set(c).at[n_batch].set(c_ctx)
    mod_all = _modulation(cvec, w_mod, b_mod)
    tile_ids = np.arange(nt)
    tile_row = np.where(tile_ids % tiles_per_seq == 0, n_batch, tile_ids // tiles_per_seq)

    cos_t, sin_t = _rope_tables(t_len)
    head_sum = jnp.asarray(np.kron(np.eye(N_Q_HEADS), np.full((HEAD_DIM, HEAD_DIM), 1.0 / HEAD_DIM)), BF16)
    moe_consts = _moe_constants()

    for l in range(DEPTH):
        mod_tiles = mod_all[l][tile_row].reshape(nt, 1, 6 * D_MODEL)
        qg = jnp.tile(q_norm_g[l], N_Q_HEADS).reshape(1, D_Q)
        kg = jnp.tile(k_norm_g[l], N_KV_HEADS).reshape(1, D_KV)
        u_tb, cz, q, kv, gt = _in_proj(xs, mod_tiles, norm1_g[l].reshape(1, D_MODEL), w_in[l].astype(BF16),
                                       cos_t, sin_t, qg, kg, head_sum, n_batch, tiles_per_seq)

        bd, cd, lam_rows = _s5_operands(ssm_lam_re[l], ssm_lam_im[l], ssm_log_dt[l], ssm_b_re[l], ssm_b_im[l],
                                        ssm_c_re[l], ssm_c_im[l])
        yf, yb = _s5(u_tb.reshape(t_len * n_batch, D_SSM), bd, cd, lam_rows, n_batch, t_len)
        ya = _attention(q, kv, attn_sinks[l], n_batch, t_len)

        wr = jnp.zeros((D_MODEL, LANES), F32).at[:, :N_EXPERTS].set(w_router[l])
        wr_hi = wr.astype(BF16)
        wr_lo = (wr - wr_hi.astype(F32)).astype(BF16)
        b_r = jnp.full((1, LANES), NEG, F32).at[0, :N_EXPERTS].set(b_router[l])
        xs, h2, logits = _merge(xs, yf.reshape(t_len, n_batch * D_SSM), yb.reshape(t_len, n_batch * D_SSM), u_tb,
                                cz, ya, gt, mod_tiles, ssm_d[l].reshape(1, D_SSM), conv_w[l],
                                w_glu[l].astype(BF16), w_ssm_out[l].astype(BF16), w_conv_out[l].astype(BF16),
                                w_attn_out[l].astype(BF16), w_o[l].astype(BF16), norm2_g[l].reshape(1, D_MODEL),
                                wr_hi, wr_lo, b_r, tiles_per_seq)

        xs = _moe(xs, h2, logits, mod_tiles, w_gate_up[l], b_gate_up[l], w_down[l], b_down[l], moe_consts)

    return xs.reshape(n_batch, t_len, D_MODEL)[:, N_CTX:, :]
```

```python
import functools
import math

import jax
import jax.numpy as jnp
import numpy as np
from jax import lax
from jax.experimental import pallas as pl
from jax.experimental.pallas import tpu as pltpu

F32 = jnp.float32
BF16 = jnp.bfloat16
I32 = jnp.int32

D_MODEL = 1024
DEPTH = 4
N_CTX = 256
HEAD_DIM = 64
N_Q_HEADS = 8
N_KV_HEADS = 2
D_Q = N_Q_HEADS * HEAD_DIM
D_KV = N_KV_HEADS * HEAD_DIM
WINDOW = 128
ATTN_BLOCK = 128
ROPE_BASE = 10000.0
GRID_W = 64
D_SSM = 256
SSM_GROUP = 16
N_SSM_GROUPS = 16
SSM_STATE = 64
N_STATE = N_SSM_GROUPS * SSM_STATE
D_CONV = 256
N_BRANCH = 3
D_IN = D_SSM + 3 * D_CONV + D_Q + 2 * D_KV + N_BRANCH * D_MODEL
N_EXPERTS = 32
TOP_K = 4
D_EXPERT = 1024
SWIGLU_LIMIT = 7.0
SWIGLU_ALPHA = 1.702
EPS = 1e-6

LANES = 128
TM = 256
MOE_BLOCK = 256
SEG = 8
LOCAL_ROWS = -(-(TM * TOP_K + N_EXPERTS * (SEG - 1)) // LANES) * LANES
S5_CHUNK = 64
S5_COLS = 512
NEG = -1e30
VMEM_LIMIT = 56 * 1024 * 1024


def _cparams(sem):
    return pltpu.CompilerParams(dimension_semantics=sem, vmem_limit_bytes=VMEM_LIMIT)


def _dot(a, b):
    return jnp.dot(a, b, preferred_element_type=F32)


def _sigmoid(x):
    return 1.0 / (1.0 + jnp.exp(-x))


def _rms(x, g):
    ms = jnp.mean(x * x, axis=-1, keepdims=True)
    return x * lax.rsqrt(ms + EPS) * g


def _mod_kernel(c_ref, w_ref, b_ref, o_ref):
    c = c_ref[...]
    s = (c * _sigmoid(c)).astype(BF16)
    o_ref[0] = _dot(s, w_ref[0].astype(BF16)) + b_ref[0]


def _modulation(cvec, w_mod, b_mod):
    rows = cvec.shape[0]
    nblk = 1536
    return pl.pallas_call(
        _mod_kernel,
        grid=(DEPTH, 6 * D_MODEL // nblk),
        in_specs=[
            pl.BlockSpec((rows, D_MODEL), lambda l, j: (0, 0)),
            pl.BlockSpec((1, D_MODEL, nblk), lambda l, j: (l, 0, j)),
            pl.BlockSpec((1, 1, nblk), lambda l, j: (l, 0, j)),
        ],
        out_specs=pl.BlockSpec((1, rows, nblk), lambda l, j: (l, 0, j)),
        out_shape=jax.ShapeDtypeStruct((DEPTH, rows, 6 * D_MODEL), F32),
        compiler_params=_cparams(("arbitrary", "arbitrary")),
        name="modulation",
    )(cvec, w_mod, b_mod.reshape(DEPTH, 1, 6 * D_MODEL))


def _assemble_kernel(c_ref, x_ref, o_ref, *, tiles_per_seq):
    r = pl.program_id(0) % tiles_per_seq

    @pl.when(r == 0)
    def _():
        o_ref[...] = c_ref[0]

    @pl.when(r > 0)
    def _():
        o_ref[...] = x_ref[0]


def _assemble(ctx, x, tiles_per_seq):
    n_batch = x.shape[0]
    nt = n_batch * tiles_per_seq
    return pl.pallas_call(
        functools.partial(_assemble_kernel, tiles_per_seq=tiles_per_seq),
        grid=(nt,),
        in_specs=[
            pl.BlockSpec((1, TM, D_MODEL), lambda i: (i // tiles_per_seq, 0, 0)),
            pl.BlockSpec((1, TM, D_MODEL), lambda i: (i // tiles_per_seq, jnp.maximum(i % tiles_per_seq - 1, 0), 0)),
        ],
        out_specs=pl.BlockSpec((TM, D_MODEL), lambda i: (i, 0)),
        out_shape=jax.ShapeDtypeStruct((nt * TM, D_MODEL), F32),
        compiler_params=_cparams(("arbitrary",)),
        name="assemble",
    )(ctx, x)


def _rot_half(x, width):
    lane = lax.broadcasted_iota(I32, x.shape, 1)
    first = (lane % HEAD_DIM) < (HEAD_DIM // 2)
    return jnp.where(first, -pltpu.roll(x, width - HEAD_DIM // 2, axis=1), pltpu.roll(x, HEAD_DIM // 2, axis=1))


def _inproj_kernel(x_ref, mod_ref, g_ref, w_ref, cos_ref, sin_ref, qg_ref, kg_ref, hs_ref,
                   u_ref, cz_ref, q_ref, kv_ref, gt_ref):
    m = mod_ref[0]
    h = _rms(x_ref[...], g_ref[...])
    h = (h * (1.0 + m[:, D_MODEL:2 * D_MODEL]) + m[:, 0:D_MODEL]).astype(BF16)
    o_gate = D_SSM + 3 * D_CONV + D_Q + 2 * D_KV
    y = _dot(h, w_ref[:, 0:o_gate])
    u_ref[...] = y[:, 0:D_SSM].astype(BF16)
    cb = y[:, D_SSM:D_SSM + D_CONV]
    cc = y[:, D_SSM + D_CONV:D_SSM + 2 * D_CONV]
    cx = y[:, D_SSM + 2 * D_CONV:D_SSM + 3 * D_CONV]
    cz_ref[...] = jnp.concatenate([cb, cc * cx], axis=-1).astype(BF16)
    o_q = D_SSM + 3 * D_CONV
    q = y[:, o_q:o_q + D_Q]
    k = y[:, o_q + D_Q:o_q + D_Q + D_KV]
    v = y[:, o_q + D_Q + D_KV:o_gate]
    cos = cos_ref[...]
    sin = sin_ref[...]
    q_ms = _dot((q * q).astype(BF16), hs_ref[...])
    qn = q * lax.rsqrt(q_ms + EPS) * qg_ref[...]
    cos_q = jnp.concatenate([cos] * (D_Q // LANES), axis=-1)
    sin_q = jnp.concatenate([sin] * (D_Q // LANES), axis=-1)
    qr = qn * cos_q + _rot_half(qn, D_Q) * sin_q
    q_ref[...] = (qr * (HEAD_DIM ** -0.5)).astype(BF16)
    k_ms = _dot((k * k).astype(BF16), hs_ref[0:D_KV, 0:D_KV])
    kn = k * lax.rsqrt(k_ms + EPS) * kg_ref[...]
    kr = kn * cos + _rot_half(kn, D_KV) * sin
    kv_ref[...] = jnp.concatenate([kr, v], axis=-1).astype(BF16)
    gt = _dot(h, w_ref[:, o_gate:D_IN])
    gt_ref[...] = _sigmoid(gt).astype(BF16)


def _in_proj(x, mod_tiles, norm_g, w_in_bf, cos_t, sin_t, qg, kg, head_sum, n_batch, tiles_per_seq):
    tok = x.shape[0]
    nt = tok // TM
    t_len = tiles_per_seq * TM
    tile = lambda i: (i, 0)
    const = lambda i: (0, 0)
    seq_tile = lambda i: (i % tiles_per_seq, 0)
    return pl.pallas_call(
        _inproj_kernel,
        grid=(nt,),
        in_specs=[
            pl.BlockSpec((TM, D_MODEL), tile),
            pl.BlockSpec((1, 1, 6 * D_MODEL), lambda i: (i, 0, 0)),
            pl.BlockSpec((1, D_MODEL), const),
            pl.BlockSpec((D_MODEL, D_IN), const),
            pl.BlockSpec((TM, LANES), seq_tile),
            pl.BlockSpec((TM, LANES), seq_tile),
            pl.BlockSpec((1, D_Q), const),
            pl.BlockSpec((1, D_KV), const),
            pl.BlockSpec((D_Q, D_Q), const),
        ],
        out_specs=[
            pl.BlockSpec((TM, D_SSM), lambda i: (i % tiles_per_seq, i // tiles_per_seq)),
            pl.BlockSpec((TM, 2 * D_CONV), tile),
            pl.BlockSpec((TM, D_Q), tile),
            pl.BlockSpec((TM, 2 * D_KV), tile),
            pl.BlockSpec((TM, N_BRANCH * D_MODEL), tile),
        ],
        out_shape=[
            jax.ShapeDtypeStruct((t_len, n_batch * D_SSM), BF16),
            jax.ShapeDtypeStruct((tok, 2 * D_CONV), BF16),
            jax.ShapeDtypeStruct((tok, D_Q), BF16),
            jax.ShapeDtypeStruct((tok, 2 * D_KV), BF16),
            jax.ShapeDtypeStruct((tok, N_BRANCH * D_MODEL), BF16),
        ],
        compiler_params=_cparams(("parallel",)),
        name="in_proj",
    )(x, mod_tiles, norm_g, w_in_bf, cos_t, sin_t, qg, kg, head_sum)


def _s5_kernel(uf_ref, ub_ref, bd_ref, cd_ref, lam_ref, yf_ref, yb_ref, sf_ref, sb_ref, carry_ref, *, n_batch):
    i = pl.program_id(0)

    @pl.when(i == 0)
    def _():
        carry_ref[...] = jnp.zeros_like(carry_ref)

    for z, (u_ref, s_ref, y_ref) in enumerate(((uf_ref, sf_ref, yf_ref), (ub_ref, sb_ref, yb_ref))):
        s_ref[...] = _dot(u_ref[...], bd_ref[z])
        for j in range(N_STATE // S5_COLS):
            re_cols = pl.ds(j * S5_COLS, S5_COLS)
            im_cols = pl.ds(N_STATE + j * S5_COLS, S5_COLS)
            lr = jnp.broadcast_to(lam_ref[2 * z:2 * z + 1, re_cols], (n_batch, S5_COLS))
            li = jnp.broadcast_to(lam_ref[2 * z + 1:2 * z + 2, re_cols], (n_batch, S5_COLS))

            def step(s, c, s_ref=s_ref, z=z, re_cols=re_cols, im_cols=im_cols, lr=lr, li=li):
                t = s if z == 0 else S5_CHUNK - 1 - s
                rows = pl.ds(pl.multiple_of(t * n_batch, n_batch), n_batch)
                xr, xi = c
                nr = lr * xr - li * xi + s_ref[rows, re_cols]
                ni = lr * xi + li * xr + s_ref[rows, im_cols]
                s_ref[rows, re_cols] = nr
                s_ref[rows, im_cols] = ni
                return nr, ni

            c0 = (carry_ref[2 * z, :, re_cols], carry_ref[2 * z + 1, :, re_cols])
            fr, fi = lax.fori_loop(0, S5_CHUNK, step, c0, unroll=True)
            carry_ref[2 * z, :, re_cols] = fr
            carry_ref[2 * z + 1, :, re_cols] = fi
        y_ref[...] = _dot(s_ref[...].astype(BF16), cd_ref[z])


def _s5(u_rows, bd, cd, lam, n_batch, t_len):
    rows = S5_CHUNK * n_batch
    n_chunks = t_len // S5_CHUNK
    ctx_chunks = N_CTX // S5_CHUNK

    def bwd_block(i):
        return (jnp.where(i < ctx_chunks, ctx_chunks - 1 - i, n_chunks - 1 + ctx_chunks - i), 0)

    return pl.pallas_call(
        functools.partial(_s5_kernel, n_batch=n_batch),
        grid=(n_chunks,),
        in_specs=[
            pl.BlockSpec((rows, D_SSM), lambda i: (i, 0)),
            pl.BlockSpec((rows, D_SSM), bwd_block),
            pl.BlockSpec((2, D_SSM, 2 * N_STATE), lambda i: (0, 0, 0)),
            pl.BlockSpec((2, 2 * N_STATE, D_SSM), lambda i: (0, 0, 0)),
            pl.BlockSpec((4, N_STATE), lambda i: (0, 0)),
        ],
        out_specs=[
            pl.BlockSpec((rows, D_SSM), lambda i: (i, 0)),
            pl.BlockSpec((rows, D_SSM), bwd_block),
        ],
        out_shape=[jax.ShapeDtypeStruct((t_len * n_batch, D_SSM), F32)] * 2,
        scratch_shapes=[
            pltpu.VMEM((rows, 2 * N_STATE), F32),
            pltpu.VMEM((rows, 2 * N_STATE), F32),
            pltpu.VMEM((4, n_batch, N_STATE), F32),
        ],
        compiler_params=_cparams(("arbitrary",)),
        name="s5_scan",
    )(u_rows, u_rows, bd, cd, lam)


def _attn_kernel(sink_ref, q_ref, kvc_ref, kv0_ref, kv1_ref, kv2_ref, o_ref, *, n_lat):
    n = pl.program_id(1) - N_CTX // ATTN_BLOCK
    rep = N_Q_HEADS // N_KV_HEADS
    rows = rep * ATTN_BLOCK
    q = q_ref[0]
    kvc = kvc_ref[0]
    band = jnp.concatenate([kv0_ref[0], kv1_ref[0], kv2_ref[0]], axis=0)
    iq = lax.broadcasted_iota(I32, (rows, 3 * ATTN_BLOCK), 0) % ATTN_BLOCK
    ik = lax.broadcasted_iota(I32, (rows, 3 * ATTN_BLOCK), 1)
    qpos = n * ATTN_BLOCK + iq
    kpos = (n - 1) * ATTN_BLOCK + ik
    valid = (jnp.abs(qpos - kpos) <= WINDOW) & (kpos >= 0) & (kpos < n_lat) & (n >= 0)
    head_of_row = lax.broadcasted_iota(I32, (rows, 1), 0) // ATTN_BLOCK
    contract_last = (((1,), (1,)), ((), ()))
    outs = []
    for g in range(N_KV_HEADS):
        qg = jnp.concatenate([q[:, (g * rep + r) * HEAD_DIM:(g * rep + r + 1) * HEAD_DIM] for r in range(rep)], axis=0)
        sink = jnp.zeros((rows, 1), F32)
        for r in range(rep):
            sink = jnp.where(head_of_row == r, sink_ref[g * rep + r], sink)
        kc = kvc[:, g * HEAD_DIM:(g + 1) * HEAD_DIM]
        vc = kvc[:, D_KV + g * HEAD_DIM:D_KV + (g + 1) * HEAD_DIM]
        kb = band[:, g * HEAD_DIM:(g + 1) * HEAD_DIM]
        vb = band[:, D_KV + g * HEAD_DIM:D_KV + (g + 1) * HEAD_DIM]
        sc = lax.dot_general(qg, kc, contract_last, preferred_element_type=F32)
        sb = lax.dot_general(qg, kb, contract_last, preferred_element_type=F32)
        sb = jnp.where(valid, sb, NEG)
        mx = jnp.maximum(jnp.maximum(jnp.max(sc, axis=-1, keepdims=True), jnp.max(sb, axis=-1, keepdims=True)), sink)
        pc = jnp.exp(sc - mx)
        pb = jnp.exp(sb - mx)
        den = jnp.sum(pc, axis=-1, keepdims=True) + jnp.sum(pb, axis=-1, keepdims=True) + jnp.exp(sink - mx)
        o = (_dot(pc.astype(BF16), vc) + _dot(pb.astype(BF16), vb)) / den
        outs.extend(o[r * ATTN_BLOCK:(r + 1) * ATTN_BLOCK] for r in range(rep))
    o_ref[0] = jnp.concatenate(outs, axis=-1).astype(BF16)


def _attention(q, kv, sinks, n_batch, t_len):
    nqb = t_len // ATTN_BLOCK
    first = N_CTX // ATTN_BLOCK
    q3 = q.reshape(n_batch, t_len, D_Q)
    kv3 = kv.reshape(n_batch, t_len, 2 * D_KV)

    def band(off):
        return lambda b, j, s: (b, jnp.clip(j + off, first, nqb - 1), 0)

    out = pl.pallas_call(
        functools.partial(_attn_kernel, n_lat=t_len - N_CTX),
        grid_spec=pltpu.PrefetchScalarGridSpec(
            num_scalar_prefetch=1,
            grid=(n_batch, nqb),
            in_specs=[
                pl.BlockSpec((1, ATTN_BLOCK, D_Q), lambda b, j, s: (b, j, 0)),
                pl.BlockSpec((1, N_CTX, 2 * D_KV), lambda b, j, s: (b, 0, 0)),
                pl.BlockSpec((1, ATTN_BLOCK, 2 * D_KV), band(-1)),
                pl.BlockSpec((1, ATTN_BLOCK, 2 * D_KV), band(0)),
                pl.BlockSpec((1, ATTN_BLOCK, 2 * D_KV), band(1)),
            ],
            out_specs=pl.BlockSpec((1, ATTN_BLOCK, D_Q), lambda b, j, s: (b, j, 0)),
        ),
        out_shape=jax.ShapeDtypeStruct((n_batch, t_len, D_Q), BF16),
        compiler_params=_cparams(("parallel", "parallel")),
        name="attention",
    )(sinks, q3, kv3, kv3, kv3, kv3)
    return out.reshape(n_batch * t_len, D_Q)


def _gelu_tanh(x):
    return 0.5 * x * (1.0 + jnp.tanh(math.sqrt(2.0 / math.pi) * (x + 0.044715 * (x * x * x))))


def _merge_kernel(x_ref, yf_ref, yb_ref, u_ref, cz_ref, czp_ref, czn_ref, ya_ref, gt_ref, mod_ref,
                  d_ref, cw_ref, wglu_ref, wso_ref, wco_ref, wao_ref, wo_ref, n2g_ref,
                  wrh_ref, wrl_ref, br_ref, tri_ref, triu_ref, xo_ref, h2_ref, rt_ref, cnt_ref, *, tiles_per_seq):
    r = pl.program_id(0) % tiles_per_seq
    ys = yf_ref[...] + yb_ref[...] + d_ref[...] * u_ref[...].astype(F32)
    z = _gelu_tanh(ys)
    glu = z * _sigmoid(_dot(z.astype(BF16), wglu_ref[...]))
    br_ssm = _dot(glu.astype(BF16), wso_ref[...])

    cz = cz_ref[...].astype(F32)
    cb = cz[:, 0:D_CONV]
    zz = cz[:, D_CONV:2 * D_CONV]
    seg_first = r <= 1
    seg_last = (r == 0) | (r == tiles_per_seq - 1)
    prev_row = jnp.where(seg_first, 0.0, czp_ref[7:8, D_CONV:2 * D_CONV].astype(F32))
    next_row = jnp.where(seg_last, 0.0, czn_ref[0:1, D_CONV:2 * D_CONV].astype(F32))
    row = lax.broadcasted_iota(I32, (TM, D_CONV), 0)
    z_dn = jnp.where(row == 0, prev_row, pltpu.roll(zz, 1, axis=0))
    z_up = jnp.where(row == TM - 1, next_row, pltpu.roll(zz, TM - 1, axis=0))
    y_conv = cb * (cw_ref[0:1, :] * z_dn + cw_ref[1:2, :] * zz + cw_ref[2:3, :] * z_up)
    br_conv = _dot(y_conv.astype(BF16), wco_ref[...])
    br_attn = _dot(ya_ref[...], wao_ref[...])

    merged = (gt_ref[:, 0:D_MODEL].astype(F32) * br_ssm
              + gt_ref[:, D_MODEL:2 * D_MODEL].astype(F32) * br_conv
              + gt_ref[:, 2 * D_MODEL:3 * D_MODEL].astype(F32) * br_attn)
    mix = _dot(merged.astype(BF16), wo_ref[...])
    m = mod_ref[0]
    xn = x_ref[...] + m[:, 2 * D_MODEL:3 * D_MODEL] * mix
    xo_ref[...] = xn
    h2 = _rms(xn, n2g_ref[...]) * (1.0 + m[:, 4 * D_MODEL:5 * D_MODEL]) + m[:, 3 * D_MODEL:4 * D_MODEL]
    hi = h2.astype(BF16)
    h2_ref[...] = hi
    lo = (h2 - hi.astype(F32)).astype(BF16)
    logits = _dot(hi, wrh_ref[...]) + _dot(lo, wrh_ref[...]) + _dot(hi, wrl_ref[...]) + br_ref[...]
    rt_ref[...], cnt_ref[...] = _route_tile(logits, tri_ref, triu_ref)


def _merge(x, yf, yb, u_tb, cz, ya, gt, mod_tiles, d_skip, conv_w, wglu, wso, wco, wao, wo, n2g,
           wr_hi, wr_lo, b_r, tri, triu, tiles_per_seq):
    tok = x.shape[0]
    nt = tok // TM
    tile = lambda i: (i, 0)
    const = lambda i: (0, 0)
    tb = lambda i: (i % tiles_per_seq, i // tiles_per_seq)
    rows8 = TM // 8
    return pl.pallas_call(
        functools.partial(_merge_kernel, tiles_per_seq=tiles_per_seq),
        grid=(nt,),
        in_specs=[
            pl.BlockSpec((TM, D_MODEL), tile),
            pl.BlockSpec((TM, D_SSM), tb),
            pl.BlockSpec((TM, D_SSM), tb),
            pl.BlockSpec((TM, D_SSM), tb),
            pl.BlockSpec((TM, 2 * D_CONV), tile),
            pl.BlockSpec((8, 2 * D_CONV), lambda i: (jnp.maximum(i * rows8 - 1, 0), 0)),
            pl.BlockSpec((8, 2 * D_CONV), lambda i: (jnp.minimum((i + 1) * rows8, tok // 8 - 1), 0)),
            pl.BlockSpec((TM, D_Q), tile),
            pl.BlockSpec((TM, N_BRANCH * D_MODEL), tile),
            pl.BlockSpec((1, 1, 6 * D_MODEL), lambda i: (i, 0, 0)),
            pl.BlockSpec((1, D_SSM), const),
            pl.BlockSpec((3, D_CONV), const),
            pl.BlockSpec((D_SSM, D_SSM), const),
            pl.BlockSpec((D_SSM, D_MODEL), const),
            pl.BlockSpec((D_CONV, D_MODEL), const),
            pl.BlockSpec((D_Q, D_MODEL), const),
            pl.BlockSpec((D_MODEL, D_MODEL), const),
            pl.BlockSpec((1, D_MODEL), const),
            pl.BlockSpec((D_MODEL, LANES), const),
            pl.BlockSpec((D_MODEL, LANES), const),
            pl.BlockSpec((1, LANES), const),
            pl.BlockSpec((TM, TM), const),
            pl.BlockSpec((LANES, LANES), const),
        ],
        out_specs=[
            pl.BlockSpec((TM, D_MODEL), tile),
            pl.BlockSpec((TM, D_MODEL), tile),
            pl.BlockSpec((TM, LANES), tile),
            pl.BlockSpec((8, LANES), tile),
        ],
        out_shape=[
            jax.ShapeDtypeStruct((tok, D_MODEL), F32),
            jax.ShapeDtypeStruct((tok, D_MODEL), BF16),
            jax.ShapeDtypeStruct((tok, LANES), F32),
            jax.ShapeDtypeStruct((nt * 8, LANES), F32),
        ],
        compiler_params=_cparams(("parallel",)),
        name="branch_merge",
    )(x, yf, yb, u_tb, cz, cz, cz, ya, gt, mod_tiles, d_skip, conv_w, wglu, wso, wco, wao, wo, n2g,
      wr_hi, wr_lo, b_r, tri, triu)


def _route_tile(l, tri_ref, triu_ref):
    lane = lax.broadcasted_iota(I32, l.shape, 1)
    vals, idxs, hots = [], [], []
    for _ in range(TOP_K):
        mx = jnp.max(l, axis=-1, keepdims=True)
        idx = jnp.min(jnp.where(l == mx, lane, LANES), axis=-1, keepdims=True)
        hot = lane == idx
        l = jnp.where(hot, -3e38, l)
        vals.append(mx)
        idxs.append(idx)
        hots.append(hot)
    ex = [jnp.exp(v - vals[0]) for v in vals]
    den = ex[0] + ex[1] + ex[2] + ex[3]
    picked = jnp.zeros(l.shape, F32)
    for hot in hots:
        picked = picked + hot.astype(F32)
    cum = _dot(tri_ref[...], picked.astype(BF16))
    cnt = jnp.sum(picked, axis=0, keepdims=True)
    seg = jnp.floor((cnt + (SEG - 1.0)) * (1.0 / SEG)) * SEG
    run_start = _dot(jnp.broadcast_to(seg, (8, LANES)).astype(BF16), triu_ref[...])[0:1]
    pos = cum + run_start
    out = jnp.zeros(l.shape, F32)
    for k in range(TOP_K):
        row = jnp.sum(jnp.where(hots[k], pos, 0.0), axis=-1, keepdims=True)
        out = jnp.where(lane == k, idxs[k].astype(F32), out)
        out = jnp.where(lane == TOP_K + k, row, out)
        out = jnp.where(lane == 2 * TOP_K + k, ex[k] / den, out)
    return out, jnp.broadcast_to(cnt, (8, LANES))


def _for_each_segment(tile, ng_ref, loc_ref, gs_ref, fn):
    def per_expert(e, c):
        m = tile * N_EXPERTS + e
        loc = loc_ref[m]
        gs = gs_ref[m]

        def per_piece(g, c2):
            fn(pl.multiple_of(loc + g * SEG, SEG), pl.multiple_of(gs + g * SEG, SEG))
            return c2

        lax.fori_loop(0, ng_ref[m], per_piece, 0)
        return c

    lax.fori_loop(0, N_EXPERTS, per_expert, 0)


def _local_rows(rt):
    rows = lax.broadcasted_iota(I32, (TM, LOCAL_ROWS), 1)
    return [rows == rt[:, TOP_K + k:TOP_K + k + 1].astype(I32) for k in range(TOP_K)]


def _dispatch_kernel(ng_ref, loc_ref, gs_ref, tot_ref, zstart_ref, nu_ref, h_ref, rt_ref, xs_hbm,
                     xl_ref, zero_ref, sem_ref, sem_z, *, n_blocks, n_tiles):
    i = pl.program_id(0)
    slot = i % 2

    @pl.when(i == 0)
    def _():
        zero_ref[...] = jnp.zeros_like(zero_ref)

        def zero_copy(start):
            rows = pl.ds(pl.multiple_of(start, MOE_BLOCK), MOE_BLOCK)
            return pltpu.make_async_copy(zero_ref, xs_hbm.at[rows, :], sem_z)

        def tail_start(b, c):
            zero_copy(b * MOE_BLOCK).start()
            return c

        def tail_wait(b, c):
            zero_copy(b * MOE_BLOCK).wait()
            return c

        for e in range(N_EXPERTS):
            @pl.when(zstart_ref[e] >= 0)
            def _():
                zero_copy(zstart_ref[e]).start()
        lax.fori_loop(nu_ref[0], n_blocks, tail_start, 0)
        for e in range(N_EXPERTS):
            @pl.when(zstart_ref[e] >= 0)
            def _():
                zero_copy(zstart_ref[e]).wait()
        lax.fori_loop(nu_ref[0], n_blocks, tail_wait, 0)

    place = jnp.zeros((TM, LOCAL_ROWS), F32)
    for hit in _local_rows(rt_ref[...]):
        place = place + hit.astype(F32)
    xl_ref[slot] = lax.dot_general(place.astype(BF16), h_ref[...], (((0,), (0,)), ((), ())),
                                   preferred_element_type=F32)

    def seg_copy(local_row, global_row, s):
        return pltpu.make_async_copy(xl_ref.at[s, pl.ds(local_row, SEG), :],
                                     xs_hbm.at[pl.ds(global_row, SEG), :], sem_ref.at[s])

    def wait_tile(tile, s):
        def one(g, c):
            seg_copy(0, 0, s).wait()
            return c

        lax.fori_loop(0, tot_ref[tile], one, 0)

    _for_each_segment(i, ng_ref, loc_ref, gs_ref, lambda lr, gr: seg_copy(lr, gr, slot).start())

    @pl.when(i > 0)
    def _():
        wait_tile(i - 1, 1 - slot)

    @pl.when(i == n_tiles - 1)
    def _():
        wait_tile(i, slot)


def _dispatch(h2, route, seg_meta, zstart, n_used, n_slots):
    tok = h2.shape[0]
    nt = tok // TM
    tile = lambda i, *_: (i, 0)
    return pl.pallas_call(
        functools.partial(_dispatch_kernel, n_blocks=n_slots // MOE_BLOCK, n_tiles=nt),
        grid_spec=pltpu.PrefetchScalarGridSpec(
            num_scalar_prefetch=6,
            grid=(nt,),
            in_specs=[
                pl.BlockSpec((TM, D_MODEL), tile),
                pl.BlockSpec((TM, LANES), tile),
            ],
            out_specs=pl.BlockSpec(memory_space=pl.ANY),
            scratch_shapes=[
                pltpu.VMEM((2, LOCAL_ROWS, D_MODEL), F32),
                pltpu.VMEM((MOE_BLOCK, D_MODEL), F32),
                pltpu.SemaphoreType.DMA((2,)),
                pltpu.SemaphoreType.DMA,
            ],
        ),
        out_shape=jax.ShapeDtypeStruct((n_slots, D_MODEL), F32),
        compiler_params=_cparams(("arbitrary",)),
        name="dispatch",
    )(*seg_meta, zstart, n_used, h2, route)


DEINT = 256


def _expert_kernel(be_ref, nu_ref, x_ref, wgu_ref, bg_ref, bu_ref, wd_ref, bd_ref, sg_ref, su_ref, y_ref,
                   wg_s, wu_s, wd_s):
    i = pl.program_id(0)
    prev = be_ref[jnp.maximum(i - 1, 0)]

    @pl.when((i == 0) | (be_ref[i] != prev))
    def _():
        for c in range(2 * D_EXPERT // DEINT):
            w = wgu_ref[0, :, c * DEINT:(c + 1) * DEINT].astype(BF16)
            cols = pl.ds(c * (DEINT // 2), DEINT // 2)
            wg_s[:, cols] = _dot(w, sg_ref[...]).astype(BF16)
            wu_s[:, cols] = _dot(w, su_ref[...]).astype(BF16)
        wd_s[...] = wd_ref[0].astype(BF16)

    @pl.when(i < nu_ref[0])
    def _():
        x = x_ref[...].astype(BF16)
        g = _dot(x, wg_s[...]) + bg_ref[0]
        u = _dot(x, wu_s[...]) + bu_ref[0]
        glu = jnp.minimum(g, SWIGLU_LIMIT)
        up = jnp.clip(u, -SWIGLU_LIMIT, SWIGLU_LIMIT)
        act = glu * _sigmoid(SWIGLU_ALPHA * glu) * (up + 1.0)
        y_ref[...] = _dot(act.astype(BF16), wd_s[...]) + bd_ref[0]

    @pl.when(i >= nu_ref[0])
    def _():
        y_ref[...] = jnp.zeros_like(y_ref)


def _experts(xs, block_expert, n_used, wgu, bg, bu, wd, bd, sel_g, sel_u):
    n_slots = xs.shape[0]
    n_blocks = n_slots // MOE_BLOCK
    wmap = lambda i, be, nu: (be[i], 0, 0)
    const = lambda i, be, nu: (0, 0)
    return pl.pallas_call(
        _expert_kernel,
        grid_spec=pltpu.PrefetchScalarGridSpec(
            num_scalar_prefetch=2,
            grid=(n_blocks,),
            in_specs=[
                pl.BlockSpec((MOE_BLOCK, D_MODEL), lambda i, be, nu: (jnp.minimum(i, nu[0] - 1), 0)),
                pl.BlockSpec((1, D_MODEL, 2 * D_EXPERT), wmap),
                pl.BlockSpec((1, 1, D_EXPERT), wmap),
                pl.BlockSpec((1, 1, D_EXPERT), wmap),
                pl.BlockSpec((1, D_EXPERT, D_MODEL), wmap),
                pl.BlockSpec((1, 1, D_MODEL), wmap),
                pl.BlockSpec((DEINT, DEINT // 2), const),
                pl.BlockSpec((DEINT, DEINT // 2), const),
            ],
            out_specs=pl.BlockSpec((MOE_BLOCK, D_MODEL), lambda i, be, nu: (i, 0)),
            scratch_shapes=[
                pltpu.VMEM((D_MODEL, D_EXPERT), BF16),
                pltpu.VMEM((D_MODEL, D_EXPERT), BF16),
                pltpu.VMEM((D_EXPERT, D_MODEL), BF16),
            ],
        ),
        out_shape=jax.ShapeDtypeStruct((n_slots, D_MODEL), F32),
        compiler_params=_cparams(("arbitrary",)),
        name="experts",
    )(block_expert, n_used, xs, wgu, bg, bu, wd, bd, sel_g, sel_u)


def _combine_kernel(ng_ref, loc_ref, gs_ref, tot_ref, ys_hbm, x_ref, rt_ref, mod_ref, o_ref, yl_ref, sem_ref,
                    *, n_tiles):
    i = pl.program_id(0)
    slot = i % 2

    def seg_copy(local_row, global_row, s):
        return pltpu.make_async_copy(ys_hbm.at[pl.ds(global_row, SEG), :],
                                     yl_ref.at[s, pl.ds(local_row, SEG), :], sem_ref.at[s])

    def fetch_tile(tile, s):
        _for_each_segment(tile, ng_ref, loc_ref, gs_ref, lambda lr, gr: seg_copy(lr, gr, s).start())

    @pl.when(i == 0)
    def _():
        yl_ref[...] = jnp.zeros_like(yl_ref)
        fetch_tile(0, 0)

    @pl.when(i + 1 < n_tiles)
    def _():
        fetch_tile(i + 1, 1 - slot)

    def one(g, c):
        seg_copy(0, 0, slot).wait()
        return c

    lax.fori_loop(0, tot_ref[i], one, 0)

    rt = rt_ref[...]
    weights = jnp.zeros((TM, LOCAL_ROWS), F32)
    for k, hit in enumerate(_local_rows(rt)):
        weights = weights + jnp.where(hit, rt[:, 2 * TOP_K + k:2 * TOP_K + k + 1], 0.0)
    w_hi = weights.astype(BF16)
    w_lo = (weights - w_hi.astype(F32)).astype(BF16)
    yl = yl_ref[slot].astype(BF16)
    y = _dot(w_hi, yl) + _dot(w_lo, yl)
    o_ref[...] = (x_ref[...] + mod_ref[0][:, 5 * D_MODEL:6 * D_MODEL] * y).reshape(o_ref.shape)


def _combine(x, ys, route, seg_meta, mod_tiles, out_seq=None):
    tok = x.shape[0]
    nt = tok // TM
    tile = lambda i, *_: (i, 0)
    if out_seq is None:
        out_spec = pl.BlockSpec((TM, D_MODEL), tile)
        out_shape = jax.ShapeDtypeStruct((tok, D_MODEL), F32)
    else:
        n_batch, tps = out_seq
        out_spec = pl.BlockSpec((1, TM, D_MODEL), lambda i, *_: (i // tps, jnp.maximum(i % tps - 1, 0), 0))
        out_shape = jax.ShapeDtypeStruct((n_batch, (tps - 1) * TM, D_MODEL), F32)
    return pl.pallas_call(
        functools.partial(_combine_kernel, n_tiles=nt),
        grid_spec=pltpu.PrefetchScalarGridSpec(
            num_scalar_prefetch=4,
            grid=(nt,),
            in_specs=[
                pl.BlockSpec(memory_space=pl.ANY),
                pl.BlockSpec((TM, D_MODEL), tile),
                pl.BlockSpec((TM, LANES), tile),
                pl.BlockSpec((1, 1, 6 * D_MODEL), lambda i, *_: (i, 0, 0)),
            ],
            out_specs=out_spec,
            scratch_shapes=[
                pltpu.VMEM((2, LOCAL_ROWS, D_MODEL), F32),
                pltpu.SemaphoreType.DMA((2,)),
            ],
        ),
        out_shape=out_shape,
        compiler_params=_cparams(("arbitrary",)),
        name="combine",
    )(*seg_meta, ys, x, route, mod_tiles)


def _rope_tables(t_len):
    n_lat = t_len - N_CTX
    t = np.arange(n_lat)
    n_pairs = HEAD_DIM // 4
    inv_freq = jnp.asarray(ROPE_BASE, F32) ** (-jnp.arange(n_pairs, dtype=F32) / n_pairs)
    row = jnp.asarray(t // GRID_W, F32)
    col = jnp.asarray(t % GRID_W, F32)
    ang = jnp.concatenate([row[:, None] * inv_freq, col[:, None] * inv_freq], axis=-1)
    ang = jnp.concatenate([jnp.zeros((N_CTX, HEAD_DIM // 2), F32), ang], axis=0)
    cos = jnp.tile(jnp.cos(ang), (1, LANES // (HEAD_DIM // 2)))
    sin = jnp.tile(jnp.sin(ang), (1, LANES // (HEAD_DIM // 2)))
    return cos, sin


def _s5_operands(lam_re, lam_im, log_dt, b_re, b_im, c_re, c_im):
    lam = lax.complex(lam_re, lam_im)
    dt = jnp.exp(log_dt)[..., None]
    lam_bar = jnp.exp(lam * dt)
    b_bar = ((lam_bar - 1.0) / lam)[..., None] * lax.complex(b_re, b_im)
    eye = jnp.eye(N_SSM_GROUPS, dtype=F32)

    def in_block(w):
        return jnp.einsum("zgph,gk->zghkp", w, eye).reshape(2, D_SSM, N_STATE)

    def out_block(w):
        return jnp.einsum("zghp,gk->zgpkh", w, eye).reshape(2, N_STATE, D_SSM)

    bd = jnp.concatenate([in_block(b_bar.real), in_block(b_bar.imag)], axis=-1).astype(BF16)
    cd = jnp.concatenate([out_block(c_re), out_block(-c_im)], axis=1).astype(BF16)
    lam_rows = jnp.stack([lam_bar[0].real.reshape(-1), lam_bar[0].imag.reshape(-1),
                          lam_bar[1].real.reshape(-1), lam_bar[1].imag.reshape(-1)])
    return bd, cd, lam_rows


def _moe_constants():
    tri = jnp.asarray(np.tril(np.ones((TM, TM)), -1), BF16)
    triu = jnp.asarray(np.triu(np.ones((LANES, LANES)), 1), BF16)
    pick = np.arange(DEINT)[:, None] == 2 * np.arange(DEINT // 2)[None, :]
    sel_g = jnp.asarray(pick, BF16)
    sel_u = jnp.asarray(np.roll(pick, 1, axis=0), BF16)
    return tri, triu, sel_g, sel_u


def _moe(xs, h2, route, counts, mod_tiles, w_gate_up, b_gate_up, w_down, b_down, consts, out_seq=None):
    _, _, sel_g, sel_u = consts
    tok = xs.shape[0]
    nt = tok // TM
    n_blocks = (tok * TOP_K + nt * N_EXPERTS * (SEG - 1) + N_EXPERTS * (MOE_BLOCK - 1)) // MOE_BLOCK
    n_slots = n_blocks * MOE_BLOCK

    counts = counts.reshape(nt, 8, LANES)[:, 0, :N_EXPERTS].astype(I32)
    seg = (counts + SEG - 1) // SEG * SEG
    region = jnp.sum(seg, axis=0)
    padded = (region + MOE_BLOCK - 1) // MOE_BLOCK * MOE_BLOCK
    pad_end = jnp.cumsum(padded)
    pad_start = pad_end - padded
    run_global = pad_start[None, :] + jnp.cumsum(seg, axis=0) - seg
    run_local = jnp.cumsum(seg, axis=1) - seg
    pieces = seg // SEG
    seg_meta = (pieces.reshape(-1), run_local.reshape(-1), run_global.reshape(-1), jnp.sum(pieces, axis=1))
    n_used = (pad_end[-1] // MOE_BLOCK).astype(I32)
    blk = jnp.minimum(jnp.arange(n_blocks, dtype=I32), n_used - 1) * MOE_BLOCK
    block_expert = jnp.minimum(jnp.sum(blk[:, None] >= pad_end[None, :], axis=1), N_EXPERTS - 1).astype(I32)
    zstart = jnp.where(region > 0, pad_end - MOE_BLOCK, -1).astype(I32)

    x_sorted = _dispatch(h2, route, seg_meta, zstart, n_used.reshape(1), n_slots)
    y_sorted = _experts(x_sorted, block_expert, n_used.reshape(1), w_gate_up,
                        b_gate_up[:, None, 0::2], b_gate_up[:, None, 1::2],
                        w_down, b_down[:, None, :], sel_g, sel_u)
    return _combine(xs, y_sorted, route, seg_meta, mod_tiles, out_seq)


def kernel(x, c, ctx, c_ctx, w_mod, b_mod, norm1_g, norm2_g, w_in, ssm_lam_re, ssm_lam_im, ssm_log_dt, ssm_b_re, ssm_b_im, ssm_c_re, ssm_c_im, ssm_d, w_glu, w_ssm_out, conv_w, w_conv_out, q_norm_g, k_norm_g, attn_sinks, w_attn_out, w_o, w_router, b_router, w_gate_up, b_gate_up, w_down, b_down):
    n_batch, n_lat, _ = x.shape
    t_len = N_CTX + n_lat
    tiles_per_seq = t_len // TM
    tok = n_batch * t_len
    nt = tok // TM

    xs = _assemble(ctx, x, tiles_per_seq)

    mod_rows = 8 * ((n_batch + 1 + 7) // 8)
    cvec = jnp.zeros((mod_rows, D_MODEL), F32).at[:n_batch].set(c).at[n_batch].set(c_ctx)
    mod_all = _modulation(cvec, w_mod, b_mod)
    tile_ids = np.arange(nt)
    tile_row = np.where(tile_ids % tiles_per_seq == 0, n_batch, tile_ids // tiles_per_seq)

    cos_t, sin_t = _rope_tables(t_len)
    head_sum = jnp.asarray(np.kron(np.eye(N_Q_HEADS), np.full((HEAD_DIM, HEAD_DIM), 1.0 / HEAD_DIM)), BF16)
    moe_consts = _moe_constants()

    for l in range(DEPTH):
        mod_tiles = mod_all[l][tile_row].reshape(nt, 1, 6 * D_MODEL)
        qg = jnp.tile(q_norm_g[l], N_Q_HEADS).reshape(1, D_Q)
        kg = jnp.tile(k_norm_g[l], N_KV_HEADS).reshape(1, D_KV)
        u_tb, cz, q, kv, gt = _in_proj(xs, mod_tiles, norm1_g[l].reshape(1, D_MODEL), w_in[l].astype(BF16),
                                       cos_t, sin_t, qg, kg, head_sum, n_batch, tiles_per_seq)

        bd, cd, lam_rows = _s5_operands(ssm_lam_re[l], ssm_lam_im[l], ssm_log_dt[l], ssm_b_re[l], ssm_b_im[l],
                                        ssm_c_re[l], ssm_c_im[l])
        yf, yb = _s5(u_tb.reshape(t_len * n_batch, D_SSM), bd, cd, lam_rows, n_batch, t_len)
        ya = _attention(q, kv, attn_sinks[l], n_batch, t_len)

        wr = jnp.zeros((D_MODEL, LANES), F32).at[:, :N_EXPERTS].set(w_router[l])
        wr_hi = wr.astype(BF16)
        wr_lo = (wr - wr_hi.astype(F32)).astype(BF16)
        b_r = jnp.full((1, LANES), NEG, F32).at[0, :N_EXPERTS].set(b_router[l])
        xs, h2, route, counts = _merge(
            xs, yf.reshape(t_len, n_batch * D_SSM), yb.reshape(t_len, n_batch * D_SSM), u_tb,
            cz, ya, gt, mod_tiles, ssm_d[l].reshape(1, D_SSM), conv_w[l],
            w_glu[l].astype(BF16), w_ssm_out[l].astype(BF16), w_conv_out[l].astype(BF16),
            w_attn_out[l].astype(BF16), w_o[l].astype(BF16), norm2_g[l].reshape(1, D_MODEL),
            wr_hi, wr_lo, b_r, moe_consts[0], moe_consts[1], tiles_per_seq)

        out_seq = (n_batch, tiles_per_seq) if l == DEPTH - 1 else None
        xs = _moe(xs, h2, route, counts, mod_tiles, w_gate_up[l], b_gate_up[l], w_down[l], b_down[l], moe_consts,
                  out_seq)

    return xs
```

```python
import functools
import math

import jax
import jax.numpy as jnp
import numpy as np
from jax import lax
from jax.experimental import pallas as pl
from jax.experimental.pallas import tpu as pltpu

F32 = jnp.float32
BF16 = jnp.bfloat16
I32 = jnp.int32

D_MODEL = 1024
DEPTH = 4
N_CTX = 256
HEAD_DIM = 64
N_Q_HEADS = 8
N_KV_HEADS = 2
D_Q = N_Q_HEADS * HEAD_DIM
D_KV = N_KV_HEADS * HEAD_DIM
WINDOW = 128
ATTN_BLOCK = 128
ROPE_BASE = 10000.0
GRID_W = 64
D_SSM = 256
SSM_GROUP = 16
N_SSM_GROUPS = 16
SSM_STATE = 64
N_STATE = N_SSM_GROUPS * SSM_STATE
D_CONV = 256
N_BRANCH = 3
D_IN = D_SSM + 3 * D_CONV + D_Q + 2 * D_KV + N_BRANCH * D_MODEL
N_EXPERTS = 32
TOP_K = 4
D_EXPERT = 1024
SWIGLU_LIMIT = 7.0
SWIGLU_ALPHA = 1.702
EPS = 1e-6

LANES = 128
TM = 256
MOE_BLOCK = 512
SEG = 8
LOCAL_ROWS = -(-(TM * TOP_K + N_EXPERTS * (SEG - 1)) // LANES) * LANES
S5_CHUNK = 64
S5_COLS = 512
NEG = -1e30
VMEM_LIMIT = 56 * 1024 * 1024


def _cparams(sem):
    return pltpu.CompilerParams(dimension_semantics=sem, vmem_limit_bytes=VMEM_LIMIT)


def _dot(a, b):
    return jnp.dot(a, b, preferred_element_type=F32)


def _sigmoid(x):
    return 1.0 / (1.0 + jnp.exp(-x))


def _rms(x, g):
    ms = jnp.mean(x * x, axis=-1, keepdims=True)
    return x * lax.rsqrt(ms + EPS) * g


def _mod_kernel(c_ref, w_ref, b_ref, o_ref):
    c = c_ref[...]
    s = (c * _sigmoid(c)).astype(BF16)
    o_ref[0] = _dot(s, w_ref[0].astype(BF16)) + b_ref[0]


def _modulation(cvec, w_mod, b_mod):
    rows = cvec.shape[0]
    nblk = 1536
    return pl.pallas_call(
        _mod_kernel,
        grid=(DEPTH, 6 * D_MODEL // nblk),
        in_specs=[
            pl.BlockSpec((rows, D_MODEL), lambda l, j: (0, 0)),
            pl.BlockSpec((1, D_MODEL, nblk), lambda l, j: (l, 0, j)),
            pl.BlockSpec((1, 1, nblk), lambda l, j: (l, 0, j)),
        ],
        out_specs=pl.BlockSpec((1, rows, nblk), lambda l, j: (l, 0, j)),
        out_shape=jax.ShapeDtypeStruct((DEPTH, rows, 6 * D_MODEL), F32),
        compiler_params=_cparams(("arbitrary", "arbitrary")),
        name="modulation",
    )(cvec, w_mod, b_mod.reshape(DEPTH, 1, 6 * D_MODEL))


def _assemble_kernel(c_ref, x_ref, o_ref, *, tiles_per_seq):
    r = pl.program_id(0) % tiles_per_seq

    @pl.when(r == 0)
    def _():
        o_ref[...] = c_ref[0]

    @pl.when(r > 0)
    def _():
        o_ref[...] = x_ref[0]


def _assemble(ctx, x, tiles_per_seq):
    n_batch = x.shape[0]
    nt = n_batch * tiles_per_seq
    return pl.pallas_call(
        functools.partial(_assemble_kernel, tiles_per_seq=tiles_per_seq),
        grid=(nt,),
        in_specs=[
            pl.BlockSpec((1, TM, D_MODEL), lambda i: (i // tiles_per_seq, 0, 0)),
            pl.BlockSpec((1, TM, D_MODEL), lambda i: (i // tiles_per_seq, jnp.maximum(i % tiles_per_seq - 1, 0), 0)),
        ],
        out_specs=pl.BlockSpec((TM, D_MODEL), lambda i: (i, 0)),
        out_shape=jax.ShapeDtypeStruct((nt * TM, D_MODEL), F32),
        compiler_params=_cparams(("arbitrary",)),
        name="assemble",
    )(ctx, x)


def _rot_half(x, width):
    lane = lax.broadcasted_iota(I32, x.shape, 1)
    first = (lane % HEAD_DIM) < (HEAD_DIM // 2)
    return jnp.where(first, -pltpu.roll(x, width - HEAD_DIM // 2, axis=1), pltpu.roll(x, HEAD_DIM // 2, axis=1))


def _inproj_kernel(x_ref, mod_ref, g_ref, w_ref, cos_ref, sin_ref, qg_ref, kg_ref, hs_ref,
                   u_ref, cz_ref, q_ref, kv_ref, gt_ref):
    m = mod_ref[0]
    h = _rms(x_ref[...], g_ref[...])
    h = (h * (1.0 + m[:, D_MODEL:2 * D_MODEL]) + m[:, 0:D_MODEL]).astype(BF16)
    o_gate = D_SSM + 3 * D_CONV + D_Q + 2 * D_KV
    y = _dot(h, w_ref[:, 0:o_gate])
    u_ref[...] = y[:, 0:D_SSM].astype(BF16)
    cb = y[:, D_SSM:D_SSM + D_CONV]
    cc = y[:, D_SSM + D_CONV:D_SSM + 2 * D_CONV]
    cx = y[:, D_SSM + 2 * D_CONV:D_SSM + 3 * D_CONV]
    cz_ref[...] = jnp.concatenate([cb, cc * cx], axis=-1).astype(BF16)
    o_q = D_SSM + 3 * D_CONV
    q = y[:, o_q:o_q + D_Q]
    k = y[:, o_q + D_Q:o_q + D_Q + D_KV]
    v = y[:, o_q + D_Q + D_KV:o_gate]
    cos = cos_ref[...]
    sin = sin_ref[...]
    q_ms = _dot((q * q).astype(BF16), hs_ref[...])
    qn = q * lax.rsqrt(q_ms + EPS) * qg_ref[...]
    cos_q = jnp.concatenate([cos] * (D_Q // LANES), axis=-1)
    sin_q = jnp.concatenate([sin] * (D_Q // LANES), axis=-1)
    qr = qn * cos_q + _rot_half(qn, D_Q) * sin_q
    q_ref[...] = (qr * (HEAD_DIM ** -0.5)).astype(BF16)
    k_ms = _dot((k * k).astype(BF16), hs_ref[0:D_KV, 0:D_KV])
    kn = k * lax.rsqrt(k_ms + EPS) * kg_ref[...]
    kr = kn * cos + _rot_half(kn, D_KV) * sin
    kv_ref[...] = jnp.concatenate([kr, v], axis=-1).astype(BF16)
    gt = _dot(h, w_ref[:, o_gate:D_IN])
    gt_ref[...] = _sigmoid(gt).astype(BF16)


def _in_proj(x, mod_tiles, norm_g, w_in_bf, cos_t, sin_t, qg, kg, head_sum, n_batch, tiles_per_seq):
    tok = x.shape[0]
    nt = tok // TM
    t_len = tiles_per_seq * TM
    tile = lambda i: (i, 0)
    const = lambda i: (0, 0)
    seq_tile = lambda i: (i % tiles_per_seq, 0)
    return pl.pallas_call(
        _inproj_kernel,
        grid=(nt,),
        in_specs=[
            pl.BlockSpec((TM, D_MODEL), tile),
            pl.BlockSpec((1, 1, 6 * D_MODEL), lambda i: (i, 0, 0)),
            pl.BlockSpec((1, D_MODEL), const),
            pl.BlockSpec((D_MODEL, D_IN), const),
            pl.BlockSpec((TM, LANES), seq_tile),
            pl.BlockSpec((TM, LANES), seq_tile),
            pl.BlockSpec((1, D_Q), const),
            pl.BlockSpec((1, D_KV), const),
            pl.BlockSpec((D_Q, D_Q), const),
        ],
        out_specs=[
            pl.BlockSpec((TM, D_SSM), lambda i: (i % tiles_per_seq, i // tiles_per_seq)),
            pl.BlockSpec((TM, 2 * D_CONV), tile),
            pl.BlockSpec((TM, D_Q), tile),
            pl.BlockSpec((TM, 2 * D_KV), tile),
            pl.BlockSpec((TM, N_BRANCH * D_MODEL), tile),
        ],
        out_shape=[
            jax.ShapeDtypeStruct((t_len, n_batch * D_SSM), BF16),
            jax.ShapeDtypeStruct((tok, 2 * D_CONV), BF16),
            jax.ShapeDtypeStruct((tok, D_Q), BF16),
            jax.ShapeDtypeStruct((tok, 2 * D_KV), BF16),
            jax.ShapeDtypeStruct((tok, N_BRANCH * D_MODEL), BF16),
        ],
        compiler_params=_cparams(("parallel",)),
        name="in_proj",
    )(x, mod_tiles, norm_g, w_in_bf, cos_t, sin_t, qg, kg, head_sum)


def _s5_kernel(uf_ref, ub_ref, bd_ref, cd_ref, lam_ref, yf_ref, yb_ref, sf_ref, sb_ref, rows_ref, carry_ref,
               *, n_batch):
    i = pl.program_id(0)

    @pl.when(i == 0)
    def _():
        carry_ref[...] = jnp.zeros_like(carry_ref)

    def batch_rows(b):
        return pl.ds(b, S5_CHUNK, stride=n_batch)

    for z, (u_ref, s_ref, y_ref) in enumerate(((uf_ref, sf_ref, yf_ref), (ub_ref, sb_ref, yb_ref))):
        for b in range(n_batch):
            for h in range(D_SSM // LANES):
                lanes = pl.ds(b * D_SSM + h * LANES, LANES)
                rows_ref[h, batch_rows(b), :] = u_ref[:, lanes].astype(F32)
        u_rows = jnp.concatenate([rows_ref[h] for h in range(D_SSM // LANES)], axis=-1)
        s_ref[...] = _dot(u_rows.astype(BF16), bd_ref[z])
        for j in range(N_STATE // S5_COLS):
            re_cols = pl.ds(j * S5_COLS, S5_COLS)
            im_cols = pl.ds(N_STATE + j * S5_COLS, S5_COLS)
            lr = jnp.broadcast_to(lam_ref[2 * z:2 * z + 1, re_cols], (n_batch, S5_COLS))
            li = jnp.broadcast_to(lam_ref[2 * z + 1:2 * z + 2, re_cols], (n_batch, S5_COLS))

            def step(s, c, s_ref=s_ref, z=z, re_cols=re_cols, im_cols=im_cols, lr=lr, li=li):
                t = s if z == 0 else S5_CHUNK - 1 - s
                rows = pl.ds(pl.multiple_of(t * n_batch, n_batch), n_batch)
                xr, xi = c
                nr = lr * xr - li * xi + s_ref[rows, re_cols]
                ni = lr * xi + li * xr + s_ref[rows, im_cols]
                s_ref[rows, re_cols] = nr
                s_ref[rows, im_cols] = ni
                return nr, ni

            c0 = (carry_ref[2 * z, :, re_cols], carry_ref[2 * z + 1, :, re_cols])
            fr, fi = lax.fori_loop(0, S5_CHUNK, step, c0, unroll=True)
            carry_ref[2 * z, :, re_cols] = fr
            carry_ref[2 * z + 1, :, re_cols] = fi
        y_rows = _dot(s_ref[...].astype(BF16), cd_ref[z])
        for h in range(D_SSM // LANES):
            rows_ref[h] = y_rows[:, h * LANES:(h + 1) * LANES]
        for b in range(n_batch):
            for h in range(D_SSM // LANES):
                y_ref[:, pl.ds(b * D_SSM + h * LANES, LANES)] = rows_ref[h, batch_rows(b), :]


def _s5(u_tb, bd, cd, lam, n_batch, t_len):
    rows = S5_CHUNK * n_batch
    width = n_batch * D_SSM
    n_chunks = t_len // S5_CHUNK
    ctx_chunks = N_CTX // S5_CHUNK

    def bwd_block(i):
        return (jnp.where(i < ctx_chunks, ctx_chunks - 1 - i, n_chunks - 1 + ctx_chunks - i), 0)

    return pl.pallas_call(
        functools.partial(_s5_kernel, n_batch=n_batch),
        grid=(n_chunks,),
        in_specs=[
            pl.BlockSpec((S5_CHUNK, width), lambda i: (i, 0)),
            pl.BlockSpec((S5_CHUNK, width), bwd_block),
            pl.BlockSpec((2, D_SSM, 2 * N_STATE), lambda i: (0, 0, 0)),
            pl.BlockSpec((2, 2 * N_STATE, D_SSM), lambda i: (0, 0, 0)),
            pl.BlockSpec((4, N_STATE), lambda i: (0, 0)),
        ],
        out_specs=[
            pl.BlockSpec((S5_CHUNK, width), lambda i: (i, 0)),
            pl.BlockSpec((S5_CHUNK, width), bwd_block),
        ],
        out_shape=[jax.ShapeDtypeStruct((t_len, width), F32)] * 2,
        scratch_shapes=[
            pltpu.VMEM((rows, 2 * N_STATE), F32),
            pltpu.VMEM((rows, 2 * N_STATE), F32),
            pltpu.VMEM((D_SSM // LANES, rows, LANES), F32),
            pltpu.VMEM((4, n_batch, N_STATE), F32),
        ],
        compiler_params=_cparams(("arbitrary",)),
        name="s5_scan",
    )(u_tb, u_tb, bd, cd, lam)


def _attn_kernel(sink_ref, q_ref, kvc_ref, kv0_ref, kv1_ref, kv2_ref, o_ref, *, n_lat):
    n = pl.program_id(1) - N_CTX // ATTN_BLOCK
    rep = N_Q_HEADS // N_KV_HEADS
    rows = rep * ATTN_BLOCK
    q = q_ref[0]
    kvc = kvc_ref[0]
    band = jnp.concatenate([kv0_ref[0], kv1_ref[0], kv2_ref[0]], axis=0)
    iq = lax.broadcasted_iota(I32, (rows, 3 * ATTN_BLOCK), 0) % ATTN_BLOCK
    ik = lax.broadcasted_iota(I32, (rows, 3 * ATTN_BLOCK), 1)
    qpos = n * ATTN_BLOCK + iq
    kpos = (n - 1) * ATTN_BLOCK + ik
    valid = (jnp.abs(qpos - kpos) <= WINDOW) & (kpos >= 0) & (kpos < n_lat) & (n >= 0)
    head_of_row = lax.broadcasted_iota(I32, (rows, 1), 0) // ATTN_BLOCK
    contract_last = (((1,), (1,)), ((), ()))
    outs = []
    for g in range(N_KV_HEADS):
        qg = jnp.concatenate([q[:, (g * rep + r) * HEAD_DIM:(g * rep + r + 1) * HEAD_DIM] for r in range(rep)], axis=0)
        sink = jnp.zeros((rows, 1), F32)
        for r in range(rep):
            sink = jnp.where(head_of_row == r, sink_ref[g * rep + r], sink)
        kc = kvc[:, g * HEAD_DIM:(g + 1) * HEAD_DIM]
        vc = kvc[:, D_KV + g * HEAD_DIM:D_KV + (g + 1) * HEAD_DIM]
        kb = band[:, g * HEAD_DIM:(g + 1) * HEAD_DIM]
        vb = band[:, D_KV + g * HEAD_DIM:D_KV + (g + 1) * HEAD_DIM]
        sc = lax.dot_general(qg, kc, contract_last, preferred_element_type=F32)
        sb = lax.dot_general(qg, kb, contract_last, preferred_element_type=F32)
        sb = jnp.where(valid, sb, NEG)
        mx = jnp.maximum(jnp.maximum(jnp.max(sc, axis=-1, keepdims=True), jnp.max(sb, axis=-1, keepdims=True)), sink)
        pc = jnp.exp(sc - mx)
        pb = jnp.exp(sb - mx)
        den = jnp.sum(pc, axis=-1, keepdims=True) + jnp.sum(pb, axis=-1, keepdims=True) + jnp.exp(sink - mx)
        o = (_dot(pc.astype(BF16), vc) + _dot(pb.astype(BF16), vb)) / den
        outs.extend(o[r * ATTN_BLOCK:(r + 1) * ATTN_BLOCK] for r in range(rep))
    o_ref[0] = jnp.concatenate(outs, axis=-1).astype(BF16)


def _attention(q, kv, sinks, n_batch, t_len):
    nqb = t_len // ATTN_BLOCK
    first = N_CTX // ATTN_BLOCK
    q3 = q.reshape(n_batch, t_len, D_Q)
    kv3 = kv.reshape(n_batch, t_len, 2 * D_KV)

    def band(off):
        return lambda b, j, s: (b, jnp.clip(j + off, first, nqb - 1), 0)

    out = pl.pallas_call(
        functools.partial(_attn_kernel, n_lat=t_len - N_CTX),
        grid_spec=pltpu.PrefetchScalarGridSpec(
            num_scalar_prefetch=1,
            grid=(n_batch, nqb),
            in_specs=[
                pl.BlockSpec((1, ATTN_BLOCK, D_Q), lambda b, j, s: (b, j, 0)),
                pl.BlockSpec((1, N_CTX, 2 * D_KV), lambda b, j, s: (b, 0, 0)),
                pl.BlockSpec((1, ATTN_BLOCK, 2 * D_KV), band(-1)),
                pl.BlockSpec((1, ATTN_BLOCK, 2 * D_KV), band(0)),
                pl.BlockSpec((1, ATTN_BLOCK, 2 * D_KV), band(1)),
            ],
            out_specs=pl.BlockSpec((1, ATTN_BLOCK, D_Q), lambda b, j, s: (b, j, 0)),
        ),
        out_shape=jax.ShapeDtypeStruct((n_batch, t_len, D_Q), BF16),
        compiler_params=_cparams(("parallel", "parallel")),
        name="attention",
    )(sinks, q3, kv3, kv3, kv3, kv3)
    return out.reshape(n_batch * t_len, D_Q)


def _gelu_tanh(x):
    return 0.5 * x * (1.0 + jnp.tanh(math.sqrt(2.0 / math.pi) * (x + 0.044715 * (x * x * x))))


def _merge_kernel(x_ref, yf_ref, yb_ref, u_ref, cz_ref, czp_ref, czn_ref, ya_ref, gt_ref, mod_ref,
                  d_ref, cw_ref, wglu_ref, wso_ref, wco_ref, wao_ref, wo_ref, n2g_ref,
                  wrh_ref, wrl_ref, br_ref, tri_ref, triu_ref, xo_ref, h2_ref, rt_ref, cnt_ref, *, tiles_per_seq):
    r = pl.program_id(0) % tiles_per_seq
    ys = yf_ref[...] + yb_ref[...] + d_ref[...] * u_ref[...].astype(F32)
    z = _gelu_tanh(ys)
    glu = z * _sigmoid(_dot(z.astype(BF16), wglu_ref[...]))
    br_ssm = _dot(glu.astype(BF16), wso_ref[...])

    cz = cz_ref[...].astype(F32)
    cb = cz[:, 0:D_CONV]
    zz = cz[:, D_CONV:2 * D_CONV]
    seg_first = r <= 1
    seg_last = (r == 0) | (r == tiles_per_seq - 1)
    prev_row = jnp.where(seg_first, 0.0, czp_ref[7:8, D_CONV:2 * D_CONV].astype(F32))
    next_row = jnp.where(seg_last, 0.0, czn_ref[0:1, D_CONV:2 * D_CONV].astype(F32))
    row = lax.broadcasted_iota(I32, (TM, D_CONV), 0)
    z_dn = jnp.where(row == 0, prev_row, pltpu.roll(zz, 1, axis=0))
    z_up = jnp.where(row == TM - 1, next_row, pltpu.roll(zz, TM - 1, axis=0))
    y_conv = cb * (cw_ref[0:1, :] * z_dn + cw_ref[1:2, :] * zz + cw_ref[2:3, :] * z_up)
    br_conv = _dot(y_conv.astype(BF16), wco_ref[...])
    br_attn = _dot(ya_ref[...], wao_ref[...])

    merged = (gt_ref[:, 0:D_MODEL].astype(F32) * br_ssm
              + gt_ref[:, D_MODEL:2 * D_MODEL].astype(F32) * br_conv
              + gt_ref[:, 2 * D_MODEL:3 * D_MODEL].astype(F32) * br_attn)
    mix = _dot(merged.astype(BF16), wo_ref[...])
    m = mod_ref[0]
    xn = x_ref[...] + m[:, 2 * D_MODEL:3 * D_MODEL] * mix
    xo_ref[...] = xn
    h2 = _rms(xn, n2g_ref[...]) * (1.0 + m[:, 4 * D_MODEL:5 * D_MODEL]) + m[:, 3 * D_MODEL:4 * D_MODEL]
    hi = h2.astype(BF16)
    h2_ref[...] = hi
    lo = (h2 - hi.astype(F32)).astype(BF16)
    logits = _dot(hi, wrh_ref[...]) + _dot(lo, wrh_ref[...]) + _dot(hi, wrl_ref[...]) + br_ref[...]
    rt_ref[...], cnt_ref[...] = _route_tile(logits, tri_ref, triu_ref)


def _merge(x, yf, yb, u_tb, cz, ya, gt, mod_tiles, d_skip, conv_w, wglu, wso, wco, wao, wo, n2g,
           wr_hi, wr_lo, b_r, tri, triu, tiles_per_seq):
    tok = x.shape[0]
    nt = tok // TM
    tile = lambda i: (i, 0)
    const = lambda i: (0, 0)
    tb = lambda i: (i % tiles_per_seq, i // tiles_per_seq)
    rows8 = TM // 8
    return pl.pallas_call(
        functools.partial(_merge_kernel, tiles_per_seq=tiles_per_seq),
        grid=(nt,),
        in_specs=[
            pl.BlockSpec((TM, D_MODEL), tile),
            pl.BlockSpec((TM, D_SSM), tb),
            pl.BlockSpec((TM, D_SSM), tb),
            pl.BlockSpec((TM, D_SSM), tb),
            pl.BlockSpec((TM, 2 * D_CONV), tile),
            pl.BlockSpec((8, 2 * D_CONV), lambda i: (jnp.maximum(i * rows8 - 1, 0), 0)),
            pl.BlockSpec((8, 2 * D_CONV), lambda i: (jnp.minimum((i + 1) * rows8, tok // 8 - 1), 0)),
            pl.BlockSpec((TM, D_Q), tile),
            pl.BlockSpec((TM, N_BRANCH * D_MODEL), tile),
            pl.BlockSpec((1, 1, 6 * D_MODEL), lambda i: (i, 0, 0)),
            pl.BlockSpec((1, D_SSM), const),
            pl.BlockSpec((3, D_CONV), const),
            pl.BlockSpec((D_SSM, D_SSM), const),
            pl.BlockSpec((D_SSM, D_MODEL), const),
            pl.BlockSpec((D_CONV, D_MODEL), const),
            pl.BlockSpec((D_Q, D_MODEL), const),
            pl.BlockSpec((D_MODEL, D_MODEL), const),
            pl.BlockSpec((1, D_MODEL), const),
            pl.BlockSpec((D_MODEL, LANES), const),
            pl.BlockSpec((D_MODEL, LANES), const),
            pl.BlockSpec((1, LANES), const),
            pl.BlockSpec((TM, TM), const),
            pl.BlockSpec((LANES, LANES), const),
        ],
        out_specs=[
            pl.BlockSpec((TM, D_MODEL), tile),
            pl.BlockSpec((TM, D_MODEL), tile),
            pl.BlockSpec((TM, LANES), tile),
            pl.BlockSpec((8, LANES), tile),
        ],
        out_shape=[
            jax.ShapeDtypeStruct((tok, D_MODEL), F32),
            jax.ShapeDtypeStruct((tok, D_MODEL), BF16),
            jax.ShapeDtypeStruct((tok, LANES), F32),
            jax.ShapeDtypeStruct((nt * 8, LANES), F32),
        ],
        compiler_params=_cparams(("parallel",)),
        name="branch_merge",
    )(x, yf, yb, u_tb, cz, cz, cz, ya, gt, mod_tiles, d_skip, conv_w, wglu, wso, wco, wao, wo, n2g,
      wr_hi, wr_lo, b_r, tri, triu)


def _route_tile(l, tri_ref, triu_ref):
    lane = lax.broadcasted_iota(I32, l.shape, 1)
    vals, idxs, hots = [], [], []
    for _ in range(TOP_K):
        mx = jnp.max(l, axis=-1, keepdims=True)
        idx = jnp.min(jnp.where(l == mx, lane, LANES), axis=-1, keepdims=True)
        hot = lane == idx
        l = jnp.where(hot, -3e38, l)
        vals.append(mx)
        idxs.append(idx)
        hots.append(hot)
    ex = [jnp.exp(v - vals[0]) for v in vals]
    den = ex[0] + ex[1] + ex[2] + ex[3]
    picked = jnp.zeros(l.shape, F32)
    for hot in hots:
        picked = picked + hot.astype(F32)
    cum = _dot(tri_ref[...], picked.astype(BF16))
    cnt = jnp.sum(picked, axis=0, keepdims=True)
    seg = jnp.floor((cnt + (SEG - 1.0)) * (1.0 / SEG)) * SEG
    run_start = _dot(jnp.broadcast_to(seg, (8, LANES)).astype(BF16), triu_ref[...])[0:1]
    pos = cum + run_start
    out = jnp.zeros(l.shape, F32)
    for k in range(TOP_K):
        row = jnp.sum(jnp.where(hots[k], pos, 0.0), axis=-1, keepdims=True)
        out = jnp.where(lane == k, idxs[k].astype(F32), out)
        out = jnp.where(lane == TOP_K + k, row, out)
        out = jnp.where(lane == 2 * TOP_K + k, ex[k] / den, out)
    return out, jnp.broadcast_to(cnt, (8, LANES))


MAX_PIECES = LOCAL_ROWS // SEG


def _for_each_piece(tile, lrow_ref, grow_ref, tot_ref, fn):
    base = tile * MAX_PIECES

    def per_piece(p, c):
        fn(pl.multiple_of(lrow_ref[base + p], SEG), pl.multiple_of(grow_ref[base + p], SEG))
        return c

    lax.fori_loop(0, tot_ref[tile], per_piece, 0)


def _local_rows(rt):
    rows = lax.broadcasted_iota(I32, (TM, LOCAL_ROWS), 1)
    return [rows == rt[:, TOP_K + k:TOP_K + k + 1].astype(I32) for k in range(TOP_K)]


def _dispatch_kernel(lrow_ref, grow_ref, tot_ref, zstart_ref, nu_ref, h_ref, rt_ref, xs_hbm,
                     xl_ref, zero_ref, sem_ref, sem_z, *, n_blocks, n_tiles):
    i = pl.program_id(0)
    slot = i % 2

    @pl.when(i == 0)
    def _():
        zero_ref[...] = jnp.zeros_like(zero_ref)

        def zero_copy(start):
            rows = pl.ds(pl.multiple_of(start, MOE_BLOCK), MOE_BLOCK)
            return pltpu.make_async_copy(zero_ref, xs_hbm.at[rows, :], sem_z)

        def tail_start(b, c):
            zero_copy(b * MOE_BLOCK).start()
            return c

        def tail_wait(b, c):
            zero_copy(b * MOE_BLOCK).wait()
            return c

        for e in range(N_EXPERTS):
            @pl.when(zstart_ref[e] >= 0)
            def _():
                zero_copy(zstart_ref[e]).start()
        lax.fori_loop(nu_ref[0], n_blocks, tail_start, 0)
        for e in range(N_EXPERTS):
            @pl.when(zstart_ref[e] >= 0)
            def _():
                zero_copy(zstart_ref[e]).wait()
        lax.fori_loop(nu_ref[0], n_blocks, tail_wait, 0)

    place = jnp.zeros((TM, LOCAL_ROWS), F32)
    for hit in _local_rows(rt_ref[...]):
        place = place + hit.astype(F32)
    xl_ref[slot] = lax.dot_general(place.astype(BF16), h_ref[...], (((0,), (0,)), ((), ())),
                                   preferred_element_type=F32)

    def seg_copy(local_row, global_row, s):
        return pltpu.make_async_copy(xl_ref.at[s, pl.ds(local_row, SEG), :],
                                     xs_hbm.at[pl.ds(global_row, SEG), :], sem_ref.at[s])

    def wait_tile(tile, s):
        def one(g, c):
            seg_copy(0, 0, s).wait()
            return c

        lax.fori_loop(0, tot_ref[tile], one, 0)

    _for_each_piece(i, lrow_ref, grow_ref, tot_ref, lambda lr, gr: seg_copy(lr, gr, slot).start())

    @pl.when(i > 0)
    def _():
        wait_tile(i - 1, 1 - slot)

    @pl.when(i == n_tiles - 1)
    def _():
        wait_tile(i, slot)


def _dispatch(h2, route, seg_meta, zstart, n_used, n_slots):
    tok = h2.shape[0]
    nt = tok // TM
    tile = lambda i, *_: (i, 0)
    return pl.pallas_call(
        functools.partial(_dispatch_kernel, n_blocks=n_slots // MOE_BLOCK, n_tiles=nt),
        grid_spec=pltpu.PrefetchScalarGridSpec(
            num_scalar_prefetch=5,
            grid=(nt,),
            in_specs=[
                pl.BlockSpec((TM, D_MODEL), tile),
                pl.BlockSpec((TM, LANES), tile),
            ],
            out_specs=pl.BlockSpec(memory_space=pl.ANY),
            scratch_shapes=[
                pltpu.VMEM((2, LOCAL_ROWS, D_MODEL), F32),
                pltpu.VMEM((MOE_BLOCK, D_MODEL), F32),
                pltpu.SemaphoreType.DMA((2,)),
                pltpu.SemaphoreType.DMA,
            ],
        ),
        out_shape=jax.ShapeDtypeStruct((n_slots, D_MODEL), F32),
        compiler_params=_cparams(("arbitrary",)),
        name="dispatch",
    )(*seg_meta, zstart, n_used, h2, route)


DEINT = 256


def _expert_kernel(be_ref, nu_ref, x_ref, wgu_ref, bg_ref, bu_ref, wd_ref, bd_ref, sg_ref, su_ref, y_ref,
                   wg_s, wu_s, wd_s):
    i = pl.program_id(0)
    prev = be_ref[jnp.maximum(i - 1, 0)]

    @pl.when((i == 0) | (be_ref[i] != prev))
    def _():
        for c in range(2 * D_EXPERT // DEINT):
            w = wgu_ref[0, 0, :, c * DEINT:(c + 1) * DEINT].astype(BF16)
            cols = pl.ds(c * (DEINT // 2), DEINT // 2)
            wg_s[:, cols] = _dot(w, sg_ref[...]).astype(BF16)
            wu_s[:, cols] = _dot(w, su_ref[...]).astype(BF16)
        wd_s[...] = wd_ref[0, 0].astype(BF16)

    @pl.when(i < nu_ref[0])
    def _():
        x = x_ref[...].astype(BF16)
        g = _dot(x, wg_s[...]) + bg_ref[0]
        u = _dot(x, wu_s[...]) + bu_ref[0]
        glu = jnp.minimum(g, SWIGLU_LIMIT)
        up = jnp.clip(u, -SWIGLU_LIMIT, SWIGLU_LIMIT)
        act = glu * _sigmoid(SWIGLU_ALPHA * glu) * (up + 1.0)
        y_ref[...] = _dot(act.astype(BF16), wd_s[...]) + bd_ref[0]

    @pl.when(i >= nu_ref[0])
    def _():
        y_ref[...] = jnp.zeros_like(y_ref)


def _experts(xs, block_expert, n_used, layer, wgu_all, bg, bu, wd_all, bd, sel_g, sel_u):
    n_slots = xs.shape[0]
    n_blocks = n_slots // MOE_BLOCK
    wmap = lambda i, be, nu: (be[i], 0, 0)
    lwmap = lambda i, be, nu: (layer, be[i], 0, 0)
    const = lambda i, be, nu: (0, 0)
    return pl.pallas_call(
        _expert_kernel,
        grid_spec=pltpu.PrefetchScalarGridSpec(
            num_scalar_prefetch=2,
            grid=(n_blocks,),
            in_specs=[
                pl.BlockSpec((MOE_BLOCK, D_MODEL), lambda i, be, nu: (jnp.minimum(i, nu[0] - 1), 0)),
                pl.BlockSpec((1, 1, D_MODEL, 2 * D_EXPERT), lwmap),
                pl.BlockSpec((1, 1, D_EXPERT), wmap),
                pl.BlockSpec((1, 1, D_EXPERT), wmap),
                pl.BlockSpec((1, 1, D_EXPERT, D_MODEL), lwmap),
                pl.BlockSpec((1, 1, D_MODEL), wmap),
                pl.BlockSpec((DEINT, DEINT // 2), const),
                pl.BlockSpec((DEINT, DEINT // 2), const),
            ],
            out_specs=pl.BlockSpec((MOE_BLOCK, D_MODEL), lambda i, be, nu: (i, 0)),
            scratch_shapes=[
                pltpu.VMEM((D_MODEL, D_EXPERT), BF16),
                pltpu.VMEM((D_MODEL, D_EXPERT), BF16),
                pltpu.VMEM((D_EXPERT, D_MODEL), BF16),
            ],
        ),
        out_shape=jax.ShapeDtypeStruct((n_slots, D_MODEL), F32),
        compiler_params=_cparams(("arbitrary",)),
        name="experts",
    )(block_expert, n_used, xs, wgu_all, bg, bu, wd_all, bd, sel_g, sel_u)


def _combine_kernel(lrow_ref, grow_ref, tot_ref, ys_hbm, x_ref, rt_ref, mod_ref, o_ref, yl_ref, sem_ref,
                    *, n_tiles):
    i = pl.program_id(0)
    slot = i % 2

    def seg_copy(local_row, global_row, s):
        return pltpu.make_async_copy(ys_hbm.at[pl.ds(global_row, SEG), :],
                                     yl_ref.at[s, pl.ds(local_row, SEG), :], sem_ref.at[s])

    def fetch_tile(tile, s):
        _for_each_piece(tile, lrow_ref, grow_ref, tot_ref, lambda lr, gr: seg_copy(lr, gr, s).start())

    @pl.when(i == 0)
    def _():
        yl_ref[...] = jnp.zeros_like(yl_ref)
        fetch_tile(0, 0)

    @pl.when(i + 1 < n_tiles)
    def _():
        fetch_tile(i + 1, 1 - slot)

    def one(g, c):
        seg_copy(0, 0, slot).wait()
        return c

    lax.fori_loop(0, tot_ref[i], one, 0)

    rt = rt_ref[...]
    weights = jnp.zeros((TM, LOCAL_ROWS), F32)
    for k, hit in enumerate(_local_rows(rt)):
        weights = weights + jnp.where(hit, rt[:, 2 * TOP_K + k:2 * TOP_K + k + 1], 0.0)
    w_hi = weights.astype(BF16)
    w_lo = (weights - w_hi.astype(F32)).astype(BF16)
    yl = yl_ref[slot].astype(BF16)
    y = _dot(w_hi, yl) + _dot(w_lo, yl)
    o_ref[...] = (x_ref[...] + mod_ref[0][:, 5 * D_MODEL:6 * D_MODEL] * y).reshape(o_ref.shape)


def _combine(x, ys, route, seg_meta, mod_tiles, out_seq=None):
    tok = x.shape[0]
    nt = tok // TM
    tile = lambda i, *_: (i, 0)
    if out_seq is None:
        out_spec = pl.BlockSpec((TM, D_MODEL), tile)
        out_shape = jax.ShapeDtypeStruct((tok, D_MODEL), F32)
    else:
        n_batch, tps = out_seq
        out_spec = pl.BlockSpec((1, TM, D_MODEL), lambda i, *_: (i // tps, jnp.maximum(i % tps - 1, 0), 0))
        out_shape = jax.ShapeDtypeStruct((n_batch, (tps - 1) * TM, D_MODEL), F32)
    return pl.pallas_call(
        functools.partial(_combine_kernel, n_tiles=nt),
        grid_spec=pltpu.PrefetchScalarGridSpec(
            num_scalar_prefetch=3,
            grid=(nt,),
            in_specs=[
                pl.BlockSpec(memory_space=pl.ANY),
                pl.BlockSpec((TM, D_MODEL), tile),
                pl.BlockSpec((TM, LANES), tile),
                pl.BlockSpec((1, 1, 6 * D_MODEL), lambda i, *_: (i, 0, 0)),
            ],
            out_specs=out_spec,
            scratch_shapes=[
                pltpu.VMEM((2, LOCAL_ROWS, D_MODEL), F32),
                pltpu.SemaphoreType.DMA((2,)),
            ],
        ),
        out_shape=out_shape,
        compiler_params=_cparams(("arbitrary",)),
        name="combine",
    )(*seg_meta, ys, x, route, mod_tiles)


def _rope_tables(t_len):
    n_lat = t_len - N_CTX
    t = np.arange(n_lat)
    n_pairs = HEAD_DIM // 4
    inv_freq = jnp.asarray(ROPE_BASE, F32) ** (-jnp.arange(n_pairs, dtype=F32) / n_pairs)
    row = jnp.asarray(t // GRID_W, F32)
    col = jnp.asarray(t % GRID_W, F32)
    ang = jnp.concatenate([row[:, None] * inv_freq, col[:, None] * inv_freq], axis=-1)
    ang = jnp.concatenate([jnp.zeros((N_CTX, HEAD_DIM // 2), F32), ang], axis=0)
    cos = jnp.tile(jnp.cos(ang), (1, LANES // (HEAD_DIM // 2)))
    sin = jnp.tile(jnp.sin(ang), (1, LANES // (HEAD_DIM // 2)))
    return cos, sin


def _s5_operands(lam_re, lam_im, log_dt, b_re, b_im, c_re, c_im):
    lam = lax.complex(lam_re, lam_im)
    dt = jnp.exp(log_dt)[..., None]
    lam_bar = jnp.exp(lam * dt)
    b_bar = ((lam_bar - 1.0) / lam)[..., None] * lax.complex(b_re, b_im)
    eye = jnp.eye(N_SSM_GROUPS, dtype=F32)

    def in_block(w):
        return jnp.einsum("zgph,gk->zghkp", w, eye).reshape(2, D_SSM, N_STATE)

    def out_block(w):
        return jnp.einsum("zghp,gk->zgpkh", w, eye).reshape(2, N_STATE, D_SSM)

    bd = jnp.concatenate([in_block(b_bar.real), in_block(b_bar.imag)], axis=-1).astype(BF16)
    cd = jnp.concatenate([out_block(c_re), out_block(-c_im)], axis=1).astype(BF16)
    lam_rows = jnp.stack([lam_bar[0].real.reshape(-1), lam_bar[0].imag.reshape(-1),
                          lam_bar[1].real.reshape(-1), lam_bar[1].imag.reshape(-1)])
    return bd, cd, lam_rows


def _moe_constants():
    tri = jnp.asarray(np.tril(np.ones((TM, TM)), -1), BF16)
    triu = jnp.asarray(np.triu(np.ones((LANES, LANES)), 1), BF16)
    pick = np.arange(DEINT)[:, None] == 2 * np.arange(DEINT // 2)[None, :]
    sel_g = jnp.asarray(pick, BF16)
    sel_u = jnp.asarray(np.roll(pick, 1, axis=0), BF16)
    return tri, triu, sel_g, sel_u


def _moe(xs, h2, route, counts, mod_tiles, layer, w_gate_up, b_gate_up, w_down, b_down, consts, out_seq=None):
    _, _, sel_g, sel_u = consts
    tok = xs.shape[0]
    nt = tok // TM
    n_blocks = (tok * TOP_K + nt * N_EXPERTS * (SEG - 1) + N_EXPERTS * (MOE_BLOCK - 1)) // MOE_BLOCK
    n_slots = n_blocks * MOE_BLOCK

    counts = counts.reshape(nt, 8, LANES)[:, 0, :N_EXPERTS].astype(I32)
    seg = (counts + SEG - 1) // SEG * SEG
    region = jnp.sum(seg, axis=0)
    padded = (region + MOE_BLOCK - 1) // MOE_BLOCK * MOE_BLOCK
    pad_end = jnp.cumsum(padded)
    pad_start = pad_end - padded
    run_global = pad_start[None, :] + jnp.cumsum(seg, axis=0) - seg
    run_local = jnp.cumsum(seg, axis=1) - seg
    pieces = seg // SEG
    ends = jnp.cumsum(pieces, axis=1)
    piece = jnp.arange(MAX_PIECES, dtype=I32)
    owner = jnp.minimum(jnp.sum(piece[None, None, :] >= ends[:, :, None], axis=1), N_EXPERTS - 1)
    within = (piece[None, :] - jnp.take_along_axis(ends - pieces, owner, axis=1)) * SEG
    piece_local = jnp.take_along_axis(run_local, owner, axis=1) + within
    piece_global = jnp.take_along_axis(run_global, owner, axis=1) + within
    seg_meta = (piece_local.reshape(-1).astype(I32), piece_global.reshape(-1).astype(I32), ends[:, -1].astype(I32))
    n_used = (pad_end[-1] // MOE_BLOCK).astype(I32)
    blk = jnp.minimum(jnp.arange(n_blocks, dtype=I32), n_used - 1) * MOE_BLOCK
    block_expert = jnp.minimum(jnp.sum(blk[:, None] >= pad_end[None, :], axis=1), N_EXPERTS - 1).astype(I32)
    zstart = jnp.where(region > 0, pad_end - MOE_BLOCK, -1).astype(I32)

    x_sorted = _dispatch(h2, route, seg_meta, zstart, n_used.reshape(1), n_slots)
    y_sorted = _experts(x_sorted, block_expert, n_used.reshape(1), layer, w_gate_up,
                        b_gate_up[:, None, 0::2], b_gate_up[:, None, 1::2],
                        w_down, b_down[:, None, :], sel_g, sel_u)
    return _combine(xs, y_sorted, route, seg_meta, mod_tiles, out_seq)


def kernel(x, c, ctx, c_ctx, w_mod, b_mod, norm1_g, norm2_g, w_in, ssm_lam_re, ssm_lam_im, ssm_log_dt, ssm_b_re, ssm_b_im, ssm_c_re, ssm_c_im, ssm_d, w_glu, w_ssm_out, conv_w, w_conv_out, q_norm_g, k_norm_g, attn_sinks, w_attn_out, w_o, w_router, b_router, w_gate_up, b_gate_up, w_down, b_down):
    n_batch, n_lat, _ = x.shape
    t_len = N_CTX + n_lat
    tiles_per_seq = t_len // TM
    tok = n_batch * t_len
    nt = tok // TM

    xs = _assemble(ctx, x, tiles_per_seq)

    mod_rows = 8 * ((n_batch + 1 + 7) // 8)
    cvec = jnp.zeros((mod_rows, D_MODEL), F32).at[:n_batch].set(c).at[n_batch].set(c_ctx)
    mod_all = _modulation(cvec, w_mod, b_mod)
    tile_ids = np.arange(nt)
    tile_row = np.where(tile_ids % tiles_per_seq == 0, n_batch, tile_ids // tiles_per_seq)

    cos_t, sin_t = _rope_tables(t_len)
    head_sum = jnp.asarray(np.kron(np.eye(N_Q_HEADS), np.full((HEAD_DIM, HEAD_DIM), 1.0 / HEAD_DIM)), BF16)
    moe_consts = _moe_constants()

    for l in range(DEPTH):
        mod_tiles = mod_all[l][tile_row].reshape(nt, 1, 6 * D_MODEL)
        qg = jnp.tile(q_norm_g[l], N_Q_HEADS).reshape(1, D_Q)
        kg = jnp.tile(k_norm_g[l], N_KV_HEADS).reshape(1, D_KV)
        u_tb, cz, q, kv, gt = _in_proj(xs, mod_tiles, norm1_g[l].reshape(1, D_MODEL), w_in[l].astype(BF16),
                                       cos_t, sin_t, qg, kg, head_sum, n_batch, tiles_per_seq)

        bd, cd, lam_rows = _s5_operands(ssm_lam_re[l], ssm_lam_im[l], ssm_log_dt[l], ssm_b_re[l], ssm_b_im[l],
                                        ssm_c_re[l], ssm_c_im[l])
        yf, yb = _s5(u_tb, bd, cd, lam_rows, n_batch, t_len)
        ya = _attention(q, kv, attn_sinks[l], n_batch, t_len)

        wr = jnp.zeros((D_MODEL, LANES), F32).at[:, :N_EXPERTS].set(w_router[l])
        wr_hi = wr.astype(BF16)
        wr_lo = (wr - wr_hi.astype(F32)).astype(BF16)
        b_r = jnp.full((1, LANES), NEG, F32).at[0, :N_EXPERTS].set(b_router[l])
        xs, h2, route, counts = _merge(
            xs, yf, yb, u_tb,
            cz, ya, gt, mod_tiles, ssm_d[l].reshape(1, D_SSM), conv_w[l],
            w_glu[l].astype(BF16), w_ssm_out[l].astype(BF16), w_conv_out[l].astype(BF16),
            w_attn_out[l].astype(BF16), w_o[l].astype(BF16), norm2_g[l].reshape(1, D_MODEL),
            wr_hi, wr_lo, b_r, moe_consts[0], moe_consts[1], tiles_per_seq)

        out_seq = (n_batch, tiles_per_seq) if l == DEPTH - 1 else None
        xs = _moe(xs, h2, route, counts, mod_tiles, l, w_gate_up, b_gate_up[l], w_down, b_down[l], moe_consts,
                  out_seq)

    return xs
```

```python
import functools
import math

import jax
import jax.numpy as jnp
import numpy as np
from jax import lax
from jax.experimental import pallas as pl
from jax.experimental.pallas import tpu as pltpu

F32 = jnp.float32
BF16 = jnp.bfloat16
I32 = jnp.int32

D_MODEL = 1024
DEPTH = 4
N_CTX = 256
HEAD_DIM = 64
N_Q_HEADS = 8
N_KV_HEADS = 2
D_Q = N_Q_HEADS * HEAD_DIM
D_KV = N_KV_HEADS * HEAD_DIM
WINDOW = 128
ATTN_BLOCK = 128
ROPE_BASE = 10000.0
GRID_W = 64
D_SSM = 256
SSM_GROUP = 16
N_SSM_GROUPS = 16
SSM_STATE = 64
N_STATE = N_SSM_GROUPS * SSM_STATE
D_CONV = 256
N_BRANCH = 3
D_IN = D_SSM + 3 * D_CONV + D_Q + 2 * D_KV + N_BRANCH * D_MODEL
N_EXPERTS = 32
TOP_K = 4
D_EXPERT = 1024
SWIGLU_LIMIT = 7.0
SWIGLU_ALPHA = 1.702
EPS = 1e-6

LANES = 128
TM = 256
MOE_BLOCK = 512
SEG = 8
LOCAL_ROWS = -(-(TM * TOP_K + N_EXPERTS * (SEG - 1)) // LANES) * LANES
S5_CHUNK = 64
S5_COLS = 512
NEG = -1e30
VMEM_LIMIT = 56 * 1024 * 1024


def _cparams(sem):
    return pltpu.CompilerParams(dimension_semantics=sem, vmem_limit_bytes=VMEM_LIMIT)


def _dot(a, b):
    return jnp.dot(a, b, preferred_element_type=F32)


def _sigmoid(x):
    return 1.0 / (1.0 + jnp.exp(-x))


def _rms(x, g):
    ms = jnp.mean(x * x, axis=-1, keepdims=True)
    return x * lax.rsqrt(ms + EPS) * g


def _mod_kernel(c_ref, w_ref, b_ref, o_ref):
    c = c_ref[...]
    s = (c * _sigmoid(c)).astype(BF16)
    o_ref[0] = _dot(s, w_ref[0].astype(BF16)) + b_ref[0]


def _modulation(cvec, w_mod, b_mod):
    rows = cvec.shape[0]
    nblk = 1536
    return pl.pallas_call(
        _mod_kernel,
        grid=(DEPTH, 6 * D_MODEL // nblk),
        in_specs=[
            pl.BlockSpec((rows, D_MODEL), lambda l, j: (0, 0)),
            pl.BlockSpec((1, D_MODEL, nblk), lambda l, j: (l, 0, j)),
            pl.BlockSpec((1, 1, nblk), lambda l, j: (l, 0, j)),
        ],
        out_specs=pl.BlockSpec((1, rows, nblk), lambda l, j: (l, 0, j)),
        out_shape=jax.ShapeDtypeStruct((DEPTH, rows, 6 * D_MODEL), F32),
        compiler_params=_cparams(("arbitrary", "arbitrary")),
        name="modulation",
    )(cvec, w_mod, b_mod.reshape(DEPTH, 1, 6 * D_MODEL))


def _assemble_kernel(c_ref, x_ref, o_ref, *, tiles_per_seq):
    r = pl.program_id(0) % tiles_per_seq

    @pl.when(r == 0)
    def _():
        o_ref[...] = c_ref[0]

    @pl.when(r > 0)
    def _():
        o_ref[...] = x_ref[0]


def _assemble(ctx, x, tiles_per_seq):
    n_batch = x.shape[0]
    nt = n_batch * tiles_per_seq
    return pl.pallas_call(
        functools.partial(_assemble_kernel, tiles_per_seq=tiles_per_seq),
        grid=(nt,),
        in_specs=[
            pl.BlockSpec((1, TM, D_MODEL), lambda i: (i // tiles_per_seq, 0, 0)),
            pl.BlockSpec((1, TM, D_MODEL), lambda i: (i // tiles_per_seq, jnp.maximum(i % tiles_per_seq - 1, 0), 0)),
        ],
        out_specs=pl.BlockSpec((TM, D_MODEL), lambda i: (i, 0)),
        out_shape=jax.ShapeDtypeStruct((nt * TM, D_MODEL), F32),
        compiler_params=_cparams(("arbitrary",)),
        name="assemble",
    )(ctx, x)


def _rot_half(x, width):
    lane = lax.broadcasted_iota(I32, x.shape, 1)
    first = (lane % HEAD_DIM) < (HEAD_DIM // 2)
    return jnp.where(first, -pltpu.roll(x, width - HEAD_DIM // 2, axis=1), pltpu.roll(x, HEAD_DIM // 2, axis=1))


def _inproj_kernel(x_ref, mod_ref, g_ref, w_ref, cos_ref, sin_ref, qg_ref, kg_ref, hs_ref,
                   u_ref, cz_ref, q_ref, kv_ref, gt_ref):
    m = mod_ref[0]
    h = _rms(x_ref[...], g_ref[...])
    h = (h * (1.0 + m[:, D_MODEL:2 * D_MODEL]) + m[:, 0:D_MODEL]).astype(BF16)
    o_gate = D_SSM + 3 * D_CONV + D_Q + 2 * D_KV
    y = _dot(h, w_ref[:, 0:o_gate])
    u_ref[...] = y[:, 0:D_SSM].astype(BF16)
    cb = y[:, D_SSM:D_SSM + D_CONV]
    cc = y[:, D_SSM + D_CONV:D_SSM + 2 * D_CONV]
    cx = y[:, D_SSM + 2 * D_CONV:D_SSM + 3 * D_CONV]
    cz_ref[...] = jnp.concatenate([cb, cc * cx], axis=-1).astype(BF16)
    o_q = D_SSM + 3 * D_CONV
    q = y[:, o_q:o_q + D_Q]
    k = y[:, o_q + D_Q:o_q + D_Q + D_KV]
    v = y[:, o_q + D_Q + D_KV:o_gate]
    cos = cos_ref[...]
    sin = sin_ref[...]
    q_ms = _dot((q * q).astype(BF16), hs_ref[...])
    qn = q * lax.rsqrt(q_ms + EPS) * qg_ref[...]
    cos_q = jnp.concatenate([cos] * (D_Q // LANES), axis=-1)
    sin_q = jnp.concatenate([sin] * (D_Q // LANES), axis=-1)
    qr = qn * cos_q + _rot_half(qn, D_Q) * sin_q
    q_ref[...] = (qr * (HEAD_DIM ** -0.5)).astype(BF16)
    k_ms = _dot((k * k).astype(BF16), hs_ref[0:D_KV, 0:D_KV])
    kn = k * lax.rsqrt(k_ms + EPS) * kg_ref[...]
    kr = kn * cos + _rot_half(kn, D_KV) * sin
    kv_ref[...] = jnp.concatenate([kr, v], axis=-1).astype(BF16)
    gt = _dot(h, w_ref[:, o_gate:D_IN])
    gt_ref[...] = _sigmoid(gt).astype(BF16)


def _in_proj(x, mod_tiles, norm_g, w_in_bf, cos_t, sin_t, qg, kg, head_sum, n_batch, tiles_per_seq):
    tok = x.shape[0]
    nt = tok // TM
    t_len = tiles_per_seq * TM
    tile = lambda i: (i, 0)
    const = lambda i: (0, 0)
    seq_tile = lambda i: (i % tiles_per_seq, 0)
    return pl.pallas_call(
        _inproj_kernel,
        grid=(nt,),
        in_specs=[
            pl.BlockSpec((TM, D_MODEL), tile),
            pl.BlockSpec((1, 1, 6 * D_MODEL), lambda i: (i, 0, 0)),
            pl.BlockSpec((1, D_MODEL), const),
            pl.BlockSpec((D_MODEL, D_IN), const),
            pl.BlockSpec((TM, LANES), seq_tile),
            pl.BlockSpec((TM, LANES), seq_tile),
            pl.BlockSpec((1, D_Q), const),
            pl.BlockSpec((1, D_KV), const),
            pl.BlockSpec((D_Q, D_Q), const),
        ],
        out_specs=[
            pl.BlockSpec((TM, D_SSM), lambda i: (i % tiles_per_seq, i // tiles_per_seq)),
            pl.BlockSpec((TM, 2 * D_CONV), tile),
            pl.BlockSpec((TM, D_Q), tile),
            pl.BlockSpec((TM, 2 * D_KV), tile),
            pl.BlockSpec((TM, N_BRANCH * D_MODEL), tile),
        ],
        out_shape=[
            jax.ShapeDtypeStruct((t_len, n_batch * D_SSM), BF16),
            jax.ShapeDtypeStruct((tok, 2 * D_CONV), BF16),
            jax.ShapeDtypeStruct((tok, D_Q), BF16),
            jax.ShapeDtypeStruct((tok, 2 * D_KV), BF16),
            jax.ShapeDtypeStruct((tok, N_BRANCH * D_MODEL), BF16),
        ],
        compiler_params=_cparams(("parallel",)),
        name="in_proj",
    )(x, mod_tiles, norm_g, w_in_bf, cos_t, sin_t, qg, kg, head_sum)


def _s5_kernel(uf_ref, ub_ref, bd_ref, cd_ref, lam_ref, yf_ref, yb_ref, sf_ref, sb_ref, rows_ref, carry_ref,
               *, n_batch):
    i = pl.program_id(0)

    @pl.when(i == 0)
    def _():
        carry_ref[...] = jnp.zeros_like(carry_ref)

    def batch_rows(b):
        return pl.ds(b, S5_CHUNK, stride=n_batch)

    for z, (u_ref, s_ref, y_ref) in enumerate(((uf_ref, sf_ref, yf_ref), (ub_ref, sb_ref, yb_ref))):
        for b in range(n_batch):
            for h in range(D_SSM // LANES):
                lanes = pl.ds(b * D_SSM + h * LANES, LANES)
                rows_ref[h, batch_rows(b), :] = u_ref[:, lanes].astype(F32)
        u_rows = jnp.concatenate([rows_ref[h] for h in range(D_SSM // LANES)], axis=-1)
        s_ref[...] = _dot(u_rows.astype(BF16), bd_ref[z])
        for j in range(N_STATE // S5_COLS):
            re_cols = pl.ds(j * S5_COLS, S5_COLS)
            im_cols = pl.ds(N_STATE + j * S5_COLS, S5_COLS)
            lr = jnp.broadcast_to(lam_ref[2 * z:2 * z + 1, re_cols], (n_batch, S5_COLS))
            li = jnp.broadcast_to(lam_ref[2 * z + 1:2 * z + 2, re_cols], (n_batch, S5_COLS))

            def step(s, c, s_ref=s_ref, z=z, re_cols=re_cols, im_cols=im_cols, lr=lr, li=li):
                t = s if z == 0 else S5_CHUNK - 1 - s
                rows = pl.ds(pl.multiple_of(t * n_batch, n_batch), n_batch)
                xr, xi = c
                nr = lr * xr - li * xi + s_ref[rows, re_cols]
                ni = lr * xi + li * xr + s_ref[rows, im_cols]
                s_ref[rows, re_cols] = nr
                s_ref[rows, im_cols] = ni
                return nr, ni

            c0 = (carry_ref[2 * z, :, re_cols], carry_ref[2 * z + 1, :, re_cols])
            fr, fi = lax.fori_loop(0, S5_CHUNK, step, c0, unroll=True)
            carry_ref[2 * z, :, re_cols] = fr
            carry_ref[2 * z + 1, :, re_cols] = fi
        y_rows = _dot(s_ref[...].astype(BF16), cd_ref[z])
        for h in range(D_SSM // LANES):
            rows_ref[h] = y_rows[:, h * LANES:(h + 1) * LANES]
        for b in range(n_batch):
            for h in range(D_SSM // LANES):
                y_ref[:, pl.ds(b * D_SSM + h * LANES, LANES)] = rows_ref[h, batch_rows(b), :]


def _s5(u_tb, bd, cd, lam, n_batch, t_len):
    rows = S5_CHUNK * n_batch
    width = n_batch * D_SSM
    n_chunks = t_len // S5_CHUNK
    ctx_chunks = N_CTX // S5_CHUNK

    def bwd_block(i):
        return (jnp.where(i < ctx_chunks, ctx_chunks - 1 - i, n_chunks - 1 + ctx_chunks - i), 0)

    return pl.pallas_call(
        functools.partial(_s5_kernel, n_batch=n_batch),
        grid=(n_chunks,),
        in_specs=[
            pl.BlockSpec((S5_CHUNK, width), lambda i: (i, 0)),
            pl.BlockSpec((S5_CHUNK, width), bwd_block),
            pl.BlockSpec((2, D_SSM, 2 * N_STATE), lambda i: (0, 0, 0)),
            pl.BlockSpec((2, 2 * N_STATE, D_SSM), lambda i: (0, 0, 0)),
            pl.BlockSpec((4, N_STATE), lambda i: (0, 0)),
        ],
        out_specs=[
            pl.BlockSpec((S5_CHUNK, width), lambda i: (i, 0)),
            pl.BlockSpec((S5_CHUNK, width), bwd_block),
        ],
        out_shape=[jax.ShapeDtypeStruct((t_len, width), F32)] * 2,
        scratch_shapes=[
            pltpu.VMEM((rows, 2 * N_STATE), F32),
            pltpu.VMEM((rows, 2 * N_STATE), F32),
            pltpu.VMEM((D_SSM // LANES, rows, LANES), F32),
            pltpu.VMEM((4, n_batch, N_STATE), F32),
        ],
        compiler_params=_cparams(("arbitrary",)),
        name="s5_scan",
    )(u_tb, u_tb, bd, cd, lam)


def _attn_kernel(sink_ref, q_ref, kvc_ref, kv0_ref, kv1_ref, kv2_ref, wb_ref, o_ref, *, n_lat):
    n = pl.program_id(1) - N_CTX // ATTN_BLOCK
    rep = N_Q_HEADS // N_KV_HEADS
    rows = rep * ATTN_BLOCK
    q = q_ref[0]
    kvc = kvc_ref[0]
    band = jnp.concatenate([kv0_ref[0], kv1_ref[0], kv2_ref[0]], axis=0)
    key_block = n - 1 + lax.broadcasted_iota(I32, (1, 3 * ATTN_BLOCK), 1) // ATTN_BLOCK
    block_ok = (key_block >= 0) & (key_block < n_lat // ATTN_BLOCK) & (n >= 0)
    bias = wb_ref[...] + jnp.where(block_ok, 0.0, NEG)
    head_of_row = lax.broadcasted_iota(I32, (rows, 1), 0) // ATTN_BLOCK
    contract_last = (((1,), (1,)), ((), ()))
    outs = []
    for g in range(N_KV_HEADS):
        qg = jnp.concatenate([q[:, (g * rep + r) * HEAD_DIM:(g * rep + r + 1) * HEAD_DIM] for r in range(rep)], axis=0)
        sink = jnp.zeros((rows, 1), F32)
        for r in range(rep):
            sink = jnp.where(head_of_row == r, sink_ref[g * rep + r], sink)
        kc = kvc[:, g * HEAD_DIM:(g + 1) * HEAD_DIM]
        vc = kvc[:, D_KV + g * HEAD_DIM:D_KV + (g + 1) * HEAD_DIM]
        kb = band[:, g * HEAD_DIM:(g + 1) * HEAD_DIM]
        vb = band[:, D_KV + g * HEAD_DIM:D_KV + (g + 1) * HEAD_DIM]
        sc = lax.dot_general(qg, kc, contract_last, preferred_element_type=F32)
        sb = lax.dot_general(qg, kb, contract_last, preferred_element_type=F32)
        sb = sb + bias
        mx = jnp.maximum(jnp.maximum(jnp.max(sc, axis=-1, keepdims=True), jnp.max(sb, axis=-1, keepdims=True)), sink)
        pc = jnp.exp(sc - mx)
        pb = jnp.exp(sb - mx)
        den = jnp.sum(pc, axis=-1, keepdims=True) + jnp.sum(pb, axis=-1, keepdims=True) + jnp.exp(sink - mx)
        o = (_dot(pc.astype(BF16), vc) + _dot(pb.astype(BF16), vb)) / den
        outs.extend(o[r * ATTN_BLOCK:(r + 1) * ATTN_BLOCK] for r in range(rep))
    o_ref[0] = jnp.concatenate(outs, axis=-1).astype(BF16)


def _attention(q, kv, sinks, n_batch, t_len):
    nqb = t_len // ATTN_BLOCK
    first = N_CTX // ATTN_BLOCK
    q3 = q.reshape(n_batch, t_len, D_Q)
    kv3 = kv.reshape(n_batch, t_len, 2 * D_KV)

    def band(off):
        return lambda b, j, s: (b, jnp.clip(j + off, first, nqb - 1), 0)

    rep = N_Q_HEADS // N_KV_HEADS
    iq = np.arange(rep * ATTN_BLOCK)[:, None] % ATTN_BLOCK
    ik = np.arange(3 * ATTN_BLOCK)[None, :]
    window_bias = jnp.asarray(np.where(np.abs(iq + ATTN_BLOCK - ik) <= WINDOW, 0.0, NEG), F32)

    out = pl.pallas_call(
        functools.partial(_attn_kernel, n_lat=t_len - N_CTX),
        grid_spec=pltpu.PrefetchScalarGridSpec(
            num_scalar_prefetch=1,
            grid=(n_batch, nqb),
            in_specs=[
                pl.BlockSpec((1, ATTN_BLOCK, D_Q), lambda b, j, s: (b, j, 0)),
                pl.BlockSpec((1, N_CTX, 2 * D_KV), lambda b, j, s: (b, 0, 0)),
                pl.BlockSpec((1, ATTN_BLOCK, 2 * D_KV), band(-1)),
                pl.BlockSpec((1, ATTN_BLOCK, 2 * D_KV), band(0)),
                pl.BlockSpec((1, ATTN_BLOCK, 2 * D_KV), band(1)),
                pl.BlockSpec((rep * ATTN_BLOCK, 3 * ATTN_BLOCK), lambda b, j, s: (0, 0)),
            ],
            out_specs=pl.BlockSpec((1, ATTN_BLOCK, D_Q), lambda b, j, s: (b, j, 0)),
        ),
        out_shape=jax.ShapeDtypeStruct((n_batch, t_len, D_Q), BF16),
        compiler_params=_cparams(("parallel", "parallel")),
        name="attention",
    )(sinks, q3, kv3, kv3, kv3, kv3, window_bias)
    return out.reshape(n_batch * t_len, D_Q)


def _gelu_tanh(x):
    return 0.5 * x * (1.0 + jnp.tanh(math.sqrt(2.0 / math.pi) * (x + 0.044715 * (x * x * x))))


def _merge_kernel(x_ref, yf_ref, yb_ref, u_ref, cz_ref, czp_ref, czn_ref, ya_ref, gt_ref, mod_ref,
                  d_ref, cw_ref, wglu_ref, wso_ref, wco_ref, wao_ref, wo_ref, n2g_ref,
                  wrh_ref, wrl_ref, br_ref, tri_ref, triu_ref, xo_ref, h2_ref, rt_ref, cnt_ref, *, tiles_per_seq):
    r = pl.program_id(0) % tiles_per_seq
    ys = yf_ref[...] + yb_ref[...] + d_ref[...] * u_ref[...].astype(F32)
    z = _gelu_tanh(ys)
    glu = z * _sigmoid(_dot(z.astype(BF16), wglu_ref[...]))
    br_ssm = _dot(glu.astype(BF16), wso_ref[...])

    cz = cz_ref[...].astype(F32)
    cb = cz[:, 0:D_CONV]
    zz = cz[:, D_CONV:2 * D_CONV]
    seg_first = r <= 1
    seg_last = (r == 0) | (r == tiles_per_seq - 1)
    prev_row = jnp.where(seg_first, 0.0, czp_ref[7:8, D_CONV:2 * D_CONV].astype(F32))
    next_row = jnp.where(seg_last, 0.0, czn_ref[0:1, D_CONV:2 * D_CONV].astype(F32))
    row = lax.broadcasted_iota(I32, (TM, D_CONV), 0)
    z_dn = jnp.where(row == 0, prev_row, pltpu.roll(zz, 1, axis=0))
    z_up = jnp.where(row == TM - 1, next_row, pltpu.roll(zz, TM - 1, axis=0))
    y_conv = cb * (cw_ref[0:1, :] * z_dn + cw_ref[1:2, :] * zz + cw_ref[2:3, :] * z_up)
    br_conv = _dot(y_conv.astype(BF16), wco_ref[...])
    br_attn = _dot(ya_ref[...], wao_ref[...])

    merged = (gt_ref[:, 0:D_MODEL].astype(F32) * br_ssm
              + gt_ref[:, D_MODEL:2 * D_MODEL].astype(F32) * br_conv
              + gt_ref[:, 2 * D_MODEL:3 * D_MODEL].astype(F32) * br_attn)
    mix = _dot(merged.astype(BF16), wo_ref[...])
    m = mod_ref[0]
    xn = x_ref[...] + m[:, 2 * D_MODEL:3 * D_MODEL] * mix
    xo_ref[...] = xn
    h2 = _rms(xn, n2g_ref[...]) * (1.0 + m[:, 4 * D_MODEL:5 * D_MODEL]) + m[:, 3 * D_MODEL:4 * D_MODEL]
    hi = h2.astype(BF16)
    h2_ref[...] = hi
    lo = (h2 - hi.astype(F32)).astype(BF16)
    logits = _dot(hi, wrh_ref[...]) + _dot(lo, wrh_ref[...]) + _dot(hi, wrl_ref[...]) + br_ref[...]
    rt_ref[...], cnt_ref[...] = _route_tile(logits, tri_ref, triu_ref)


def _merge(x, yf, yb, u_tb, cz, ya, gt, mod_tiles, d_skip, conv_w, wglu, wso, wco, wao, wo, n2g,
           wr_hi, wr_lo, b_r, tri, triu, tiles_per_seq):
    tok = x.shape[0]
    nt = tok // TM
    tile = lambda i: (i, 0)
    const = lambda i: (0, 0)
    tb = lambda i: (i % tiles_per_seq, i // tiles_per_seq)
    rows8 = TM // 8
    return pl.pallas_call(
        functools.partial(_merge_kernel, tiles_per_seq=tiles_per_seq),
        grid=(nt,),
        in_specs=[
            pl.BlockSpec((TM, D_MODEL), tile),
            pl.BlockSpec((TM, D_SSM), tb),
            pl.BlockSpec((TM, D_SSM), tb),
            pl.BlockSpec((TM, D_SSM), tb),
            pl.BlockSpec((TM, 2 * D_CONV), tile),
            pl.BlockSpec((8, 2 * D_CONV), lambda i: (jnp.maximum(i * rows8 - 1, 0), 0)),
            pl.BlockSpec((8, 2 * D_CONV), lambda i: (jnp.minimum((i + 1) * rows8, tok // 8 - 1), 0)),
            pl.BlockSpec((TM, D_Q), tile),
            pl.BlockSpec((TM, N_BRANCH * D_MODEL), tile),
            pl.BlockSpec((1, 1, 6 * D_MODEL), lambda i: (i, 0, 0)),
            pl.BlockSpec((1, D_SSM), const),
            pl.BlockSpec((3, D_CONV), const),
            pl.BlockSpec((D_SSM, D_SSM), const),
            pl.BlockSpec((D_SSM, D_MODEL), const),
            pl.BlockSpec((D_CONV, D_MODEL), const),
            pl.BlockSpec((D_Q, D_MODEL), const),
            pl.BlockSpec((D_MODEL, D_MODEL), const),
            pl.BlockSpec((1, D_MODEL), const),
            pl.BlockSpec((D_MODEL, LANES), const),
            pl.BlockSpec((D_MODEL, LANES), const),
            pl.BlockSpec((1, LANES), const),
            pl.BlockSpec((TM, TM), const),
            pl.BlockSpec((LANES, LANES), const),
        ],
        out_specs=[
            pl.BlockSpec((TM, D_MODEL), tile),
            pl.BlockSpec((TM, D_MODEL), tile),
            pl.BlockSpec((TM, LANES), tile),
            pl.BlockSpec((8, LANES), tile),
        ],
        out_shape=[
            jax.ShapeDtypeStruct((tok, D_MODEL), F32),
            jax.ShapeDtypeStruct((tok, D_MODEL), BF16),
            jax.ShapeDtypeStruct((tok, LANES), F32),
            jax.ShapeDtypeStruct((nt * 8, LANES), F32),
        ],
        compiler_params=_cparams(("parallel",)),
        name="branch_merge",
    )(x, yf, yb, u_tb, cz, cz, cz, ya, gt, mod_tiles, d_skip, conv_w, wglu, wso, wco, wao, wo, n2g,
      wr_hi, wr_lo, b_r, tri, triu)


def _route_tile(l, tri_ref, triu_ref):
    lane = lax.broadcasted_iota(I32, l.shape, 1)
    vals, idxs, hots = [], [], []
    for _ in range(TOP_K):
        mx = jnp.max(l, axis=-1, keepdims=True)
        idx = jnp.min(jnp.where(l == mx, lane, LANES), axis=-1, keepdims=True)
        hot = lane == idx
        l = jnp.where(hot, -3e38, l)
        vals.append(mx)
        idxs.append(idx)
        hots.append(hot)
    ex = [jnp.exp(v - vals[0]) for v in vals]
    den = ex[0] + ex[1] + ex[2] + ex[3]
    picked = jnp.zeros(l.shape, F32)
    for hot in hots:
        picked = picked + hot.astype(F32)
    cum = _dot(tri_ref[...], picked.astype(BF16))
    cnt = jnp.sum(picked, axis=0, keepdims=True)
    seg = jnp.floor((cnt + (SEG - 1.0)) * (1.0 / SEG)) * SEG
    run_start = _dot(jnp.broadcast_to(seg, (8, LANES)).astype(BF16), triu_ref[...])[0:1]
    pos = cum + run_start
    out = jnp.zeros(l.shape, F32)
    for k in range(TOP_K):
        row = jnp.sum(jnp.where(hots[k], pos, 0.0), axis=-1, keepdims=True)
        out = jnp.where(lane == k, idxs[k].astype(F32), out)
        out = jnp.where(lane == TOP_K + k, row, out)
        out = jnp.where(lane == 2 * TOP_K + k, ex[k] / den, out)
    return out, jnp.broadcast_to(cnt, (8, LANES))


MAX_PIECES = LOCAL_ROWS // SEG


def _for_each_piece(tile, grow_ref, tot_ref, fn):
    base = tile * MAX_PIECES

    def per_piece(p, c):
        fn(pl.multiple_of(p * SEG, SEG), pl.multiple_of(grow_ref[base + p], SEG))
        return c

    lax.fori_loop(0, tot_ref[tile], per_piece, 0)


def _local_rows(rt):
    rows = lax.broadcasted_iota(I32, (TM, LOCAL_ROWS), 1)
    return [rows == rt[:, TOP_K + k:TOP_K + k + 1].astype(I32) for k in range(TOP_K)]


def _dispatch_kernel(grow_ref, tot_ref, zstart_ref, nu_ref, h_ref, rt_ref, xs_hbm,
                     xl_ref, zero_ref, sem_ref, sem_z, *, n_blocks, n_tiles):
    i = pl.program_id(0)
    slot = i % 2

    @pl.when(i == 0)
    def _():
        zero_ref[...] = jnp.zeros_like(zero_ref)

        def zero_copy(start):
            rows = pl.ds(pl.multiple_of(start, MOE_BLOCK), MOE_BLOCK)
            return pltpu.make_async_copy(zero_ref, xs_hbm.at[rows, :], sem_z)

        def tail_start(b, c):
            zero_copy(b * MOE_BLOCK).start()
            return c

        def tail_wait(b, c):
            zero_copy(b * MOE_BLOCK).wait()
            return c

        for e in range(N_EXPERTS):
            @pl.when(zstart_ref[e] >= 0)
            def _():
                zero_copy(zstart_ref[e]).start()
        lax.fori_loop(nu_ref[0], n_blocks, tail_start, 0)
        for e in range(N_EXPERTS):
            @pl.when(zstart_ref[e] >= 0)
            def _():
                zero_copy(zstart_ref[e]).wait()
        lax.fori_loop(nu_ref[0], n_blocks, tail_wait, 0)

    place = jnp.zeros((TM, LOCAL_ROWS), F32)
    for hit in _local_rows(rt_ref[...]):
        place = place + hit.astype(F32)
    xl_ref[slot] = lax.dot_general(place.astype(BF16), h_ref[...], (((0,), (0,)), ((), ())),
                                   preferred_element_type=F32)

    def seg_copy(local_row, global_row, s):
        return pltpu.make_async_copy(xl_ref.at[s, pl.ds(local_row, SEG), :],
                                     xs_hbm.at[pl.ds(global_row, SEG), :], sem_ref.at[s])

    def wait_tile(tile, s):
        def one(g, c):
            seg_copy(0, 0, s).wait()
            return c

        lax.fori_loop(0, tot_ref[tile], one, 0)

    _for_each_piece(i, grow_ref, tot_ref, lambda lr, gr: seg_copy(lr, gr, slot).start())

    @pl.when(i > 0)
    def _():
        wait_tile(i - 1, 1 - slot)

    @pl.when(i == n_tiles - 1)
    def _():
        wait_tile(i, slot)


def _dispatch(h2, route, seg_meta, zstart, n_used, n_slots):
    tok = h2.shape[0]
    nt = tok // TM
    tile = lambda i, *_: (i, 0)
    return pl.pallas_call(
        functools.partial(_dispatch_kernel, n_blocks=n_slots // MOE_BLOCK, n_tiles=nt),
        grid_spec=pltpu.PrefetchScalarGridSpec(
            num_scalar_prefetch=4,
            grid=(nt,),
            in_specs=[
                pl.BlockSpec((TM, D_MODEL), tile),
                pl.BlockSpec((TM, LANES), tile),
            ],
            out_specs=pl.BlockSpec(memory_space=pl.ANY),
            scratch_shapes=[
                pltpu.VMEM((2, LOCAL_ROWS, D_MODEL), F32),
                pltpu.VMEM((MOE_BLOCK, D_MODEL), F32),
                pltpu.SemaphoreType.DMA((2,)),
                pltpu.SemaphoreType.DMA,
            ],
        ),
        out_shape=jax.ShapeDtypeStruct((n_slots, D_MODEL), F32),
        compiler_params=_cparams(("arbitrary",)),
        name="dispatch",
    )(*seg_meta, zstart, n_used, h2, route)


DEINT = 256


def _expert_kernel(be_ref, nu_ref, x_ref, wgu_ref, bg_ref, bu_ref, wd_ref, bd_ref, sg_ref, su_ref, y_ref,
                   wg_s, wu_s, wd_s):
    i = pl.program_id(0)
    prev = be_ref[jnp.maximum(i - 1, 0)]

    @pl.when((i == 0) | (be_ref[i] != prev))
    def _():
        for c in range(2 * D_EXPERT // DEINT):
            w = wgu_ref[0, 0, :, c * DEINT:(c + 1) * DEINT].astype(BF16)
            cols = pl.ds(c * (DEINT // 2), DEINT // 2)
            wg_s[:, cols] = _dot(w, sg_ref[...]).astype(BF16)
            wu_s[:, cols] = _dot(w, su_ref[...]).astype(BF16)
        wd_s[...] = wd_ref[0, 0].astype(BF16)

    @pl.when(i < nu_ref[0])
    def _():
        x = x_ref[...].astype(BF16)
        g = _dot(x, wg_s[...]) + bg_ref[0]
        u = _dot(x, wu_s[...]) + bu_ref[0]
        glu = jnp.minimum(g, SWIGLU_LIMIT)
        up = jnp.clip(u, -SWIGLU_LIMIT, SWIGLU_LIMIT)
        act = glu * _sigmoid(SWIGLU_ALPHA * glu) * (up + 1.0)
        y_ref[...] = _dot(act.astype(BF16), wd_s[...]) + bd_ref[0]

    @pl.when(i >= nu_ref[0])
    def _():
        y_ref[...] = jnp.zeros_like(y_ref)


def _experts(xs, block_expert, n_used, layer, wgu_all, bg, bu, wd_all, bd, sel_g, sel_u):
    n_slots = xs.shape[0]
    n_blocks = n_slots // MOE_BLOCK
    wmap = lambda i, be, nu: (be[i], 0, 0)
    lwmap = lambda i, be, nu: (layer, be[i], 0, 0)
    const = lambda i, be, nu: (0, 0)
    return pl.pallas_call(
        _expert_kernel,
        grid_spec=pltpu.PrefetchScalarGridSpec(
            num_scalar_prefetch=2,
            grid=(n_blocks,),
            in_specs=[
                pl.BlockSpec((MOE_BLOCK, D_MODEL), lambda i, be, nu: (jnp.minimum(i, nu[0] - 1), 0)),
                pl.BlockSpec((1, 1, D_MODEL, 2 * D_EXPERT), lwmap),
                pl.BlockSpec((1, 1, D_EXPERT), wmap),
                pl.BlockSpec((1, 1, D_EXPERT), wmap),
                pl.BlockSpec((1, 1, D_EXPERT, D_MODEL), lwmap),
                pl.BlockSpec((1, 1, D_MODEL), wmap),
                pl.BlockSpec((DEINT, DEINT // 2), const),
                pl.BlockSpec((DEINT, DEINT // 2), const),
            ],
            out_specs=pl.BlockSpec((MOE_BLOCK, D_MODEL), lambda i, be, nu: (i, 0)),
            scratch_shapes=[
                pltpu.VMEM((D_MODEL, D_EXPERT), BF16),
                pltpu.VMEM((D_MODEL, D_EXPERT), BF16),
                pltpu.VMEM((D_EXPERT, D_MODEL), BF16),
            ],
        ),
        out_shape=jax.ShapeDtypeStruct((n_slots, D_MODEL), F32),
        compiler_params=_cparams(("arbitrary",)),
        name="experts",
    )(block_expert, n_used, xs, wgu_all, bg, bu, wd_all, bd, sel_g, sel_u)


def _combine_kernel(grow_ref, tot_ref, ys_hbm, x_ref, rt_ref, mod_ref, o_ref, yl_ref, sem_ref,
                    *, n_tiles):
    i = pl.program_id(0)
    slot = i % 2

    def seg_copy(local_row, global_row, s):
        return pltpu.make_async_copy(ys_hbm.at[pl.ds(global_row, SEG), :],
                                     yl_ref.at[s, pl.ds(local_row, SEG), :], sem_ref.at[s])

    def fetch_tile(tile, s):
        _for_each_piece(tile, grow_ref, tot_ref, lambda lr, gr: seg_copy(lr, gr, s).start())

    @pl.when(i == 0)
    def _():
        yl_ref[...] = jnp.zeros_like(yl_ref)
        fetch_tile(0, 0)

    @pl.when(i + 1 < n_tiles)
    def _():
        fetch_tile(i + 1, 1 - slot)

    def one(g, c):
        seg_copy(0, 0, slot).wait()
        return c

    lax.fori_loop(0, tot_ref[i], one, 0)

    rt = rt_ref[...]
    weights = jnp.zeros((TM, LOCAL_ROWS), F32)
    for k, hit in enumerate(_local_rows(rt)):
        weights = weights + jnp.where(hit, rt[:, 2 * TOP_K + k:2 * TOP_K + k + 1], 0.0)
    w_hi = weights.astype(BF16)
    w_lo = (weights - w_hi.astype(F32)).astype(BF16)
    yl = yl_ref[slot].astype(BF16)
    y = _dot(w_hi, yl) + _dot(w_lo, yl)
    o_ref[...] = (x_ref[...] + mod_ref[0][:, 5 * D_MODEL:6 * D_MODEL] * y).reshape(o_ref.shape)


def _combine(x, ys, route, seg_meta, mod_tiles, out_seq=None):
    tok = x.shape[0]
    nt = tok // TM
    tile = lambda i, *_: (i, 0)
    if out_seq is None:
        out_spec = pl.BlockSpec((TM, D_MODEL), tile)
        out_shape = jax.ShapeDtypeStruct((tok, D_MODEL), F32)
    else:
        n_batch, tps = out_seq
        out_spec = pl.BlockSpec((1, TM, D_MODEL), lambda i, *_: (i // tps, jnp.maximum(i % tps - 1, 0), 0))
        out_shape = jax.ShapeDtypeStruct((n_batch, (tps - 1) * TM, D_MODEL), F32)
    return pl.pallas_call(
        functools.partial(_combine_kernel, n_tiles=nt),
        grid_spec=pltpu.PrefetchScalarGridSpec(
            num_scalar_prefetch=2,
            grid=(nt,),
            in_specs=[
                pl.BlockSpec(memory_space=pl.ANY),
                pl.BlockSpec((TM, D_MODEL), tile),
                pl.BlockSpec((TM, LANES), tile),
                pl.BlockSpec((1, 1, 6 * D_MODEL), lambda i, *_: (i, 0, 0)),
            ],
            out_specs=out_spec,
            scratch_shapes=[
                pltpu.VMEM((2, LOCAL_ROWS, D_MODEL), F32),
                pltpu.SemaphoreType.DMA((2,)),
            ],
        ),
        out_shape=out_shape,
        compiler_params=_cparams(("arbitrary",)),
        name="combine",
    )(*seg_meta, ys, x, route, mod_tiles)


def _rope_tables(t_len):
    n_lat = t_len - N_CTX
    t = np.arange(n_lat)
    n_pairs = HEAD_DIM // 4
    inv_freq = jnp.asarray(ROPE_BASE, F32) ** (-jnp.arange(n_pairs, dtype=F32) / n_pairs)
    row = jnp.asarray(t // GRID_W, F32)
    col = jnp.asarray(t % GRID_W, F32)
    ang = jnp.concatenate([row[:, None] * inv_freq, col[:, None] * inv_freq], axis=-1)
    ang = jnp.concatenate([jnp.zeros((N_CTX, HEAD_DIM // 2), F32), ang], axis=0)
    cos = jnp.tile(jnp.cos(ang), (1, LANES // (HEAD_DIM // 2)))
    sin = jnp.tile(jnp.sin(ang), (1, LANES // (HEAD_DIM // 2)))
    return cos, sin


def _s5_operands(lam_re, lam_im, log_dt, b_re, b_im, c_re, c_im):
    lam = lax.complex(lam_re, lam_im)
    dt = jnp.exp(log_dt)[..., None]
    lam_bar = jnp.exp(lam * dt)
    b_bar = ((lam_bar - 1.0) / lam)[..., None] * lax.complex(b_re, b_im)
    eye = jnp.eye(N_SSM_GROUPS, dtype=F32)

    def in_block(w):
        return jnp.einsum("zgph,gk->zghkp", w, eye).reshape(2, D_SSM, N_STATE)

    def out_block(w):
        return jnp.einsum("zghp,gk->zgpkh", w, eye).reshape(2, N_STATE, D_SSM)

    bd = jnp.concatenate([in_block(b_bar.real), in_block(b_bar.imag)], axis=-1).astype(BF16)
    cd = jnp.concatenate([out_block(c_re), out_block(-c_im)], axis=1).astype(BF16)
    lam_rows = jnp.stack([lam_bar[0].real.reshape(-1), lam_bar[0].imag.reshape(-1),
                          lam_bar[1].real.reshape(-1), lam_bar[1].imag.reshape(-1)])
    return bd, cd, lam_rows


def _moe_constants():
    tri = jnp.asarray(np.tril(np.ones((TM, TM)), -1), BF16)
    triu = jnp.asarray(np.triu(np.ones((LANES, LANES)), 1), BF16)
    pick = np.arange(DEINT)[:, None] == 2 * np.arange(DEINT // 2)[None, :]
    sel_g = jnp.asarray(pick, BF16)
    sel_u = jnp.asarray(np.roll(pick, 1, axis=0), BF16)
    return tri, triu, sel_g, sel_u


def _moe(xs, h2, route, counts, mod_tiles, layer, w_gate_up, b_gate_up, w_down, b_down, consts, out_seq=None):
    _, _, sel_g, sel_u = consts
    tok = xs.shape[0]
    nt = tok // TM
    n_blocks = (tok * TOP_K + nt * N_EXPERTS * (SEG - 1) + N_EXPERTS * (MOE_BLOCK - 1)) // MOE_BLOCK
    n_slots = n_blocks * MOE_BLOCK

    counts = counts.reshape(nt, 8, LANES)[:, 0, :N_EXPERTS].astype(I32)
    seg = (counts + SEG - 1) // SEG * SEG
    region = jnp.sum(seg, axis=0)
    padded = (region + MOE_BLOCK - 1) // MOE_BLOCK * MOE_BLOCK
    pad_end = jnp.cumsum(padded)
    pad_start = pad_end - padded
    run_global = pad_start[None, :] + jnp.cumsum(seg, axis=0) - seg
    run_local = jnp.cumsum(seg, axis=1) - seg
    pieces = seg // SEG
    ends = jnp.cumsum(pieces, axis=1)
    shift = run_global - run_local
    jump = shift - jnp.concatenate([jnp.zeros((nt, 1), I32), shift[:, :-1]], axis=1)
    piece = jnp.arange(MAX_PIECES, dtype=I32)
    started = piece[None, None, :] >= (ends - pieces)[:, :, None]
    piece_global = piece[None, :] * SEG + jnp.sum(jnp.where(started, jump[:, :, None], 0), axis=1)
    seg_meta = (piece_global.reshape(-1).astype(I32), ends[:, -1].astype(I32))
    n_used = (pad_end[-1] // MOE_BLOCK).astype(I32)
    blk = jnp.minimum(jnp.arange(n_blocks, dtype=I32), n_used - 1) * MOE_BLOCK
    block_expert = jnp.minimum(jnp.sum(blk[:, None] >= pad_end[None, :], axis=1), N_EXPERTS - 1).astype(I32)
    zstart = jnp.where(region > 0, pad_end - MOE_BLOCK, -1).astype(I32)

    x_sorted = _dispatch(h2, route, seg_meta, zstart, n_used.reshape(1), n_slots)
    y_sorted = _experts(x_sorted, block_expert, n_used.reshape(1), layer, w_gate_up,
                        b_gate_up[:, None, 0::2], b_gate_up[:, None, 1::2],
                        w_down, b_down[:, None, :], sel_g, sel_u)
    return _combine(xs, y_sorted, route, seg_meta, mod_tiles, out_seq)


def kernel(x, c, ctx, c_ctx, w_mod, b_mod, norm1_g, norm2_g, w_in, ssm_lam_re, ssm_lam_im, ssm_log_dt, ssm_b_re, ssm_b_im, ssm_c_re, ssm_c_im, ssm_d, w_glu, w_ssm_out, conv_w, w_conv_out, q_norm_g, k_norm_g, attn_sinks, w_attn_out, w_o, w_router, b_router, w_gate_up, b_gate_up, w_down, b_down):
    n_batch, n_lat, _ = x.shape
    t_len = N_CTX + n_lat
    tiles_per_seq = t_len // TM
    tok = n_batch * t_len
    nt = tok // TM

    xs = _assemble(ctx, x, tiles_per_seq)

    mod_rows = 8 * ((n_batch + 1 + 7) // 8)
    cvec = jnp.zeros((mod_rows, D_MODEL), F32).at[:n_batch].set(c).at[n_batch].set(c_ctx)
    mod_all = _modulation(cvec, w_mod, b_mod)
    tile_ids = np.arange(nt)
    tile_row = np.where(tile_ids % tiles_per_seq == 0, n_batch, tile_ids // tiles_per_seq)

    cos_t, sin_t = _rope_tables(t_len)
    head_sum = jnp.asarray(np.kron(np.eye(N_Q_HEADS), np.full((HEAD_DIM, HEAD_DIM), 1.0 / HEAD_DIM)), BF16)
    moe_consts = _moe_constants()

    for l in range(DEPTH):
        mod_tiles = mod_all[l][tile_row].reshape(nt, 1, 6 * D_MODEL)
        qg = jnp.tile(q_norm_g[l], N_Q_HEADS).reshape(1, D_Q)
        kg = jnp.tile(k_norm_g[l], N_KV_HEADS).reshape(1, D_KV)
        u_tb, cz, q, kv, gt = _in_proj(xs, mod_tiles, norm1_g[l].reshape(1, D_MODEL), w_in[l].astype(BF16),
                                       cos_t, sin_t, qg, kg, head_sum, n_batch, tiles_per_seq)

        bd, cd, lam_rows = _s5_operands(ssm_lam_re[l], ssm_lam_im[l], ssm_log_dt[l], ssm_b_re[l], ssm_b_im[l],
                                        ssm_c_re[l], ssm_c_im[l])
        yf, yb = _s5(u_tb, bd, cd, lam_rows, n_batch, t_len)
        ya = _attention(q, kv, attn_sinks[l], n_batch, t_len)

        wr = jnp.zeros((D_MODEL, LANES), F32).at[:, :N_EXPERTS].set(w_router[l])
        wr_hi = wr.astype(BF16)
        wr_lo = (wr - wr_hi.astype(F32)).astype(BF16)
        b_r = jnp.full((1, LANES), NEG, F32).at[0, :N_EXPERTS].set(b_router[l])
        xs, h2, route, counts = _merge(
            xs, yf, yb, u_tb,
            cz, ya, gt, mod_tiles, ssm_d[l].reshape(1, D_SSM), conv_w[l],
            w_glu[l].astype(BF16), w_ssm_out[l].astype(BF16), w_conv_out[l].astype(BF16),
            w_attn_out[l].astype(BF16), w_o[l].astype(BF16), norm2_g[l].reshape(1, D_MODEL),
            wr_hi, wr_lo, b_r, moe_consts[0], moe_consts[1], tiles_per_seq)

        out_seq = (n_batch, tiles_per_seq) if l == DEPTH - 1 else None
        xs = _moe(xs, h2, route, counts, mod_tiles, l, w_gate_up, b_gate_up[l], w_down, b_down[l], moe_consts,
                  out_seq)

    return xs
```

```python
import functools
import math

import jax
import jax.numpy as jnp
import numpy as np
from jax import lax
from jax.experimental import pallas as pl
from jax.experimental.pallas import tpu as pltpu

F32 = jnp.float32
BF16 = jnp.bfloat16
I32 = jnp.int32

D_MODEL = 1024
DEPTH = 4
N_CTX = 256
HEAD_DIM = 64
N_Q_HEADS = 8
N_KV_HEADS = 2
D_Q = N_Q_HEADS * HEAD_DIM
D_KV = N_KV_HEADS * HEAD_DIM
WINDOW = 128
ATTN_BLOCK = 128
ROPE_BASE = 10000.0
GRID_W = 64
D_SSM = 256
SSM_GROUP = 16
N_SSM_GROUPS = 16
SSM_STATE = 64
N_STATE = N_SSM_GROUPS * SSM_STATE
D_CONV = 256
N_BRANCH = 3
D_IN = D_SSM + 3 * D_CONV + D_Q + 2 * D_KV + N_BRANCH * D_MODEL
N_EXPERTS = 32
TOP_K = 4
D_EXPERT = 1024
SWIGLU_LIMIT = 7.0
SWIGLU_ALPHA = 1.702
EPS = 1e-6

LANES = 128
TM = 256
MOE_BLOCK = 512
SEG = 8
LOCAL_ROWS = -(-(TM * TOP_K + N_EXPERTS * (SEG - 1)) // LANES) * LANES
S5_CHUNK = 64
S5_COLS = 512
NEG = -1e30
VMEM_LIMIT = 56 * 1024 * 1024


def _cparams(sem):
    return pltpu.CompilerParams(dimension_semantics=sem, vmem_limit_bytes=VMEM_LIMIT)


def _dot(a, b):
    return jnp.dot(a, b, preferred_element_type=F32)


def _sigmoid(x):
    return 1.0 / (1.0 + jnp.exp(-x))


def _rms(x, g):
    ms = jnp.mean(x * x, axis=-1, keepdims=True)
    return x * lax.rsqrt(ms + EPS) * g


def _mod_kernel(c_ref, w_ref, b_ref, o_ref):
    c = c_ref[...]
    s = (c * _sigmoid(c)).astype(BF16)
    o_ref[0] = _dot(s, w_ref[0].astype(BF16)) + b_ref[0]


def _modulation(cvec, w_mod, b_mod):
    rows = cvec.shape[0]
    nblk = 1536
    return pl.pallas_call(
        _mod_kernel,
        grid=(DEPTH, 6 * D_MODEL // nblk),
        in_specs=[
            pl.BlockSpec((rows, D_MODEL), lambda l, j: (0, 0)),
            pl.BlockSpec((1, D_MODEL, nblk), lambda l, j: (l, 0, j)),
            pl.BlockSpec((1, 1, nblk), lambda l, j: (l, 0, j)),
        ],
        out_specs=pl.BlockSpec((1, rows, nblk), lambda l, j: (l, 0, j)),
        out_shape=jax.ShapeDtypeStruct((DEPTH, rows, 6 * D_MODEL), F32),
        compiler_params=_cparams(("arbitrary", "arbitrary")),
        name="modulation",
    )(cvec, w_mod, b_mod.reshape(DEPTH, 1, 6 * D_MODEL))


def _assemble_kernel(c_ref, x_ref, o_ref, *, tiles_per_seq):
    r = pl.program_id(0) % tiles_per_seq

    @pl.when(r == 0)
    def _():
        o_ref[...] = c_ref[0]

    @pl.when(r > 0)
    def _():
        o_ref[...] = x_ref[0]


def _assemble(ctx, x, tiles_per_seq):
    n_batch = x.shape[0]
    nt = n_batch * tiles_per_seq
    return pl.pallas_call(
        functools.partial(_assemble_kernel, tiles_per_seq=tiles_per_seq),
        grid=(nt,),
        in_specs=[
            pl.BlockSpec((1, TM, D_MODEL), lambda i: (i // tiles_per_seq, 0, 0)),
            pl.BlockSpec((1, TM, D_MODEL), lambda i: (i // tiles_per_seq, jnp.maximum(i % tiles_per_seq - 1, 0), 0)),
        ],
        out_specs=pl.BlockSpec((TM, D_MODEL), lambda i: (i, 0)),
        out_shape=jax.ShapeDtypeStruct((nt * TM, D_MODEL), F32),
        compiler_params=_cparams(("arbitrary",)),
        name="assemble",
    )(ctx, x)


def _rot_half(x, width):
    lane = lax.broadcasted_iota(I32, x.shape, 1)
    first = (lane % HEAD_DIM) < (HEAD_DIM // 2)
    return jnp.where(first, -pltpu.roll(x, width - HEAD_DIM // 2, axis=1), pltpu.roll(x, HEAD_DIM // 2, axis=1))


def _inproj_kernel(x_ref, mod_ref, g_ref, w_ref, cos_ref, sin_ref, qg_ref, kg_ref, hs_ref,
                   u_ref, cz_ref, q_ref, kv_ref, gt_ref):
    m = mod_ref[0]
    h = _rms(x_ref[...], g_ref[...])
    h = (h * (1.0 + m[:, D_MODEL:2 * D_MODEL]) + m[:, 0:D_MODEL]).astype(BF16)
    o_gate = D_SSM + 3 * D_CONV + D_Q + 2 * D_KV
    y = _dot(h, w_ref[:, 0:o_gate])
    u_ref[...] = y[:, 0:D_SSM].astype(BF16)
    cb = y[:, D_SSM:D_SSM + D_CONV]
    cc = y[:, D_SSM + D_CONV:D_SSM + 2 * D_CONV]
    cx = y[:, D_SSM + 2 * D_CONV:D_SSM + 3 * D_CONV]
    cz_ref[...] = jnp.concatenate([cb, cc * cx], axis=-1).astype(BF16)
    o_q = D_SSM + 3 * D_CONV
    q = y[:, o_q:o_q + D_Q]
    k = y[:, o_q + D_Q:o_q + D_Q + D_KV]
    v = y[:, o_q + D_Q + D_KV:o_gate]
    cos = cos_ref[...]
    sin = sin_ref[...]
    q_ms = _dot((q * q).astype(BF16), hs_ref[...])
    qn = q * lax.rsqrt(q_ms + EPS) * qg_ref[...]
    cos_q = jnp.concatenate([cos] * (D_Q // LANES), axis=-1)
    sin_q = jnp.concatenate([sin] * (D_Q // LANES), axis=-1)
    qr = qn * cos_q + _rot_half(qn, D_Q) * sin_q
    q_ref[...] = (qr * (HEAD_DIM ** -0.5)).astype(BF16)
    k_ms = _dot((k * k).astype(BF16), hs_ref[0:D_KV, 0:D_KV])
    kn = k * lax.rsqrt(k_ms + EPS) * kg_ref[...]
    kr = kn * cos + _rot_half(kn, D_KV) * sin
    kv_ref[...] = jnp.concatenate([kr, v], axis=-1).astype(BF16)
    gt = _dot(h, w_ref[:, o_gate:D_IN])
    gt_ref[...] = _sigmoid(gt).astype(BF16)


def _in_proj(x, mod_tiles, norm_g, w_in_bf, cos_t, sin_t, qg, kg, head_sum, n_batch, tiles_per_seq):
    tok = x.shape[0]
    nt = tok // TM
    t_len = tiles_per_seq * TM
    tile = lambda i: (i, 0)
    const = lambda i: (0, 0)
    seq_tile = lambda i: (i % tiles_per_seq, 0)
    return pl.pallas_call(
        _inproj_kernel,
        grid=(nt,),
        in_specs=[
            pl.BlockSpec((TM, D_MODEL), tile),
            pl.BlockSpec((1, 1, 6 * D_MODEL), lambda i: (i, 0, 0)),
            pl.BlockSpec((1, D_MODEL), const),
            pl.BlockSpec((D_MODEL, D_IN), const),
            pl.BlockSpec((TM, LANES), seq_tile),
            pl.BlockSpec((TM, LANES), seq_tile),
            pl.BlockSpec((1, D_Q), const),
            pl.BlockSpec((1, D_KV), const),
            pl.BlockSpec((D_Q, D_Q), const),
        ],
        out_specs=[
            pl.BlockSpec((TM, D_SSM), lambda i: (i % tiles_per_seq, i // tiles_per_seq)),
            pl.BlockSpec((TM, 2 * D_CONV), tile),
            pl.BlockSpec((TM, D_Q), tile),
            pl.BlockSpec((TM, 2 * D_KV), tile),
            pl.BlockSpec((TM, N_BRANCH * D_MODEL), tile),
        ],
        out_shape=[
            jax.ShapeDtypeStruct((t_len, n_batch * D_SSM), BF16),
            jax.ShapeDtypeStruct((tok, 2 * D_CONV), BF16),
            jax.ShapeDtypeStruct((tok, D_Q), BF16),
            jax.ShapeDtypeStruct((tok, 2 * D_KV), BF16),
            jax.ShapeDtypeStruct((tok, N_BRANCH * D_MODEL), BF16),
        ],
        compiler_params=_cparams(("parallel",)),
        name="in_proj",
    )(x, mod_tiles, norm_g, w_in_bf, cos_t, sin_t, qg, kg, head_sum)


def _s5_kernel(uf_ref, ub_ref, bd_ref, cd_ref, lam_ref, yf_ref, yb_ref, sf_ref, sb_ref, rows_ref, carry_ref,
               *, n_batch):
    i = pl.program_id(0)

    @pl.when(i == 0)
    def _():
        carry_ref[...] = jnp.zeros_like(carry_ref)

    def batch_rows(b):
        return pl.ds(b, S5_CHUNK, stride=n_batch)

    for z, (u_ref, s_ref, y_ref) in enumerate(((uf_ref, sf_ref, yf_ref), (ub_ref, sb_ref, yb_ref))):
        for b in range(n_batch):
            for h in range(D_SSM // LANES):
                lanes = pl.ds(b * D_SSM + h * LANES, LANES)
                rows_ref[h, batch_rows(b), :] = u_ref[:, lanes].astype(F32)
        u_rows = jnp.concatenate([rows_ref[h] for h in range(D_SSM // LANES)], axis=-1)
        s_ref[...] = _dot(u_rows.astype(BF16), bd_ref[z])
        for j in range(N_STATE // S5_COLS):
            re_cols = pl.ds(j * S5_COLS, S5_COLS)
            im_cols = pl.ds(N_STATE + j * S5_COLS, S5_COLS)
            lr = jnp.broadcast_to(lam_ref[2 * z:2 * z + 1, re_cols], (n_batch, S5_COLS))
            li = jnp.broadcast_to(lam_ref[2 * z + 1:2 * z + 2, re_cols], (n_batch, S5_COLS))

            def step(s, c, s_ref=s_ref, z=z, re_cols=re_cols, im_cols=im_cols, lr=lr, li=li):
                t = s if z == 0 else S5_CHUNK - 1 - s
                rows = pl.ds(pl.multiple_of(t * n_batch, n_batch), n_batch)
                xr, xi = c
                nr = lr * xr - li * xi + s_ref[rows, re_cols]
                ni = lr * xi + li * xr + s_ref[rows, im_cols]
                s_ref[rows, re_cols] = nr
                s_ref[rows, im_cols] = ni
                return nr, ni

            c0 = (carry_ref[2 * z, :, re_cols], carry_ref[2 * z + 1, :, re_cols])
            fr, fi = lax.fori_loop(0, S5_CHUNK, step, c0, unroll=True)
            carry_ref[2 * z, :, re_cols] = fr
            carry_ref[2 * z + 1, :, re_cols] = fi
        y_rows = _dot(s_ref[...].astype(BF16), cd_ref[z])
        for h in range(D_SSM // LANES):
            rows_ref[h] = y_rows[:, h * LANES:(h + 1) * LANES]
        for b in range(n_batch):
            for h in range(D_SSM // LANES):
                y_ref[:, pl.ds(b * D_SSM + h * LANES, LANES)] = rows_ref[h, batch_rows(b), :]


def _s5(u_tb, bd, cd, lam, n_batch, t_len):
    rows = S5_CHUNK * n_batch
    width = n_batch * D_SSM
    n_chunks = t_len // S5_CHUNK
    ctx_chunks = N_CTX // S5_CHUNK

    def bwd_block(i):
        return (jnp.where(i < ctx_chunks, ctx_chunks - 1 - i, n_chunks - 1 + ctx_chunks - i), 0)

    return pl.pallas_call(
        functools.partial(_s5_kernel, n_batch=n_batch),
        grid=(n_chunks,),
        in_specs=[
            pl.BlockSpec((S5_CHUNK, width), lambda i: (i, 0)),
            pl.BlockSpec((S5_CHUNK, width), bwd_block),
            pl.BlockSpec((2, D_SSM, 2 * N_STATE), lambda i: (0, 0, 0)),
            pl.BlockSpec((2, 2 * N_STATE, D_SSM), lambda i: (0, 0, 0)),
            pl.BlockSpec((4, N_STATE), lambda i: (0, 0)),
        ],
        out_specs=[
            pl.BlockSpec((S5_CHUNK, width), lambda i: (i, 0)),
            pl.BlockSpec((S5_CHUNK, width), bwd_block),
        ],
        out_shape=[jax.ShapeDtypeStruct((t_len, width), F32)] * 2,
        scratch_shapes=[
            pltpu.VMEM((rows, 2 * N_STATE), F32),
            pltpu.VMEM((rows, 2 * N_STATE), F32),
            pltpu.VMEM((D_SSM // LANES, rows, LANES), F32),
            pltpu.VMEM((4, n_batch, N_STATE), F32),
        ],
        compiler_params=_cparams(("arbitrary",)),
        name="s5_scan",
    )(u_tb, u_tb, bd, cd, lam)


def _attn_kernel(sink_ref, q_ref, kvc_ref, kv0_ref, kv1_ref, kv2_ref, wb_ref, o_ref, *, n_lat):
    n = pl.program_id(1) - N_CTX // ATTN_BLOCK
    rep = N_Q_HEADS // N_KV_HEADS
    rows = rep * ATTN_BLOCK
    q = q_ref[0]
    kvc = kvc_ref[0]
    band = jnp.concatenate([kv0_ref[0], kv1_ref[0], kv2_ref[0]], axis=0)
    key_block = n - 1 + lax.broadcasted_iota(I32, (1, 3 * ATTN_BLOCK), 1) // ATTN_BLOCK
    block_ok = (key_block >= 0) & (key_block < n_lat // ATTN_BLOCK) & (n >= 0)
    bias = wb_ref[...] + jnp.where(block_ok, 0.0, NEG)
    head_of_row = lax.broadcasted_iota(I32, (rows, 1), 0) // ATTN_BLOCK
    contract_last = (((1,), (1,)), ((), ()))
    outs = []
    for g in range(N_KV_HEADS):
        qg = jnp.concatenate([q[:, (g * rep + r) * HEAD_DIM:(g * rep + r + 1) * HEAD_DIM] for r in range(rep)], axis=0)
        sink = jnp.zeros((rows, 1), F32)
        for r in range(rep):
            sink = jnp.where(head_of_row == r, sink_ref[g * rep + r], sink)
        kc = kvc[:, g * HEAD_DIM:(g + 1) * HEAD_DIM]
        vc = kvc[:, D_KV + g * HEAD_DIM:D_KV + (g + 1) * HEAD_DIM]
        kb = band[:, g * HEAD_DIM:(g + 1) * HEAD_DIM]
        vb = band[:, D_KV + g * HEAD_DIM:D_KV + (g + 1) * HEAD_DIM]
        sc = lax.dot_general(qg, kc, contract_last, preferred_element_type=F32)
        sb = lax.dot_general(qg, kb, contract_last, preferred_element_type=F32)
        sb = sb + bias
        mx = jnp.maximum(jnp.maximum(jnp.max(sc, axis=-1, keepdims=True), jnp.max(sb, axis=-1, keepdims=True)), sink)
        pc = jnp.exp(sc - mx)
        pb = jnp.exp(sb - mx)
        den = jnp.sum(pc, axis=-1, keepdims=True) + jnp.sum(pb, axis=-1, keepdims=True) + jnp.exp(sink - mx)
        o = (_dot(pc.astype(BF16), vc) + _dot(pb.astype(BF16), vb)) / den
        outs.extend(o[r * ATTN_BLOCK:(r + 1) * ATTN_BLOCK] for r in range(rep))
    o_ref[0] = jnp.concatenate(outs, axis=-1).astype(BF16)


def _attention(q, kv, sinks, n_batch, t_len):
    nqb = t_len // ATTN_BLOCK
    first = N_CTX // ATTN_BLOCK
    q3 = q.reshape(n_batch, t_len, D_Q)
    kv3 = kv.reshape(n_batch, t_len, 2 * D_KV)

    def band(off):
        return lambda b, j, s: (b, jnp.clip(j + off, first, nqb - 1), 0)

    rep = N_Q_HEADS // N_KV_HEADS
    iq = np.arange(rep * ATTN_BLOCK)[:, None] % ATTN_BLOCK
    ik = np.arange(3 * ATTN_BLOCK)[None, :]
    window_bias = jnp.asarray(np.where(np.abs(iq + ATTN_BLOCK - ik) <= WINDOW, 0.0, NEG), F32)

    out = pl.pallas_call(
        functools.partial(_attn_kernel, n_lat=t_len - N_CTX),
        grid_spec=pltpu.PrefetchScalarGridSpec(
            num_scalar_prefetch=1,
            grid=(n_batch, nqb),
            in_specs=[
                pl.BlockSpec((1, ATTN_BLOCK, D_Q), lambda b, j, s: (b, j, 0)),
                pl.BlockSpec((1, N_CTX, 2 * D_KV), lambda b, j, s: (b, 0, 0)),
                pl.BlockSpec((1, ATTN_BLOCK, 2 * D_KV), band(-1)),
                pl.BlockSpec((1, ATTN_BLOCK, 2 * D_KV), band(0)),
                pl.BlockSpec((1, ATTN_BLOCK, 2 * D_KV), band(1)),
                pl.BlockSpec((rep * ATTN_BLOCK, 3 * ATTN_BLOCK), lambda b, j, s: (0, 0)),
            ],
            out_specs=pl.BlockSpec((1, ATTN_BLOCK, D_Q), lambda b, j, s: (b, j, 0)),
        ),
        out_shape=jax.ShapeDtypeStruct((n_batch, t_len, D_Q), BF16),
        compiler_params=_cparams(("parallel", "parallel")),
        name="attention",
    )(sinks, q3, kv3, kv3, kv3, kv3, window_bias)
    return out.reshape(n_batch * t_len, D_Q)


def _gelu_tanh(x):
    return 0.5 * x * (1.0 + jnp.tanh(math.sqrt(2.0 / math.pi) * (x + 0.044715 * (x * x * x))))


def _merge_kernel(x_ref, yf_ref, yb_ref, u_ref, cz_ref, czp_ref, czn_ref, ya_ref, gt_ref, mod_ref,
                  d_ref, cw_ref, wglu_ref, wso_ref, wco_ref, wao_ref, wo_ref, n2g_ref,
                  wrh_ref, wrl_ref, br_ref, tri_ref, triu_ref, xo_ref, h2_ref, rt_ref, cnt_ref, lg_ref,
                  *, tiles_per_seq, n_tiles):
    @pl.when(pl.program_id(0) == 0)
    def _():
        lg_ref[...] = jnp.full(lg_ref.shape, NEG, F32)

    rt_ref[...], cnt_ref[...] = _route_tile(lg_ref[...], tri_ref, triu_ref)

    r = jnp.minimum(pl.program_id(0), n_tiles - 1) % tiles_per_seq
    ys = yf_ref[...] + yb_ref[...] + d_ref[...] * u_ref[...].astype(F32)
    z = _gelu_tanh(ys)
    glu = z * _sigmoid(_dot(z.astype(BF16), wglu_ref[...]))
    br_ssm = _dot(glu.astype(BF16), wso_ref[...])

    cz = cz_ref[...].astype(F32)
    cb = cz[:, 0:D_CONV]
    zz = cz[:, D_CONV:2 * D_CONV]
    seg_first = r <= 1
    seg_last = (r == 0) | (r == tiles_per_seq - 1)
    prev_row = jnp.where(seg_first, 0.0, czp_ref[7:8, D_CONV:2 * D_CONV].astype(F32))
    next_row = jnp.where(seg_last, 0.0, czn_ref[0:1, D_CONV:2 * D_CONV].astype(F32))
    row = lax.broadcasted_iota(I32, (TM, D_CONV), 0)
    z_dn = jnp.where(row == 0, prev_row, pltpu.roll(zz, 1, axis=0))
    z_up = jnp.where(row == TM - 1, next_row, pltpu.roll(zz, TM - 1, axis=0))
    y_conv = cb * (cw_ref[0:1, :] * z_dn + cw_ref[1:2, :] * zz + cw_ref[2:3, :] * z_up)
    br_conv = _dot(y_conv.astype(BF16), wco_ref[...])
    br_attn = _dot(ya_ref[...], wao_ref[...])

    merged = (gt_ref[:, 0:D_MODEL].astype(F32) * br_ssm
              + gt_ref[:, D_MODEL:2 * D_MODEL].astype(F32) * br_conv
              + gt_ref[:, 2 * D_MODEL:3 * D_MODEL].astype(F32) * br_attn)
    mix = _dot(merged.astype(BF16), wo_ref[...])
    m = mod_ref[0]
    xn = x_ref[...] + m[:, 2 * D_MODEL:3 * D_MODEL] * mix
    xo_ref[...] = xn
    h2 = _rms(xn, n2g_ref[...]) * (1.0 + m[:, 4 * D_MODEL:5 * D_MODEL]) + m[:, 3 * D_MODEL:4 * D_MODEL]
    hi = h2.astype(BF16)
    h2_ref[...] = hi
    lo = (h2 - hi.astype(F32)).astype(BF16)
    lg_ref[...] = _dot(hi, wrh_ref[...]) + _dot(lo, wrh_ref[...]) + _dot(hi, wrl_ref[...]) + br_ref[...]


def _merge(x, yf, yb, u_tb, cz, ya, gt, mod_tiles, d_skip, conv_w, wglu, wso, wco, wao, wo, n2g,
           wr_hi, wr_lo, b_r, tri, triu, tiles_per_seq):
    tok = x.shape[0]
    nt = tok // TM
    cur = lambda i: jnp.minimum(i, nt - 1)
    tile = lambda i: (cur(i), 0)
    routed = lambda i: (jnp.maximum(i - 1, 0), 0)
    const = lambda i: (0, 0)
    tb = lambda i: (cur(i) % tiles_per_seq, cur(i) // tiles_per_seq)
    rows8 = TM // 8
    return pl.pallas_call(
        functools.partial(_merge_kernel, tiles_per_seq=tiles_per_seq, n_tiles=nt),
        grid=(nt + 1,),
        in_specs=[
            pl.BlockSpec((TM, D_MODEL), tile),
            pl.BlockSpec((TM, D_SSM), tb),
            pl.BlockSpec((TM, D_SSM), tb),
            pl.BlockSpec((TM, D_SSM), tb),
            pl.BlockSpec((TM, 2 * D_CONV), tile),
            pl.BlockSpec((8, 2 * D_CONV), lambda i: (jnp.maximum(cur(i) * rows8 - 1, 0), 0)),
            pl.BlockSpec((8, 2 * D_CONV), lambda i: (jnp.minimum((cur(i) + 1) * rows8, tok // 8 - 1), 0)),
            pl.BlockSpec((TM, D_Q), tile),
            pl.BlockSpec((TM, N_BRANCH * D_MODEL), tile),
            pl.BlockSpec((1, 1, 6 * D_MODEL), lambda i: (cur(i), 0, 0)),
            pl.BlockSpec((1, D_SSM), const),
            pl.BlockSpec((3, D_CONV), const),
            pl.BlockSpec((D_SSM, D_SSM), const),
            pl.BlockSpec((D_SSM, D_MODEL), const),
            pl.BlockSpec((D_CONV, D_MODEL), const),
            pl.BlockSpec((D_Q, D_MODEL), const),
            pl.BlockSpec((D_MODEL, D_MODEL), const),
            pl.BlockSpec((1, D_MODEL), const),
            pl.BlockSpec((D_MODEL, LANES), const),
            pl.BlockSpec((D_MODEL, LANES), const),
            pl.BlockSpec((1, LANES), const),
            pl.BlockSpec((TM, TM), const),
            pl.BlockSpec((LANES, LANES), const),
        ],
        out_specs=[
            pl.BlockSpec((TM, D_MODEL), tile),
            pl.BlockSpec((TM, D_MODEL), tile),
            pl.BlockSpec((TM, LANES), routed),
            pl.BlockSpec((8, LANES), routed),
        ],
        out_shape=[
            jax.ShapeDtypeStruct((tok, D_MODEL), F32),
            jax.ShapeDtypeStruct((tok, D_MODEL), BF16),
            jax.ShapeDtypeStruct((tok, LANES), F32),
            jax.ShapeDtypeStruct((nt * 8, LANES), F32),
        ],
        scratch_shapes=[pltpu.VMEM((TM, LANES), F32)],
        compiler_params=_cparams(("arbitrary",)),
        name="branch_merge",
    )(x, yf, yb, u_tb, cz, cz, cz, ya, gt, mod_tiles, d_skip, conv_w, wglu, wso, wco, wao, wo, n2g,
      wr_hi, wr_lo, b_r, tri, triu)


def _route_tile(l, tri_ref, triu_ref):
    lane = lax.broadcasted_iota(I32, l.shape, 1)
    vals, idxs, hots = [], [], []
    for _ in range(TOP_K):
        mx = jnp.max(l, axis=-1, keepdims=True)
        idx = jnp.min(jnp.where(l == mx, lane, LANES), axis=-1, keepdims=True)
        hot = lane == idx
        l = jnp.where(hot, -3e38, l)
        vals.append(mx)
        idxs.append(idx)
        hots.append(hot)
    ex = [jnp.exp(v - vals[0]) for v in vals]
    den = ex[0] + ex[1] + ex[2] + ex[3]
    picked = jnp.zeros(l.shape, F32)
    for hot in hots:
        picked = picked + hot.astype(F32)
    cum = _dot(tri_ref[...], picked.astype(BF16))
    cnt = jnp.sum(picked, axis=0, keepdims=True)
    seg = jnp.floor((cnt + (SEG - 1.0)) * (1.0 / SEG)) * SEG
    run_start = _dot(jnp.broadcast_to(seg, (8, LANES)).astype(BF16), triu_ref[...])[0:1]
    pos = cum + run_start
    out = jnp.zeros(l.shape, F32)
    for k in range(TOP_K):
        row = jnp.sum(jnp.where(hots[k], pos, 0.0), axis=-1, keepdims=True)
        out = jnp.where(lane == k, idxs[k].astype(F32), out)
        out = jnp.where(lane == TOP_K + k, row, out)
        out = jnp.where(lane == 2 * TOP_K + k, ex[k] / den, out)
    return out, jnp.broadcast_to(cnt, (8, LANES))


MAX_PIECES = LOCAL_ROWS // SEG


def _for_each_piece(tile, grow_ref, tot_ref, fn):
    base = tile * MAX_PIECES

    def per_piece(p, c):
        fn(pl.multiple_of(p * SEG, SEG), pl.multiple_of(grow_ref[base + p], SEG))
        return c

    lax.fori_loop(0, tot_ref[tile], per_piece, 0)


def _local_rows(rt):
    rows = lax.broadcasted_iota(I32, (TM, LOCAL_ROWS), 1)
    return [rows == rt[:, TOP_K + k:TOP_K + k + 1].astype(I32) for k in range(TOP_K)]


def _dispatch_kernel(grow_ref, tot_ref, zstart_ref, nu_ref, h_ref, rt_ref, xs_hbm,
                     xl_ref, zero_ref, sem_ref, sem_z, *, n_blocks, n_tiles):
    i = pl.program_id(0)
    slot = i % 2

    @pl.when(i == 0)
    def _():
        zero_ref[...] = jnp.zeros_like(zero_ref)

        def zero_copy(start):
            rows = pl.ds(pl.multiple_of(start, MOE_BLOCK), MOE_BLOCK)
            return pltpu.make_async_copy(zero_ref, xs_hbm.at[rows, :], sem_z)

        def tail_start(b, c):
            zero_copy(b * MOE_BLOCK).start()
            return c

        def tail_wait(b, c):
            zero_copy(b * MOE_BLOCK).wait()
            return c

        for e in range(N_EXPERTS):
            @pl.when(zstart_ref[e] >= 0)
            def _():
                zero_copy(zstart_ref[e]).start()
        lax.fori_loop(nu_ref[0], n_blocks, tail_start, 0)
        for e in range(N_EXPERTS):
            @pl.when(zstart_ref[e] >= 0)
            def _():
                zero_copy(zstart_ref[e]).wait()
        lax.fori_loop(nu_ref[0], n_blocks, tail_wait, 0)

    place = jnp.zeros((TM, LOCAL_ROWS), F32)
    for hit in _local_rows(rt_ref[...]):
        place = place + hit.astype(F32)
    xl_ref[slot] = lax.dot_general(place.astype(BF16), h_ref[...], (((0,), (0,)), ((), ())),
                                   preferred_element_type=F32)

    def seg_copy(local_row, global_row, s):
        return pltpu.make_async_copy(xl_ref.at[s, pl.ds(local_row, SEG), :],
                                     xs_hbm.at[pl.ds(global_row, SEG), :], sem_ref.at[s])

    def wait_tile(tile, s):
        def one(g, c):
            seg_copy(0, 0, s).wait()
            return c

        lax.fori_loop(0, tot_ref[tile], one, 0)

    _for_each_piece(i, grow_ref, tot_ref, lambda lr, gr: seg_copy(lr, gr, slot).start())

    @pl.when(i > 0)
    def _():
        wait_tile(i - 1, 1 - slot)

    @pl.when(i == n_tiles - 1)
    def _():
        wait_tile(i, slot)


def _dispatch(h2, route, seg_meta, zstart, n_used, n_slots):
    tok = h2.shape[0]
    nt = tok // TM
    tile = lambda i, *_: (i, 0)
    return pl.pallas_call(
        functools.partial(_dispatch_kernel, n_blocks=n_slots // MOE_BLOCK, n_tiles=nt),
        grid_spec=pltpu.PrefetchScalarGridSpec(
            num_scalar_prefetch=4,
            grid=(nt,),
            in_specs=[
                pl.BlockSpec((TM, D_MODEL), tile),
                pl.BlockSpec((TM, LANES), tile),
            ],
            out_specs=pl.BlockSpec(memory_space=pl.ANY),
            scratch_shapes=[
                pltpu.VMEM((2, LOCAL_ROWS, D_MODEL), F32),
                pltpu.VMEM((MOE_BLOCK, D_MODEL), F32),
                pltpu.SemaphoreType.DMA((2,)),
                pltpu.SemaphoreType.DMA,
            ],
        ),
        out_shape=jax.ShapeDtypeStruct((n_slots, D_MODEL), F32),
        compiler_params=_cparams(("arbitrary",)),
        name="dispatch",
    )(*seg_meta, zstart, n_used, h2, route)


DEINT = 256


def _expert_kernel(be_ref, nu_ref, x_ref, wgu_ref, bg_ref, bu_ref, wd_ref, bd_ref, sel_ref, y_ref,
                   wg_s, wu_s, wd_s):
    i = pl.program_id(0)
    prev = be_ref[jnp.maximum(i - 1, 0)]

    @pl.when((i == 0) | (be_ref[i] != prev))
    def _():
        for c in range(2 * D_EXPERT // DEINT):
            w = wgu_ref[0, 0, :, c * DEINT:(c + 1) * DEINT].astype(BF16)
            cols = pl.ds(c * (DEINT // 2), DEINT // 2)
            split = _dot(w, sel_ref[...])
            wg_s[:, cols] = split[:, 0:DEINT // 2].astype(BF16)
            wu_s[:, cols] = split[:, DEINT // 2:DEINT].astype(BF16)
        wd_s[...] = wd_ref[0, 0].astype(BF16)

    @pl.when(i < nu_ref[0])
    def _():
        x = x_ref[...].astype(BF16)
        g = _dot(x, wg_s[...]) + bg_ref[0]
        u = _dot(x, wu_s[...]) + bu_ref[0]
        glu = jnp.minimum(g, SWIGLU_LIMIT)
        up = jnp.clip(u, -SWIGLU_LIMIT, SWIGLU_LIMIT)
        act = glu * _sigmoid(SWIGLU_ALPHA * glu) * (up + 1.0)
        y_ref[...] = _dot(act.astype(BF16), wd_s[...]) + bd_ref[0]

    @pl.when(i >= nu_ref[0])
    def _():
        y_ref[...] = jnp.zeros_like(y_ref)


def _experts(xs, block_expert, n_used, layer, wgu_all, bg, bu, wd_all, bd, sel):
    n_slots = xs.shape[0]
    n_blocks = n_slots // MOE_BLOCK
    wmap = lambda i, be, nu: (be[i], 0, 0)
    lwmap = lambda i, be, nu: (layer, be[i], 0, 0)
    const = lambda i, be, nu: (0, 0)
    return pl.pallas_call(
        _expert_kernel,
        grid_spec=pltpu.PrefetchScalarGridSpec(
            num_scalar_prefetch=2,
            grid=(n_blocks,),
            in_specs=[
                pl.BlockSpec((MOE_BLOCK, D_MODEL), lambda i, be, nu: (jnp.minimum(i, nu[0] - 1), 0)),
                pl.BlockSpec((1, 1, D_MODEL, 2 * D_EXPERT), lwmap),
                pl.BlockSpec((1, 1, D_EXPERT), wmap),
                pl.BlockSpec((1, 1, D_EXPERT), wmap),
                pl.BlockSpec((1, 1, D_EXPERT, D_MODEL), lwmap),
                pl.BlockSpec((1, 1, D_MODEL), wmap),
                pl.BlockSpec((DEINT, DEINT), const),
            ],
            out_specs=pl.BlockSpec((MOE_BLOCK, D_MODEL), lambda i, be, nu: (i, 0)),
            scratch_shapes=[
                pltpu.VMEM((D_MODEL, D_EXPERT), BF16),
                pltpu.VMEM((D_MODEL, D_EXPERT), BF16),
                pltpu.VMEM((D_EXPERT, D_MODEL), BF16),
            ],
        ),
        out_shape=jax.ShapeDtypeStruct((n_slots, D_MODEL), F32),
        compiler_params=_cparams(("arbitrary",)),
        name="experts",
    )(block_expert, n_used, xs, wgu_all, bg, bu, wd_all, bd, sel)


def _combine_kernel(grow_ref, tot_ref, ys_hbm, x_ref, rt_ref, mod_ref, o_ref, yl_ref, sem_ref,
                    *, n_tiles):
    i = pl.program_id(0)
    slot = i % 2

    def seg_copy(local_row, global_row, s):
        return pltpu.make_async_copy(ys_hbm.at[pl.ds(global_row, SEG), :],
                                     yl_ref.at[s, pl.ds(local_row, SEG), :], sem_ref.at[s])

    def fetch_tile(tile, s):
        _for_each_piece(tile, grow_ref, tot_ref, lambda lr, gr: seg_copy(lr, gr, s).start())

    @pl.when(i == 0)
    def _():
        yl_ref[...] = jnp.zeros_like(yl_ref)
        fetch_tile(0, 0)

    @pl.when(i + 1 < n_tiles)
    def _():
        fetch_tile(i + 1, 1 - slot)

    def one(g, c):
        seg_copy(0, 0, slot).wait()
        return c

    lax.fori_loop(0, tot_ref[i], one, 0)

    rt = rt_ref[...]
    weights = jnp.zeros((TM, LOCAL_ROWS), F32)
    for k, hit in enumerate(_local_rows(rt)):
        weights = weights + jnp.where(hit, rt[:, 2 * TOP_K + k:2 * TOP_K + k + 1], 0.0)
    w_hi = weights.astype(BF16)
    w_lo = (weights - w_hi.astype(F32)).astype(BF16)
    yl = yl_ref[slot].astype(BF16)
    y = _dot(w_hi, yl) + _dot(w_lo, yl)
    o_ref[...] = (x_ref[...] + mod_ref[0][:, 5 * D_MODEL:6 * D_MODEL] * y).reshape(o_ref.shape)


def _combine(x, ys, route, seg_meta, mod_tiles, out_seq=None):
    tok = x.shape[0]
    nt = tok // TM
    tile = lambda i, *_: (i, 0)
    if out_seq is None:
        out_spec = pl.BlockSpec((TM, D_MODEL), tile)
        out_shape = jax.ShapeDtypeStruct((tok, D_MODEL), F32)
    else:
        n_batch, tps = out_seq
        out_spec = pl.BlockSpec((1, TM, D_MODEL), lambda i, *_: (i // tps, jnp.maximum(i % tps - 1, 0), 0))
        out_shape = jax.ShapeDtypeStruct((n_batch, (tps - 1) * TM, D_MODEL), F32)
    return pl.pallas_call(
        functools.partial(_combine_kernel, n_tiles=nt),
        grid_spec=pltpu.PrefetchScalarGridSpec(
            num_scalar_prefetch=2,
            grid=(nt,),
            in_specs=[
                pl.BlockSpec(memory_space=pl.ANY),
                pl.BlockSpec((TM, D_MODEL), tile),
                pl.BlockSpec((TM, LANES), tile),
                pl.BlockSpec((1, 1, 6 * D_MODEL), lambda i, *_: (i, 0, 0)),
            ],
            out_specs=out_spec,
            scratch_shapes=[
                pltpu.VMEM((2, LOCAL_ROWS, D_MODEL), F32),
                pltpu.SemaphoreType.DMA((2,)),
            ],
        ),
        out_shape=out_shape,
        compiler_params=_cparams(("arbitrary",)),
        name="combine",
    )(*seg_meta, ys, x, route, mod_tiles)


def _rope_tables(t_len):
    n_lat = t_len - N_CTX
    t = np.arange(n_lat)
    n_pairs = HEAD_DIM // 4
    inv_freq = jnp.asarray(ROPE_BASE, F32) ** (-jnp.arange(n_pairs, dtype=F32) / n_pairs)
    row = jnp.asarray(t // GRID_W, F32)
    col = jnp.asarray(t % GRID_W, F32)
    ang = jnp.concatenate([row[:, None] * inv_freq, col[:, None] * inv_freq], axis=-1)
    ang = jnp.concatenate([jnp.zeros((N_CTX, HEAD_DIM // 2), F32), ang], axis=0)
    cos = jnp.tile(jnp.cos(ang), (1, LANES // (HEAD_DIM // 2)))
    sin = jnp.tile(jnp.sin(ang), (1, LANES // (HEAD_DIM // 2)))
    return cos, sin


def _s5_operands(lam_re, lam_im, log_dt, b_re, b_im, c_re, c_im):
    lam = lax.complex(lam_re, lam_im)
    dt = jnp.exp(log_dt)[..., None]
    lam_bar = jnp.exp(lam * dt)
    b_bar = ((lam_bar - 1.0) / lam)[..., None] * lax.complex(b_re, b_im)
    eye = jnp.eye(N_SSM_GROUPS, dtype=F32)

    def in_block(w):
        return jnp.einsum("zgph,gk->zghkp", w, eye).reshape(2, D_SSM, N_STATE)

    def out_block(w):
        return jnp.einsum("zghp,gk->zgpkh", w, eye).reshape(2, N_STATE, D_SSM)

    bd = jnp.concatenate([in_block(b_bar.real), in_block(b_bar.imag)], axis=-1).astype(BF16)
    cd = jnp.concatenate([out_block(c_re), out_block(-c_im)], axis=1).astype(BF16)
    lam_rows = jnp.stack([lam_bar[0].real.reshape(-1), lam_bar[0].imag.reshape(-1),
                          lam_bar[1].real.reshape(-1), lam_bar[1].imag.reshape(-1)])
    return bd, cd, lam_rows


def _moe_constants():
    tri = jnp.asarray(np.tril(np.ones((TM, TM)), -1), BF16)
    triu = jnp.asarray(np.triu(np.ones((LANES, LANES)), 1), BF16)
    pick = np.arange(DEINT)[:, None] == 2 * np.arange(DEINT // 2)[None, :]
    sel = jnp.asarray(np.concatenate([pick, np.roll(pick, 1, axis=0)], axis=1), BF16)
    return tri, triu, sel


def _moe(xs, h2, route, counts, mod_tiles, layer, w_gate_up, b_gate_up, w_down, b_down, consts, out_seq=None):
    sel = consts[2]
    tok = xs.shape[0]
    nt = tok // TM
    n_blocks = (tok * TOP_K + nt * N_EXPERTS * (SEG - 1) + N_EXPERTS * (MOE_BLOCK - 1)) // MOE_BLOCK
    n_slots = n_blocks * MOE_BLOCK

    counts = counts.reshape(nt, 8, LANES)[:, 0, :N_EXPERTS].astype(I32)
    seg = (counts + SEG - 1) // SEG * SEG
    region = jnp.sum(seg, axis=0)
    padded = (region + MOE_BLOCK - 1) // MOE_BLOCK * MOE_BLOCK
    pad_end = jnp.cumsum(padded)
    pad_start = pad_end - padded
    run_global = pad_start[None, :] + jnp.cumsum(seg, axis=0) - seg
    run_local = jnp.cumsum(seg, axis=1) - seg
    pieces = seg // SEG
    ends = jnp.cumsum(pieces, axis=1)
    shift = run_global - run_local
    jump = shift - jnp.concatenate([jnp.zeros((nt, 1), I32), shift[:, :-1]], axis=1)
    piece = jnp.arange(MAX_PIECES, dtype=I32)
    started = piece[None, None, :] >= (ends - pieces)[:, :, None]
    piece_global = piece[None, :] * SEG + jnp.sum(jnp.where(started, jump[:, :, None], 0), axis=1)
    seg_meta = (piece_global.reshape(-1).astype(I32), ends[:, -1].astype(I32))
    n_used = (pad_end[-1] // MOE_BLOCK).astype(I32)
    blk = jnp.minimum(jnp.arange(n_blocks, dtype=I32), n_used - 1) * MOE_BLOCK
    block_expert = jnp.minimum(jnp.sum(blk[:, None] >= pad_end[None, :], axis=1), N_EXPERTS - 1).astype(I32)
    zstart = jnp.where(region > 0, pad_end - MOE_BLOCK, -1).astype(I32)

    x_sorted = _dispatch(h2, route, seg_meta, zstart, n_used.reshape(1), n_slots)
    y_sorted = _experts(x_sorted, block_expert, n_used.reshape(1), layer, w_gate_up,
                        b_gate_up[:, None, 0::2], b_gate_up[:, None, 1::2],
                        w_down, b_down[:, None, :], sel)
    return _combine(xs, y_sorted, route, seg_meta, mod_tiles, out_seq)


def kernel(x, c, ctx, c_ctx, w_mod, b_mod, norm1_g, norm2_g, w_in, ssm_lam_re, ssm_lam_im, ssm_log_dt, ssm_b_re, ssm_b_im, ssm_c_re, ssm_c_im, ssm_d, w_glu, w_ssm_out, conv_w, w_conv_out, q_norm_g, k_norm_g, attn_sinks, w_attn_out, w_o, w_router, b_router, w_gate_up, b_gate_up, w_down, b_down):
    n_batch, n_lat, _ = x.shape
    t_len = N_CTX + n_lat
    tiles_per_seq = t_len // TM
    tok = n_batch * t_len
    nt = tok // TM

    xs = _assemble(ctx, x, tiles_per_seq)

    mod_rows = 8 * ((n_batch + 1 + 7) // 8)
    cvec = jnp.zeros((mod_rows, D_MODEL), F32).at[:n_batch].set(c).at[n_batch].set(c_ctx)
    mod_all = _modulation(cvec, w_mod, b_mod)
    tile_ids = np.arange(nt)
    tile_row = np.where(tile_ids % tiles_per_seq == 0, n_batch, tile_ids // tiles_per_seq)

    cos_t, sin_t = _rope_tables(t_len)
    head_sum = jnp.asarray(np.kron(np.eye(N_Q_HEADS), np.full((HEAD_DIM, HEAD_DIM), 1.0 / HEAD_DIM)), BF16)
    moe_consts = _moe_constants()

    for l in range(DEPTH):
        mod_tiles = mod_all[l][tile_row].reshape(nt, 1, 6 * D_MODEL)
        qg = jnp.tile(q_norm_g[l], N_Q_HEADS).reshape(1, D_Q)
        kg = jnp.tile(k_norm_g[l], N_KV_HEADS).reshape(1, D_KV)
        u_tb, cz, q, kv, gt = _in_proj(xs, mod_tiles, norm1_g[l].reshape(1, D_MODEL), w_in[l].astype(BF16),
                                       cos_t, sin_t, qg, kg, head_sum, n_batch, tiles_per_seq)

        bd, cd, lam_rows = _s5_operands(ssm_lam_re[l], ssm_lam_im[l], ssm_log_dt[l], ssm_b_re[l], ssm_b_im[l],
                                        ssm_c_re[l], ssm_c_im[l])
        yf, yb = _s5(u_tb, bd, cd, lam_rows, n_batch, t_len)
        ya = _attention(q, kv, attn_sinks[l], n_batch, t_len)

        wr = jnp.zeros((D_MODEL, LANES), F32).at[:, :N_EXPERTS].set(w_router[l])
        wr_hi = wr.astype(BF16)
        wr_lo = (wr - wr_hi.astype(F32)).astype(BF16)
        b_r = jnp.full((1, LANES), NEG, F32).at[0, :N_EXPERTS].set(b_router[l])
        xs, h2, route, counts = _merge(
            xs, yf, yb, u_tb,
            cz, ya, gt, mod_tiles, ssm_d[l].reshape(1, D_SSM), conv_w[l],
            w_glu[l].astype(BF16), w_ssm_out[l].astype(BF16), w_conv_out[l].astype(BF16),
            w_attn_out[l].astype(BF16), w_o[l].astype(BF16), norm2_g[l].reshape(1, D_MODEL),
            wr_hi, wr_lo, b_r, moe_consts[0], moe_consts[1], tiles_per_seq)

        out_seq = None
        if l == DEPTH - 1:
            out_seq = (n_batch, tiles_per_seq)
            ctx_tile = jnp.asarray(tile_ids % tiles_per_seq == 0)
            counts = jnp.where(jnp.repeat(ctx_tile, 8)[:, None], 0.0, counts)
            lane = jnp.arange(LANES)[None, :]
            row_lanes = (lane >= TOP_K) & (lane < 2 * TOP_K)
            route = jnp.where(jnp.repeat(ctx_tile, TM)[:, None] & row_lanes, float(LOCAL_ROWS), route)
        xs = _moe(xs, h2, route, counts, mod_tiles, l, w_gate_up, b_gate_up[l], w_down, b_down[l], moe_consts,
                  out_seq)

    return xs
```

```python
import functools
import math

import jax
import jax.numpy as jnp
import numpy as np
from jax import lax
from jax.experimental import pallas as pl
from jax.experimental.pallas import tpu as pltpu

F32 = jnp.float32
BF16 = jnp.bfloat16
I32 = jnp.int32

D_MODEL = 1024
DEPTH = 4
N_CTX = 256
HEAD_DIM = 64
N_Q_HEADS = 8
N_KV_HEADS = 2
D_Q = N_Q_HEADS * HEAD_DIM
D_KV = N_KV_HEADS * HEAD_DIM
WINDOW = 128
ATTN_BLOCK = 128
ROPE_BASE = 10000.0
GRID_W = 64
D_SSM = 256
SSM_GROUP = 16
N_SSM_GROUPS = 16
SSM_STATE = 64
N_STATE = N_SSM_GROUPS * SSM_STATE
D_CONV = 256
N_BRANCH = 3
D_IN = D_SSM + 3 * D_CONV + D_Q + 2 * D_KV + N_BRANCH * D_MODEL
N_EXPERTS = 32
TOP_K = 4
D_EXPERT = 1024
SWIGLU_LIMIT = 7.0
SWIGLU_ALPHA = 1.702
EPS = 1e-6

LANES = 128
TM = 256
MOE_BLOCK = 512
SEG = 8
LOCAL_ROWS = -(-(TM * TOP_K + N_EXPERTS * (SEG - 1)) // LANES) * LANES
S5_CHUNK = 64
S5_COLS = 512
NEG = -1e30
VMEM_LIMIT = 56 * 1024 * 1024


def _cparams(sem):
    return pltpu.CompilerParams(dimension_semantics=sem, vmem_limit_bytes=VMEM_LIMIT)


def _dot(a, b):
    return jnp.dot(a, b, preferred_element_type=F32)


def _sigmoid(x):
    return 1.0 / (1.0 + jnp.exp(-x))


def _rms(x, g):
    ms = jnp.mean(x * x, axis=-1, keepdims=True)
    return x * lax.rsqrt(ms + EPS) * g


def _mod_kernel(c_ref, w_ref, b_ref, o_ref):
    c = c_ref[...]
    s = (c * _sigmoid(c)).astype(BF16)
    o_ref[0] = _dot(s, w_ref[0].astype(BF16)) + b_ref[0]


def _modulation(cvec, w_mod, b_mod):
    rows = cvec.shape[0]
    nblk = 1536
    return pl.pallas_call(
        _mod_kernel,
        grid=(DEPTH, 6 * D_MODEL // nblk),
        in_specs=[
            pl.BlockSpec((rows, D_MODEL), lambda l, j: (0, 0)),
            pl.BlockSpec((1, D_MODEL, nblk), lambda l, j: (l, 0, j)),
            pl.BlockSpec((1, 1, nblk), lambda l, j: (l, 0, j)),
        ],
        out_specs=pl.BlockSpec((1, rows, nblk), lambda l, j: (l, 0, j)),
        out_shape=jax.ShapeDtypeStruct((DEPTH, rows, 6 * D_MODEL), F32),
        compiler_params=_cparams(("arbitrary", "arbitrary")),
        name="modulation",
    )(cvec, w_mod, b_mod.reshape(DEPTH, 1, 6 * D_MODEL))


def _assemble_kernel(c_ref, x_ref, o_ref, *, tiles_per_seq):
    r = pl.program_id(0) % tiles_per_seq

    @pl.when(r == 0)
    def _():
        o_ref[...] = c_ref[0]

    @pl.when(r > 0)
    def _():
        o_ref[...] = x_ref[0]


def _assemble(ctx, x, tiles_per_seq):
    n_batch = x.shape[0]
    nt = n_batch * tiles_per_seq
    return pl.pallas_call(
        functools.partial(_assemble_kernel, tiles_per_seq=tiles_per_seq),
        grid=(nt,),
        in_specs=[
            pl.BlockSpec((1, TM, D_MODEL), lambda i: (i // tiles_per_seq, 0, 0)),
            pl.BlockSpec((1, TM, D_MODEL), lambda i: (i // tiles_per_seq, jnp.maximum(i % tiles_per_seq - 1, 0), 0)),
        ],
        out_specs=pl.BlockSpec((TM, D_MODEL), lambda i: (i, 0)),
        out_shape=jax.ShapeDtypeStruct((nt * TM, D_MODEL), F32),
        compiler_params=_cparams(("arbitrary",)),
        name="assemble",
    )(ctx, x)


def _rot_half(x, width):
    lane = lax.broadcasted_iota(I32, x.shape, 1)
    first = (lane % HEAD_DIM) < (HEAD_DIM // 2)
    return jnp.where(first, -pltpu.roll(x, width - HEAD_DIM // 2, axis=1), pltpu.roll(x, HEAD_DIM // 2, axis=1))


def _inproj_kernel(x_ref, mod_ref, g_ref, w_ref, cos_ref, sin_ref, qg_ref, kg_ref, hs_ref,
                   u_ref, cz_ref, q_ref, kv_ref, gt_ref):
    m = mod_ref[0]
    h = _rms(x_ref[...], g_ref[...])
    h = (h * (1.0 + m[:, D_MODEL:2 * D_MODEL]) + m[:, 0:D_MODEL]).astype(BF16)
    o_gate = D_SSM + 3 * D_CONV + D_Q + 2 * D_KV
    y = _dot(h, w_ref[:, 0:o_gate])
    u_ref[...] = y[:, 0:D_SSM].astype(BF16)
    cb = y[:, D_SSM:D_SSM + D_CONV]
    cc = y[:, D_SSM + D_CONV:D_SSM + 2 * D_CONV]
    cx = y[:, D_SSM + 2 * D_CONV:D_SSM + 3 * D_CONV]
    cz_ref[...] = jnp.concatenate([cb, cc * cx], axis=-1).astype(BF16)
    o_q = D_SSM + 3 * D_CONV
    q = y[:, o_q:o_q + D_Q]
    k = y[:, o_q + D_Q:o_q + D_Q + D_KV]
    v = y[:, o_q + D_Q + D_KV:o_gate]
    cos = cos_ref[...]
    sin = sin_ref[...]
    q_ms = _dot((q * q).astype(BF16), hs_ref[...])
    qn = q * lax.rsqrt(q_ms + EPS) * qg_ref[...]
    cos_q = jnp.concatenate([cos] * (D_Q // LANES), axis=-1)
    sin_q = jnp.concatenate([sin] * (D_Q // LANES), axis=-1)
    qr = qn * cos_q + _rot_half(qn, D_Q) * sin_q
    q_ref[...] = (qr * (HEAD_DIM ** -0.5)).astype(BF16)
    k_ms = _dot((k * k).astype(BF16), hs_ref[0:D_KV, 0:D_KV])
    kn = k * lax.rsqrt(k_ms + EPS) * kg_ref[...]
    kr = kn * cos + _rot_half(kn, D_KV) * sin
    kv_ref[...] = jnp.concatenate([kr, v], axis=-1).astype(BF16)
    gt = _dot(h, w_ref[:, o_gate:D_IN])
    gt_ref[...] = _sigmoid(gt).astype(BF16)


def _in_proj(x, mod_tiles, norm_g, w_in_bf, cos_t, sin_t, qg, kg, head_sum, n_batch, tiles_per_seq):
    tok = x.shape[0]
    nt = tok // TM
    t_len = tiles_per_seq * TM
    tile = lambda i: (i, 0)
    const = lambda i: (0, 0)
    seq_tile = lambda i: (i % tiles_per_seq, 0)
    return pl.pallas_call(
        _inproj_kernel,
        grid=(nt,),
        in_specs=[
            pl.BlockSpec((TM, D_MODEL), tile),
            pl.BlockSpec((1, 1, 6 * D_MODEL), lambda i: (i, 0, 0)),
            pl.BlockSpec((1, D_MODEL), const),
            pl.BlockSpec((D_MODEL, D_IN), const),
            pl.BlockSpec((TM, LANES), seq_tile),
            pl.BlockSpec((TM, LANES), seq_tile),
            pl.BlockSpec((1, D_Q), const),
            pl.BlockSpec((1, D_KV), const),
            pl.BlockSpec((D_Q, D_Q), const),
        ],
        out_specs=[
            pl.BlockSpec((TM, D_SSM), lambda i: (i % tiles_per_seq, i // tiles_per_seq)),
            pl.BlockSpec((TM, 2 * D_CONV), tile),
            pl.BlockSpec((TM, D_Q), tile),
            pl.BlockSpec((TM, 2 * D_KV), tile),
            pl.BlockSpec((TM, N_BRANCH * D_MODEL), tile),
        ],
        out_shape=[
            jax.ShapeDtypeStruct((t_len, n_batch * D_SSM), BF16),
            jax.ShapeDtypeStruct((tok, 2 * D_CONV), BF16),
            jax.ShapeDtypeStruct((tok, D_Q), BF16),
            jax.ShapeDtypeStruct((tok, 2 * D_KV), BF16),
            jax.ShapeDtypeStruct((tok, N_BRANCH * D_MODEL), BF16),
        ],
        compiler_params=_cparams(("parallel",)),
        name="in_proj",
    )(x, mod_tiles, norm_g, w_in_bf, cos_t, sin_t, qg, kg, head_sum)


def _s5_kernel(uf_ref, ub_ref, bd_ref, cd_ref, lam_ref, yf_ref, yb_ref, sf_ref, sb_ref, in_ref, out_ref, carry_ref,
               *, n_batch):
    i = pl.program_id(0)

    @pl.when(i == 0)
    def _():
        carry_ref[...] = jnp.zeros_like(carry_ref)

    halves = D_SSM // LANES

    def batch_rows(b):
        return pl.ds(b, S5_CHUNK, stride=n_batch)

    def project_in(z, u_ref, s_ref):
        for b in range(n_batch):
            for h in range(halves):
                in_ref[z, h, batch_rows(b), :] = u_ref[:, pl.ds(b * D_SSM + h * LANES, LANES)].astype(F32)
        u_rows = jnp.concatenate([in_ref[z, h] for h in range(halves)], axis=-1)
        s_ref[...] = _dot(u_rows.astype(BF16), bd_ref[z])

    def recur(z, s_ref):
        for j in range(N_STATE // S5_COLS):
            re_cols = pl.ds(j * S5_COLS, S5_COLS)
            im_cols = pl.ds(N_STATE + j * S5_COLS, S5_COLS)
            lr = jnp.broadcast_to(lam_ref[2 * z:2 * z + 1, re_cols], (n_batch, S5_COLS))
            li = jnp.broadcast_to(lam_ref[2 * z + 1:2 * z + 2, re_cols], (n_batch, S5_COLS))
            xr = carry_ref[2 * z, :, re_cols]
            xi = carry_ref[2 * z + 1, :, re_cols]
            for s in range(S5_CHUNK):
                t = s if z == 0 else S5_CHUNK - 1 - s
                rows = pl.ds(t * n_batch, n_batch)
                xr, xi = (lr * xr - li * xi + s_ref[rows, re_cols],
                          lr * xi + li * xr + s_ref[rows, im_cols])
                s_ref[rows, re_cols] = xr
                s_ref[rows, im_cols] = xi
            carry_ref[2 * z, :, re_cols] = xr
            carry_ref[2 * z + 1, :, re_cols] = xi

    def project_out(z, s_ref, y_ref):
        y_rows = _dot(s_ref[...].astype(BF16), cd_ref[z])
        for h in range(halves):
            out_ref[z, h] = y_rows[:, h * LANES:(h + 1) * LANES]
        for b in range(n_batch):
            for h in range(halves):
                y_ref[:, pl.ds(b * D_SSM + h * LANES, LANES)] = out_ref[z, h, batch_rows(b), :]

    project_in(0, uf_ref, sf_ref)
    project_in(1, ub_ref, sb_ref)
    recur(0, sf_ref)
    recur(1, sb_ref)
    project_out(0, sf_ref, yf_ref)
    project_out(1, sb_ref, yb_ref)


def _s5(u_tb, bd, cd, lam, n_batch, t_len):
    rows = S5_CHUNK * n_batch
    width = n_batch * D_SSM
    n_chunks = t_len // S5_CHUNK
    ctx_chunks = N_CTX // S5_CHUNK

    def bwd_block(i):
        return (jnp.where(i < ctx_chunks, ctx_chunks - 1 - i, n_chunks - 1 + ctx_chunks - i), 0)

    return pl.pallas_call(
        functools.partial(_s5_kernel, n_batch=n_batch),
        grid=(n_chunks,),
        in_specs=[
            pl.BlockSpec((S5_CHUNK, width), lambda i: (i, 0)),
            pl.BlockSpec((S5_CHUNK, width), bwd_block),
            pl.BlockSpec((2, D_SSM, 2 * N_STATE), lambda i: (0, 0, 0)),
            pl.BlockSpec((2, 2 * N_STATE, D_SSM), lambda i: (0, 0, 0)),
            pl.BlockSpec((4, N_STATE), lambda i: (0, 0)),
        ],
        out_specs=[
            pl.BlockSpec((S5_CHUNK, width), lambda i: (i, 0)),
            pl.BlockSpec((S5_CHUNK, width), bwd_block),
        ],
        out_shape=[jax.ShapeDtypeStruct((t_len, width), F32)] * 2,
        scratch_shapes=[
            pltpu.VMEM((rows, 2 * N_STATE), F32),
            pltpu.VMEM((rows, 2 * N_STATE), F32),
            pltpu.VMEM((2, D_SSM // LANES, rows, LANES), F32),
            pltpu.VMEM((2, D_SSM // LANES, rows, LANES), F32),
            pltpu.VMEM((4, n_batch, N_STATE), F32),
        ],
        compiler_params=_cparams(("arbitrary",)),
        name="s5_scan",
    )(u_tb, u_tb, bd, cd, lam)


def _attn_kernel(sink_ref, q_ref, kvc_ref, kv0_ref, kv1_ref, kv2_ref, wb_ref, o_ref, *, n_lat):
    n = pl.program_id(1) - N_CTX // ATTN_BLOCK
    rep = N_Q_HEADS // N_KV_HEADS
    rows = rep * ATTN_BLOCK
    q = q_ref[0]
    kvc = kvc_ref[0]
    band = jnp.concatenate([kv0_ref[0], kv1_ref[0], kv2_ref[0]], axis=0)
    key_block = n - 1 + lax.broadcasted_iota(I32, (1, 3 * ATTN_BLOCK), 1) // ATTN_BLOCK
    block_ok = (key_block >= 0) & (key_block < n_lat // ATTN_BLOCK) & (n >= 0)
    bias = wb_ref[...] + jnp.where(block_ok, 0.0, NEG)
    head_of_row = lax.broadcasted_iota(I32, (rows, 1), 0) // ATTN_BLOCK
    contract_last = (((1,), (1,)), ((), ()))
    outs = []
    for g in range(N_KV_HEADS):
        qg = jnp.concatenate([q[:, (g * rep + r) * HEAD_DIM:(g * rep + r + 1) * HEAD_DIM] for r in range(rep)], axis=0)
        sink = jnp.zeros((rows, 1), F32)
        for r in range(rep):
            sink = jnp.where(head_of_row == r, sink_ref[g * rep + r], sink)
        kc = kvc[:, g * HEAD_DIM:(g + 1) * HEAD_DIM]
        vc = kvc[:, D_KV + g * HEAD_DIM:D_KV + (g + 1) * HEAD_DIM]
        kb = band[:, g * HEAD_DIM:(g + 1) * HEAD_DIM]
        vb = band[:, D_KV + g * HEAD_DIM:D_KV + (g + 1) * HEAD_DIM]
        sc = lax.dot_general(qg, kc, contract_last, preferred_element_type=F32)
        sb = lax.dot_general(qg, kb, contract_last, preferred_element_type=F32)
        sb = sb + bias
        mx = jnp.maximum(jnp.maximum(jnp.max(sc, axis=-1, keepdims=True), jnp.max(sb, axis=-1, keepdims=True)), sink)
        pc = jnp.exp(sc - mx)
        pb = jnp.exp(sb - mx)
        den = jnp.sum(pc, axis=-1, keepdims=True) + jnp.sum(pb, axis=-1, keepdims=True) + jnp.exp(sink - mx)
        o = (_dot(pc.astype(BF16), vc) + _dot(pb.astype(BF16), vb)) / den
        outs.extend(o[r * ATTN_BLOCK:(r + 1) * ATTN_BLOCK] for r in range(rep))
    o_ref[0] = jnp.concatenate(outs, axis=-1).astype(BF16)


def _attention(q, kv, sinks, n_batch, t_len):
    nqb = t_len // ATTN_BLOCK
    first = N_CTX // ATTN_BLOCK
    q3 = q.reshape(n_batch, t_len, D_Q)
    kv3 = kv.reshape(n_batch, t_len, 2 * D_KV)

    def band(off):
        return lambda b, j, s: (b, jnp.clip(j + off, first, nqb - 1), 0)

    rep = N_Q_HEADS // N_KV_HEADS
    iq = np.arange(rep * ATTN_BLOCK)[:, None] % ATTN_BLOCK
    ik = np.arange(3 * ATTN_BLOCK)[None, :]
    window_bias = jnp.asarray(np.where(np.abs(iq + ATTN_BLOCK - ik) <= WINDOW, 0.0, NEG), F32)

    out = pl.pallas_call(
        functools.partial(_attn_kernel, n_lat=t_len - N_CTX),
        grid_spec=pltpu.PrefetchScalarGridSpec(
            num_scalar_prefetch=1,
            grid=(n_batch, nqb),
            in_specs=[
                pl.BlockSpec((1, ATTN_BLOCK, D_Q), lambda b, j, s: (b, j, 0)),
                pl.BlockSpec((1, N_CTX, 2 * D_KV), lambda b, j, s: (b, 0, 0)),
                pl.BlockSpec((1, ATTN_BLOCK, 2 * D_KV), band(-1)),
                pl.BlockSpec((1, ATTN_BLOCK, 2 * D_KV), band(0)),
                pl.BlockSpec((1, ATTN_BLOCK, 2 * D_KV), band(1)),
                pl.BlockSpec((rep * ATTN_BLOCK, 3 * ATTN_BLOCK), lambda b, j, s: (0, 0)),
            ],
            out_specs=pl.BlockSpec((1, ATTN_BLOCK, D_Q), lambda b, j, s: (b, j, 0)),
        ),
        out_shape=jax.ShapeDtypeStruct((n_batch, t_len, D_Q), BF16),
        compiler_params=_cparams(("parallel", "parallel")),
        name="attention",
    )(sinks, q3, kv3, kv3, kv3, kv3, window_bias)
    return out.reshape(n_batch * t_len, D_Q)


def _gelu_tanh(x):
    return 0.5 * x * (1.0 + jnp.tanh(math.sqrt(2.0 / math.pi) * (x + 0.044715 * (x * x * x))))


def _merge_kernel(x_ref, yf_ref, yb_ref, u_ref, cz_ref, czp_ref, czn_ref, ya_ref, gt_ref, mod_ref,
                  d_ref, cw_ref, wglu_ref, wso_ref, wco_ref, wao_ref, wo_ref, n2g_ref,
                  wrh_ref, wrl_ref, br_ref, tri_ref, triu_ref, xo_ref, h2_ref, rt_ref, cnt_ref, lg_ref,
                  *, tiles_per_seq, n_tiles):
    @pl.when(pl.program_id(0) == 0)
    def _():
        lg_ref[...] = jnp.full(lg_ref.shape, NEG, F32)

    rt_ref[...], cnt_ref[...] = _route_tile(lg_ref[...], tri_ref, triu_ref)

    r = jnp.minimum(pl.program_id(0), n_tiles - 1) % tiles_per_seq
    ys = yf_ref[...] + yb_ref[...] + d_ref[...] * u_ref[...].astype(F32)
    z = _gelu_tanh(ys)
    glu = z * _sigmoid(_dot(z.astype(BF16), wglu_ref[...]))
    br_ssm = _dot(glu.astype(BF16), wso_ref[...])

    cz = cz_ref[...].astype(F32)
    cb = cz[:, 0:D_CONV]
    zz = cz[:, D_CONV:2 * D_CONV]
    seg_first = r <= 1
    seg_last = (r == 0) | (r == tiles_per_seq - 1)
    prev_row = jnp.where(seg_first, 0.0, czp_ref[7:8, D_CONV:2 * D_CONV].astype(F32))
    next_row = jnp.where(seg_last, 0.0, czn_ref[0:1, D_CONV:2 * D_CONV].astype(F32))
    row = lax.broadcasted_iota(I32, (TM, D_CONV), 0)
    z_dn = jnp.where(row == 0, prev_row, pltpu.roll(zz, 1, axis=0))
    z_up = jnp.where(row == TM - 1, next_row, pltpu.roll(zz, TM - 1, axis=0))
    y_conv = cb * (cw_ref[0:1, :] * z_dn + cw_ref[1:2, :] * zz + cw_ref[2:3, :] * z_up)
    br_conv = _dot(y_conv.astype(BF16), wco_ref[...])
    br_attn = _dot(ya_ref[...], wao_ref[...])

    merged = (gt_ref[:, 0:D_MODEL].astype(F32) * br_ssm
              + gt_ref[:, D_MODEL:2 * D_MODEL].astype(F32) * br_conv
              + gt_ref[:, 2 * D_MODEL:3 * D_MODEL].astype(F32) * br_attn)
    mix = _dot(merged.astype(BF16), wo_ref[...])
    m = mod_ref[0]
    xn = x_ref[...] + m[:, 2 * D_MODEL:3 * D_MODEL] * mix
    xo_ref[...] = xn
    h2 = _rms(xn, n2g_ref[...]) * (1.0 + m[:, 4 * D_MODEL:5 * D_MODEL]) + m[:, 3 * D_MODEL:4 * D_MODEL]
    hi = h2.astype(BF16)
    h2_ref[...] = hi
    lo = (h2 - hi.astype(F32)).astype(BF16)
    lg_ref[...] = _dot(hi, wrh_ref[...]) + _dot(lo, wrh_ref[...]) + _dot(hi, wrl_ref[...]) + br_ref[...]


def _merge(x, yf, yb, u_tb, cz, ya, gt, mod_tiles, d_skip, conv_w, wglu, wso, wco, wao, wo, n2g,
           wr_hi, wr_lo, b_r, tri, triu, tiles_per_seq):
    tok = x.shape[0]
    nt = tok // TM
    cur = lambda i: jnp.minimum(i, nt - 1)
    tile = lambda i: (cur(i), 0)
    routed = lambda i: (jnp.maximum(i - 1, 0), 0)
    const = lambda i: (0, 0)
    tb = lambda i: (cur(i) % tiles_per_seq, cur(i) // tiles_per_seq)
    rows8 = TM // 8
    return pl.pallas_call(
        functools.partial(_merge_kernel, tiles_per_seq=tiles_per_seq, n_tiles=nt),
        grid=(nt + 1,),
        in_specs=[
            pl.BlockSpec((TM, D_MODEL), tile),
            pl.BlockSpec((TM, D_SSM), tb),
            pl.BlockSpec((TM, D_SSM), tb),
            pl.BlockSpec((TM, D_SSM), tb),
            pl.BlockSpec((TM, 2 * D_CONV), tile),
            pl.BlockSpec((8, 2 * D_CONV), lambda i: (jnp.maximum(cur(i) * rows8 - 1, 0), 0)),
            pl.BlockSpec((8, 2 * D_CONV), lambda i: (jnp.minimum((cur(i) + 1) * rows8, tok // 8 - 1), 0)),
            pl.BlockSpec((TM, D_Q), tile),
            pl.BlockSpec((TM, N_BRANCH * D_MODEL), tile),
            pl.BlockSpec((1, 1, 6 * D_MODEL), lambda i: (cur(i), 0, 0)),
            pl.BlockSpec((1, D_SSM), const),
            pl.BlockSpec((3, D_CONV), const),
            pl.BlockSpec((D_SSM, D_SSM), const),
            pl.BlockSpec((D_SSM, D_MODEL), const),
            pl.BlockSpec((D_CONV, D_MODEL), const),
            pl.BlockSpec((D_Q, D_MODEL), const),
            pl.BlockSpec((D_MODEL, D_MODEL), const),
            pl.BlockSpec((1, D_MODEL), const),
            pl.BlockSpec((D_MODEL, LANES), const),
            pl.BlockSpec((D_MODEL, LANES), const),
            pl.BlockSpec((1, LANES), const),
            pl.BlockSpec((TM, TM), const),
            pl.BlockSpec((LANES, LANES), const),
        ],
        out_specs=[
            pl.BlockSpec((TM, D_MODEL), tile),
            pl.BlockSpec((TM, D_MODEL), tile),
            pl.BlockSpec((TM, LANES), routed),
            pl.BlockSpec((8, LANES), routed),
        ],
        out_shape=[
            jax.ShapeDtypeStruct((tok, D_MODEL), F32),
            jax.ShapeDtypeStruct((tok, D_MODEL), BF16),
            jax.ShapeDtypeStruct((tok, LANES), F32),
            jax.ShapeDtypeStruct((nt * 8, LANES), F32),
        ],
        scratch_shapes=[pltpu.VMEM((TM, LANES), F32)],
        compiler_params=_cparams(("arbitrary",)),
        name="branch_merge",
    )(x, yf, yb, u_tb, cz, cz, cz, ya, gt, mod_tiles, d_skip, conv_w, wglu, wso, wco, wao, wo, n2g,
      wr_hi, wr_lo, b_r, tri, triu)


def _route_tile(l, tri_ref, triu_ref):
    lane = lax.broadcasted_iota(I32, l.shape, 1)
    vals, idxs, hots = [], [], []
    for _ in range(TOP_K):
        mx = jnp.max(l, axis=-1, keepdims=True)
        idx = jnp.min(jnp.where(l == mx, lane, LANES), axis=-1, keepdims=True)
        hot = lane == idx
        l = jnp.where(hot, -3e38, l)
        vals.append(mx)
        idxs.append(idx)
        hots.append(hot)
    ex = [jnp.exp(v - vals[0]) for v in vals]
    den = ex[0] + ex[1] + ex[2] + ex[3]
    picked = jnp.zeros(l.shape, F32)
    for hot in hots:
        picked = picked + hot.astype(F32)
    cum = _dot(tri_ref[...], picked.astype(BF16))
    cnt = jnp.sum(picked, axis=0, keepdims=True)
    seg = jnp.floor((cnt + (SEG - 1.0)) * (1.0 / SEG)) * SEG
    run_start = _dot(jnp.broadcast_to(seg, (8, LANES)).astype(BF16), triu_ref[...])[0:1]
    pos = cum + run_start
    out = jnp.zeros(l.shape, F32)
    for k in range(TOP_K):
        row = jnp.sum(jnp.where(hots[k], pos, 0.0), axis=-1, keepdims=True)
        out = jnp.where(lane == k, idxs[k].astype(F32), out)
        out = jnp.where(lane == TOP_K + k, row, out)
        out = jnp.where(lane == 2 * TOP_K + k, ex[k] / den, out)
    return out, jnp.broadcast_to(cnt, (8, LANES))


MAX_PIECES = LOCAL_ROWS // SEG


def _for_each_piece(tile, grow_ref, tot_ref, fn):
    base = tile * MAX_PIECES

    def per_piece(p, c):
        fn(pl.multiple_of(p * SEG, SEG), pl.multiple_of(grow_ref[base + p], SEG))
        return c

    lax.fori_loop(0, tot_ref[tile], per_piece, 0)


ROW_CHUNK = 256


def _local_rows(rt, c):
    rows = c * ROW_CHUNK + lax.broadcasted_iota(I32, (TM, ROW_CHUNK), 1)
    return [rows == rt[:, TOP_K + k:TOP_K + k + 1].astype(I32) for k in range(TOP_K)]


def _dispatch_kernel(grow_ref, tot_ref, zstart_ref, nu_ref, h_ref, rt_ref, xs_hbm,
                     xl_ref, zero_ref, sem_ref, sem_z, *, n_blocks, n_tiles):
    i = pl.program_id(0)
    slot = i % 2

    @pl.when(i == 0)
    def _():
        zero_ref[...] = jnp.zeros_like(zero_ref)

        def zero_copy(start):
            rows = pl.ds(pl.multiple_of(start, MOE_BLOCK), MOE_BLOCK)
            return pltpu.make_async_copy(zero_ref, xs_hbm.at[rows, :], sem_z)

        def tail_start(b, c):
            zero_copy(b * MOE_BLOCK).start()
            return c

        def tail_wait(b, c):
            zero_copy(b * MOE_BLOCK).wait()
            return c

        for e in range(N_EXPERTS):
            @pl.when(zstart_ref[e] >= 0)
            def _():
                zero_copy(zstart_ref[e]).start()
        lax.fori_loop(nu_ref[0], n_blocks, tail_start, 0)
        for e in range(N_EXPERTS):
            @pl.when(zstart_ref[e] >= 0)
            def _():
                zero_copy(zstart_ref[e]).wait()
        lax.fori_loop(nu_ref[0], n_blocks, tail_wait, 0)

    rt = rt_ref[...]
    h = h_ref[...]
    for c in range(LOCAL_ROWS // ROW_CHUNK):
        place = jnp.zeros((TM, ROW_CHUNK), F32)
        for hit in _local_rows(rt, c):
            place = place + hit.astype(F32)
        xl_ref[slot, pl.ds(c * ROW_CHUNK, ROW_CHUNK), :] = lax.dot_general(
            place.astype(BF16), h, (((0,), (0,)), ((), ())), preferred_element_type=F32)

    def seg_copy(local_row, global_row, s):
        return pltpu.make_async_copy(xl_ref.at[s, pl.ds(local_row, SEG), :],
                                     xs_hbm.at[pl.ds(global_row, SEG), :], sem_ref.at[s])

    def wait_tile(tile, s):
        def one(g, c):
            seg_copy(0, 0, s).wait()
            return c

        lax.fori_loop(0, tot_ref[tile], one, 0)

    _for_each_piece(i, grow_ref, tot_ref, lambda lr, gr: seg_copy(lr, gr, slot).start())

    @pl.when(i > 0)
    def _():
        wait_tile(i - 1, 1 - slot)

    @pl.when(i == n_tiles - 1)
    def _():
        wait_tile(i, slot)


def _dispatch(h2, route, seg_meta, zstart, n_used, n_slots):
    tok = h2.shape[0]
    nt = tok // TM
    tile = lambda i, *_: (i, 0)
    return pl.pallas_call(
        functools.partial(_dispatch_kernel, n_blocks=n_slots // MOE_BLOCK, n_tiles=nt),
        grid_spec=pltpu.PrefetchScalarGridSpec(
            num_scalar_prefetch=4,
            grid=(nt,),
            in_specs=[
                pl.BlockSpec((TM, D_MODEL), tile),
                pl.BlockSpec((TM, LANES), tile),
            ],
            out_specs=pl.BlockSpec(memory_space=pl.ANY),
            scratch_shapes=[
                pltpu.VMEM((2, LOCAL_ROWS, D_MODEL), F32),
                pltpu.VMEM((MOE_BLOCK, D_MODEL), F32),
                pltpu.SemaphoreType.DMA((2,)),
                pltpu.SemaphoreType.DMA,
            ],
        ),
        out_shape=jax.ShapeDtypeStruct((n_slots, D_MODEL), F32),
        compiler_params=_cparams(("arbitrary",)),
        name="dispatch",
    )(*seg_meta, zstart, n_used, h2, route)


DEINT = 256


def _expert_kernel(be_ref, nu_ref, x_ref, wgu_ref, bg_ref, bu_ref, wd_ref, bd_ref, sel_ref, y_ref,
                   wg_s, wu_s, wd_s):
    i = pl.program_id(0)
    prev = be_ref[jnp.maximum(i - 1, 0)]

    @pl.when((i == 0) | (be_ref[i] != prev))
    def _():
        for c in range(2 * D_EXPERT // DEINT):
            w = wgu_ref[0, 0, :, c * DEINT:(c + 1) * DEINT].astype(BF16)
            cols = pl.ds(c * (DEINT // 2), DEINT // 2)
            split = _dot(w, sel_ref[...])
            wg_s[:, cols] = split[:, 0:DEINT // 2].astype(BF16)
            wu_s[:, cols] = split[:, DEINT // 2:DEINT].astype(BF16)
        wd_s[...] = wd_ref[0, 0].astype(BF16)

    @pl.when(i < nu_ref[0])
    def _():
        x = x_ref[...].astype(BF16)
        g = _dot(x, wg_s[...]) + bg_ref[0]
        u = _dot(x, wu_s[...]) + bu_ref[0]
        glu = jnp.minimum(g, SWIGLU_LIMIT)
        up = jnp.clip(u, -SWIGLU_LIMIT, SWIGLU_LIMIT)
        act = glu * _sigmoid(SWIGLU_ALPHA * glu) * (up + 1.0)
        y_ref[...] = _dot(act.astype(BF16), wd_s[...]) + bd_ref[0]

    @pl.when(i >= nu_ref[0])
    def _():
        y_ref[...] = jnp.zeros_like(y_ref)


def _experts(xs, block_expert, n_used, layer, wgu_all, bg, bu, wd_all, bd, sel):
    n_slots = xs.shape[0]
    n_blocks = n_slots // MOE_BLOCK
    wmap = lambda i, be, nu: (be[i], 0, 0)
    lwmap = lambda i, be, nu: (layer, be[i], 0, 0)
    const = lambda i, be, nu: (0, 0)
    return pl.pallas_call(
        _expert_kernel,
        grid_spec=pltpu.PrefetchScalarGridSpec(
            num_scalar_prefetch=2,
            grid=(n_blocks,),
            in_specs=[
                pl.BlockSpec((MOE_BLOCK, D_MODEL), lambda i, be, nu: (jnp.minimum(i, nu[0] - 1), 0)),
                pl.BlockSpec((1, 1, D_MODEL, 2 * D_EXPERT), lwmap),
                pl.BlockSpec((1, 1, D_EXPERT), wmap),
                pl.BlockSpec((1, 1, D_EXPERT), wmap),
                pl.BlockSpec((1, 1, D_EXPERT, D_MODEL), lwmap),
                pl.BlockSpec((1, 1, D_MODEL), wmap),
                pl.BlockSpec((DEINT, DEINT), const),
            ],
            out_specs=pl.BlockSpec((MOE_BLOCK, D_MODEL), lambda i, be, nu: (i, 0)),
            scratch_shapes=[
                pltpu.VMEM((D_MODEL, D_EXPERT), BF16),
                pltpu.VMEM((D_MODEL, D_EXPERT), BF16),
                pltpu.VMEM((D_EXPERT, D_MODEL), BF16),
            ],
        ),
        out_shape=jax.ShapeDtypeStruct((n_slots, D_MODEL), F32),
        compiler_params=_cparams(("arbitrary",)),
        name="experts",
    )(block_expert, n_used, xs, wgu_all, bg, bu, wd_all, bd, sel)


def _combine_kernel(grow_ref, tot_ref, ys_hbm, x_ref, rt_ref, mod_ref, o_ref, yl_ref, sem_ref,
                    *, n_tiles):
    i = pl.program_id(0)
    slot = i % 2

    def seg_copy(local_row, global_row, s):
        return pltpu.make_async_copy(ys_hbm.at[pl.ds(global_row, SEG), :],
                                     yl_ref.at[s, pl.ds(local_row, SEG), :], sem_ref.at[s])

    def fetch_tile(tile, s):
        _for_each_piece(tile, grow_ref, tot_ref, lambda lr, gr: seg_copy(lr, gr, s).start())

    @pl.when(i == 0)
    def _():
        yl_ref[...] = jnp.zeros_like(yl_ref)
        fetch_tile(0, 0)

    @pl.when(i + 1 < n_tiles)
    def _():
        fetch_tile(i + 1, 1 - slot)

    def one(g, c):
        seg_copy(0, 0, slot).wait()
        return c

    lax.fori_loop(0, tot_ref[i], one, 0)

    rt = rt_ref[...]
    y = jnp.zeros((TM, D_MODEL), F32)
    for c in range(LOCAL_ROWS // ROW_CHUNK):
        weights = jnp.zeros((TM, ROW_CHUNK), F32)
        for k, hit in enumerate(_local_rows(rt, c)):
            weights = weights + jnp.where(hit, rt[:, 2 * TOP_K + k:2 * TOP_K + k + 1], 0.0)
        w_hi = weights.astype(BF16)
        w_lo = (weights - w_hi.astype(F32)).astype(BF16)
        yl = yl_ref[slot, pl.ds(c * ROW_CHUNK, ROW_CHUNK), :].astype(BF16)
        y = y + _dot(w_hi, yl) + _dot(w_lo, yl)
    o_ref[...] = (x_ref[...] + mod_ref[0][:, 5 * D_MODEL:6 * D_MODEL] * y).reshape(o_ref.shape)


def _combine(x, ys, route, seg_meta, mod_tiles, out_seq=None):
    tok = x.shape[0]
    nt = tok // TM
    tile = lambda i, *_: (i, 0)
    if out_seq is None:
        out_spec = pl.BlockSpec((TM, D_MODEL), tile)
        out_shape = jax.ShapeDtypeStruct((tok, D_MODEL), F32)
    else:
        n_batch, tps = out_seq
        out_spec = pl.BlockSpec((1, TM, D_MODEL), lambda i, *_: (i // tps, jnp.maximum(i % tps - 1, 0), 0))
        out_shape = jax.ShapeDtypeStruct((n_batch, (tps - 1) * TM, D_MODEL), F32)
    return pl.pallas_call(
        functools.partial(_combine_kernel, n_tiles=nt),
        grid_spec=pltpu.PrefetchScalarGridSpec(
            num_scalar_prefetch=2,
            grid=(nt,),
            in_specs=[
                pl.BlockSpec(memory_space=pl.ANY),
                pl.BlockSpec((TM, D_MODEL), tile),
                pl.BlockSpec((TM, LANES), tile),
                pl.BlockSpec((1, 1, 6 * D_MODEL), lambda i, *_: (i, 0, 0)),
            ],
            out_specs=out_spec,
            scratch_shapes=[
                pltpu.VMEM((2, LOCAL_ROWS, D_MODEL), F32),
                pltpu.SemaphoreType.DMA((2,)),
            ],
        ),
        out_shape=out_shape,
        compiler_params=_cparams(("arbitrary",)),
        name="combine",
    )(*seg_meta, ys, x, route, mod_tiles)


def _rope_tables(t_len):
    n_lat = t_len - N_CTX
    t = np.arange(n_lat)
    n_pairs = HEAD_DIM // 4
    inv_freq = jnp.asarray(ROPE_BASE, F32) ** (-jnp.arange(n_pairs, dtype=F32) / n_pairs)
    row = jnp.asarray(t // GRID_W, F32)
    col = jnp.asarray(t % GRID_W, F32)
    ang = jnp.concatenate([row[:, None] * inv_freq, col[:, None] * inv_freq], axis=-1)
    ang = jnp.concatenate([jnp.zeros((N_CTX, HEAD_DIM // 2), F32), ang], axis=0)
    cos = jnp.tile(jnp.cos(ang), (1, LANES // (HEAD_DIM // 2)))
    sin = jnp.tile(jnp.sin(ang), (1, LANES // (HEAD_DIM // 2)))
    return cos, sin


def _s5_operands(lam_re, lam_im, log_dt, b_re, b_im, c_re, c_im):
    lam = lax.complex(lam_re, lam_im)
    dt = jnp.exp(log_dt)[..., None]
    lam_bar = jnp.exp(lam * dt)
    b_bar = ((lam_bar - 1.0) / lam)[..., None] * lax.complex(b_re, b_im)
    eye = jnp.eye(N_SSM_GROUPS, dtype=F32)

    def in_block(w):
        return jnp.einsum("zgph,gk->zghkp", w, eye).reshape(2, D_SSM, N_STATE)

    def out_block(w):
        return jnp.einsum("zghp,gk->zgpkh", w, eye).reshape(2, N_STATE, D_SSM)

    bd = jnp.concatenate([in_block(b_bar.real), in_block(b_bar.imag)], axis=-1).astype(BF16)
    cd = jnp.concatenate([out_block(c_re), out_block(-c_im)], axis=1).astype(BF16)
    lam_rows = jnp.stack([lam_bar[0].real.reshape(-1), lam_bar[0].imag.reshape(-1),
                          lam_bar[1].real.reshape(-1), lam_bar[1].imag.reshape(-1)])
    return bd, cd, lam_rows


def _moe_constants():
    tri = jnp.asarray(np.tril(np.ones((TM, TM)), -1), BF16)
    triu = jnp.asarray(np.triu(np.ones((LANES, LANES)), 1), BF16)
    pick = np.arange(DEINT)[:, None] == 2 * np.arange(DEINT // 2)[None, :]
    sel = jnp.asarray(np.concatenate([pick, np.roll(pick, 1, axis=0)], axis=1), BF16)
    return tri, triu, sel


def _moe(xs, h2, route, counts, mod_tiles, layer, w_gate_up, b_gate_up, w_down, b_down, consts, out_seq=None):
    sel = consts[2]
    tok = xs.shape[0]
    nt = tok // TM
    n_blocks = (tok * TOP_K + nt * N_EXPERTS * (SEG - 1) + N_EXPERTS * (MOE_BLOCK - 1)) // MOE_BLOCK
    n_slots = n_blocks * MOE_BLOCK

    counts = counts.reshape(nt, 8, LANES)[:, 0, :N_EXPERTS].astype(I32)
    seg = (counts + SEG - 1) // SEG * SEG
    region = jnp.sum(seg, axis=0)
    padded = (region + MOE_BLOCK - 1) // MOE_BLOCK * MOE_BLOCK
    pad_end = jnp.cumsum(padded)
    pad_start = pad_end - padded
    run_global = pad_start[None, :] + jnp.cumsum(seg, axis=0) - seg
    run_local = jnp.cumsum(seg, axis=1) - seg
    pieces = seg // SEG
    ends = jnp.cumsum(pieces, axis=1)
    shift = run_global - run_local
    jump = shift - jnp.concatenate([jnp.zeros((nt, 1), I32), shift[:, :-1]], axis=1)
    piece = jnp.arange(MAX_PIECES, dtype=I32)
    started = piece[None, None, :] >= (ends - pieces)[:, :, None]
    piece_global = piece[None, :] * SEG + jnp.sum(jnp.where(started, jump[:, :, None], 0), axis=1)
    seg_meta = (piece_global.reshape(-1).astype(I32), ends[:, -1].astype(I32))
    n_used = (pad_end[-1] // MOE_BLOCK).astype(I32)
    blk = jnp.minimum(jnp.arange(n_blocks, dtype=I32), n_used - 1) * MOE_BLOCK
    block_expert = jnp.minimum(jnp.sum(blk[:, None] >= pad_end[None, :], axis=1), N_EXPERTS - 1).astype(I32)
    zstart = jnp.where(region > 0, pad_end - MOE_BLOCK, -1).astype(I32)

    x_sorted = _dispatch(h2, route, seg_meta, zstart, n_used.reshape(1), n_slots)
    y_sorted = _experts(x_sorted, block_expert, n_used.reshape(1), layer, w_gate_up,
                        b_gate_up[:, None, 0::2], b_gate_up[:, None, 1::2],
                        w_down, b_down[:, None, :], sel)
    return _combine(xs, y_sorted, route, seg_meta, mod_tiles, out_seq)


def kernel(x, c, ctx, c_ctx, w_mod, b_mod, norm1_g, norm2_g, w_in, ssm_lam_re, ssm_lam_im, ssm_log_dt, ssm_b_re, ssm_b_im, ssm_c_re, ssm_c_im, ssm_d, w_glu, w_ssm_out, conv_w, w_conv_out, q_norm_g, k_norm_g, attn_sinks, w_attn_out, w_o, w_router, b_router, w_gate_up, b_gate_up, w_down, b_down):
    n_batch, n_lat, _ = x.shape
    t_len = N_CTX + n_lat
    tiles_per_seq = t_len // TM
    tok = n_batch * t_len
    nt = tok // TM

    xs = _assemble(ctx, x, tiles_per_seq)

    mod_rows = 8 * ((n_batch + 1 + 7) // 8)
    cvec = jnp.zeros((mod_rows, D_MODEL), F32).at[:n_batch].set(c).at[n_batch].set(c_ctx)
    mod_all = _modulation(cvec, w_mod, b_mod)
    tile_ids = np.arange(nt)
    tile_row = np.where(tile_ids % tiles_per_seq == 0, n_batch, tile_ids // tiles_per_seq)

    cos_t, sin_t = _rope_tables(t_len)
    head_sum = jnp.asarray(np.kron(np.eye(N_Q_HEADS), np.full((HEAD_DIM, HEAD_DIM), 1.0 / HEAD_DIM)), BF16)
    moe_consts = _moe_constants()

    for l in range(DEPTH):
        mod_tiles = mod_all[l][tile_row].reshape(nt, 1, 6 * D_MODEL)
        qg = jnp.tile(q_norm_g[l], N_Q_HEADS).reshape(1, D_Q)
        kg = jnp.tile(k_norm_g[l], N_KV_HEADS).reshape(1, D_KV)
        u_tb, cz, q, kv, gt = _in_proj(xs, mod_tiles, norm1_g[l].reshape(1, D_MODEL), w_in[l].astype(BF16),
                                       cos_t, sin_t, qg, kg, head_sum, n_batch, tiles_per_seq)

        bd, cd, lam_rows = _s5_operands(ssm_lam_re[l], ssm_lam_im[l], ssm_log_dt[l], ssm_b_re[l], ssm_b_im[l],
                                        ssm_c_re[l], ssm_c_im[l])
        yf, yb = _s5(u_tb, bd, cd, lam_rows, n_batch, t_len)
        ya = _attention(q, kv, attn_sinks[l], n_batch, t_len)

        wr = jnp.zeros((D_MODEL, LANES), F32).at[:, :N_EXPERTS].set(w_router[l])
        wr_hi = wr.astype(BF16)
        wr_lo = (wr - wr_hi.astype(F32)).astype(BF16)
        b_r = jnp.full((1, LANES), NEG, F32).at[0, :N_EXPERTS].set(b_router[l])
        xs, h2, route, counts = _merge(
            xs, yf, yb, u_tb,
            cz, ya, gt, mod_tiles, ssm_d[l].reshape(1, D_SSM), conv_w[l],
            w_glu[l].astype(BF16), w_ssm_out[l].astype(BF16), w_conv_out[l].astype(BF16),
            w_attn_out[l].astype(BF16), w_o[l].astype(BF16), norm2_g[l].reshape(1, D_MODEL),
            wr_hi, wr_lo, b_r, moe_consts[0], moe_consts[1], tiles_per_seq)

        out_seq = None
        if l == DEPTH - 1:
            out_seq = (n_batch, tiles_per_seq)
            ctx_tile = jnp.asarray(tile_ids % tiles_per_seq == 0)
            counts = jnp.where(jnp.repeat(ctx_tile, 8)[:, None], 0.0, counts)
            lane = jnp.arange(LANES)[None, :]
            row_lanes = (lane >= TOP_K) & (lane < 2 * TOP_K)
            route = jnp.where(jnp.repeat(ctx_tile, TM)[:, None] & row_lanes, float(LOCAL_ROWS), route)
        xs = _moe(xs, h2, route, counts, mod_tiles, l, w_gate_up, b_gate_up[l], w_down, b_down[l], moe_consts,
                  out_seq)

    return xs
```

```python
import functools
import math

import jax
import jax.numpy as jnp
import numpy as np
from jax import lax
from jax.experimental import pallas as pl
from jax.experimental.pallas import tpu as pltpu

F32 = jnp.float32
BF16 = jnp.bfloat16
I32 = jnp.int32

D_MODEL = 1024
DEPTH = 4
N_CTX = 256
HEAD_DIM = 64
N_Q_HEADS = 8
N_KV_HEADS = 2
D_Q = N_Q_HEADS * HEAD_DIM
D_KV = N_KV_HEADS * HEAD_DIM
WINDOW = 128
ATTN_BLOCK = 128
ROPE_BASE = 10000.0
GRID_W = 64
D_SSM = 256
SSM_GROUP = 16
N_SSM_GROUPS = 16
SSM_STATE = 64
N_STATE = N_SSM_GROUPS * SSM_STATE
D_CONV = 256
N_BRANCH = 3
D_IN = D_SSM + 3 * D_CONV + D_Q + 2 * D_KV + N_BRANCH * D_MODEL
N_EXPERTS = 32
TOP_K = 4
D_EXPERT = 1024
SWIGLU_LIMIT = 7.0
SWIGLU_ALPHA = 1.702
EPS = 1e-6

LANES = 128
TM = 256
MOE_BLOCK = 512
SEG = 8
LOCAL_ROWS = -(-(TM * TOP_K + N_EXPERTS * (SEG - 1)) // LANES) * LANES
S5_CHUNK = 64
S5_COLS = 512
NEG = -1e30
VMEM_LIMIT = 56 * 1024 * 1024


def _cparams(sem):
    return pltpu.CompilerParams(dimension_semantics=sem, vmem_limit_bytes=VMEM_LIMIT)


def _dot(a, b):
    return jnp.dot(a, b, preferred_element_type=F32)


def _sigmoid(x):
    return 1.0 / (1.0 + jnp.exp(-x))


def _rms(x, g):
    ms = jnp.mean(x * x, axis=-1, keepdims=True)
    return x * lax.rsqrt(ms + EPS) * g


def _mod_kernel(c_ref, w_ref, b_ref, o_ref):
    c = c_ref[...]
    s = (c * _sigmoid(c)).astype(BF16)
    o_ref[0] = _dot(s, w_ref[0].astype(BF16)) + b_ref[0]


def _modulation(cvec, w_mod, b_mod):
    rows = cvec.shape[0]
    nblk = 1536
    return pl.pallas_call(
        _mod_kernel,
        grid=(DEPTH, 6 * D_MODEL // nblk),
        in_specs=[
            pl.BlockSpec((rows, D_MODEL), lambda l, j: (0, 0)),
            pl.BlockSpec((1, D_MODEL, nblk), lambda l, j: (l, 0, j)),
            pl.BlockSpec((1, 1, nblk), lambda l, j: (l, 0, j)),
        ],
        out_specs=pl.BlockSpec((1, rows, nblk), lambda l, j: (l, 0, j)),
        out_shape=jax.ShapeDtypeStruct((DEPTH, rows, 6 * D_MODEL), F32),
        compiler_params=_cparams(("arbitrary", "arbitrary")),
        name="modulation",
    )(cvec, w_mod, b_mod.reshape(DEPTH, 1, 6 * D_MODEL))


def _assemble_kernel(c_ref, x_ref, o_ref, *, tiles_per_seq):
    r = pl.program_id(0) % tiles_per_seq

    @pl.when(r == 0)
    def _():
        o_ref[...] = c_ref[0]

    @pl.when(r > 0)
    def _():
        o_ref[...] = x_ref[0]


def _assemble(ctx, x, tiles_per_seq):
    n_batch = x.shape[0]
    nt = n_batch * tiles_per_seq
    return pl.pallas_call(
        functools.partial(_assemble_kernel, tiles_per_seq=tiles_per_seq),
        grid=(nt,),
        in_specs=[
            pl.BlockSpec((1, TM, D_MODEL), lambda i: (i // tiles_per_seq, 0, 0)),
            pl.BlockSpec((1, TM, D_MODEL), lambda i: (i // tiles_per_seq, jnp.maximum(i % tiles_per_seq - 1, 0), 0)),
        ],
        out_specs=pl.BlockSpec((TM, D_MODEL), lambda i: (i, 0)),
        out_shape=jax.ShapeDtypeStruct((nt * TM, D_MODEL), F32),
        compiler_params=_cparams(("arbitrary",)),
        name="assemble",
    )(ctx, x)


def _rot_half(x, width):
    lane = lax.broadcasted_iota(I32, x.shape, 1)
    first = (lane % HEAD_DIM) < (HEAD_DIM // 2)
    return jnp.where(first, -pltpu.roll(x, width - HEAD_DIM // 2, axis=1), pltpu.roll(x, HEAD_DIM // 2, axis=1))


def _inproj_kernel(x_ref, mod_ref, g_ref, w_ref, cos_ref, sin_ref, qg_ref, kg_ref, hs_ref,
                   u_ref, cz_ref, q_ref, kv_ref, gt_ref):
    m = mod_ref[0]
    h = _rms(x_ref[...], g_ref[...])
    h = (h * (1.0 + m[:, D_MODEL:2 * D_MODEL]) + m[:, 0:D_MODEL]).astype(BF16)
    o_gate = D_SSM + 3 * D_CONV + D_Q + 2 * D_KV
    y = _dot(h, w_ref[:, 0:o_gate])
    u_ref[...] = y[:, 0:D_SSM].astype(BF16)
    cb = y[:, D_SSM:D_SSM + D_CONV]
    cc = y[:, D_SSM + D_CONV:D_SSM + 2 * D_CONV]
    cx = y[:, D_SSM + 2 * D_CONV:D_SSM + 3 * D_CONV]
    cz_ref[...] = jnp.concatenate([cb, cc * cx], axis=-1).astype(BF16)
    o_q = D_SSM + 3 * D_CONV
    q = y[:, o_q:o_q + D_Q]
    k = y[:, o_q + D_Q:o_q + D_Q + D_KV]
    v = y[:, o_q + D_Q + D_KV:o_gate]
    cos = cos_ref[...]
    sin = sin_ref[...]
    q_ms = _dot((q * q).astype(BF16), hs_ref[...])
    qn = q * lax.rsqrt(q_ms + EPS) * qg_ref[...]
    cos_q = jnp.concatenate([cos] * (D_Q // LANES), axis=-1)
    sin_q = jnp.concatenate([sin] * (D_Q // LANES), axis=-1)
    qr = qn * cos_q + _rot_half(qn, D_Q) * sin_q
    q_ref[...] = (qr * (HEAD_DIM ** -0.5)).astype(BF16)
    k_ms = _dot((k * k).astype(BF16), hs_ref[0:D_KV, 0:D_KV])
    kn = k * lax.rsqrt(k_ms + EPS) * kg_ref[...]
    kr = kn * cos + _rot_half(kn, D_KV) * sin
    kv_ref[...] = jnp.concatenate([kr, v], axis=-1).astype(BF16)
    gt = _dot(h, w_ref[:, o_gate:D_IN])
    gt_ref[...] = _sigmoid(gt).astype(BF16)


def _in_proj(x, mod_tiles, norm_g, w_in_bf, cos_t, sin_t, qg, kg, head_sum, n_batch, tiles_per_seq):
    tok = x.shape[0]
    nt = tok // TM
    t_len = tiles_per_seq * TM
    tile = lambda i: (i, 0)
    const = lambda i: (0, 0)
    seq_tile = lambda i: (i % tiles_per_seq, 0)
    return pl.pallas_call(
        _inproj_kernel,
        grid=(nt,),
        in_specs=[
            pl.BlockSpec((TM, D_MODEL), tile),
            pl.BlockSpec((1, 1, 6 * D_MODEL), lambda i: (i, 0, 0)),
            pl.BlockSpec((1, D_MODEL), const),
            pl.BlockSpec((D_MODEL, D_IN), const),
            pl.BlockSpec((TM, LANES), seq_tile),
            pl.BlockSpec((TM, LANES), seq_tile),
            pl.BlockSpec((1, D_Q), const),
            pl.BlockSpec((1, D_KV), const),
            pl.BlockSpec((D_Q, D_Q), const),
        ],
        out_specs=[
            pl.BlockSpec((TM, D_SSM), lambda i: (i % tiles_per_seq, i // tiles_per_seq)),
            pl.BlockSpec((TM, 2 * D_CONV), tile),
            pl.BlockSpec((TM, D_Q), tile),
            pl.BlockSpec((TM, 2 * D_KV), tile),
            pl.BlockSpec((TM, N_BRANCH * D_MODEL), tile),
        ],
        out_shape=[
            jax.ShapeDtypeStruct((t_len, n_batch * D_SSM), BF16),
            jax.ShapeDtypeStruct((tok, 2 * D_CONV), BF16),
            jax.ShapeDtypeStruct((tok, D_Q), BF16),
            jax.ShapeDtypeStruct((tok, 2 * D_KV), BF16),
            jax.ShapeDtypeStruct((tok, N_BRANCH * D_MODEL), BF16),
        ],
        compiler_params=_cparams(("parallel",)),
        name="in_proj",
    )(x, mod_tiles, norm_g, w_in_bf, cos_t, sin_t, qg, kg, head_sum)


def _s5_kernel(uf_ref, ub_ref, bd_ref, cd_ref, lam_ref, yf_ref, yb_ref, sf_ref, sb_ref, in_ref, out_ref, carry_ref,
               *, n_batch):
    i = pl.program_id(0)

    @pl.when(i == 0)
    def _():
        carry_ref[...] = jnp.zeros_like(carry_ref)

    halves = D_SSM // LANES

    def batch_rows(b):
        return pl.ds(b, S5_CHUNK, stride=n_batch)

    def project_in(z, u_ref, s_ref):
        for b in range(n_batch):
            for h in range(halves):
                in_ref[z, h, batch_rows(b), :] = u_ref[:, pl.ds(b * D_SSM + h * LANES, LANES)].astype(F32)
        u_rows = jnp.concatenate([in_ref[z, h] for h in range(halves)], axis=-1)
        s_ref[...] = _dot(u_rows.astype(BF16), bd_ref[z])

    def recur(z, s_ref):
        for j in range(N_STATE // S5_COLS):
            re_cols = pl.ds(j * S5_COLS, S5_COLS)
            im_cols = pl.ds(N_STATE + j * S5_COLS, S5_COLS)
            lr = jnp.broadcast_to(lam_ref[2 * z:2 * z + 1, re_cols], (n_batch, S5_COLS))
            li = jnp.broadcast_to(lam_ref[2 * z + 1:2 * z + 2, re_cols], (n_batch, S5_COLS))
            xr = carry_ref[2 * z, :, re_cols]
            xi = carry_ref[2 * z + 1, :, re_cols]
            for s in range(S5_CHUNK):
                t = s if z == 0 else S5_CHUNK - 1 - s
                rows = pl.ds(t * n_batch, n_batch)
                xr, xi = (lr * xr - li * xi + s_ref[rows, re_cols],
                          lr * xi + li * xr + s_ref[rows, im_cols])
                s_ref[rows, re_cols] = xr
                s_ref[rows, im_cols] = xi
            carry_ref[2 * z, :, re_cols] = xr
            carry_ref[2 * z + 1, :, re_cols] = xi

    def project_out(z, s_ref, y_ref):
        y_rows = _dot(s_ref[...].astype(BF16), cd_ref[z])
        for h in range(halves):
            out_ref[z, h] = y_rows[:, h * LANES:(h + 1) * LANES]
        for b in range(n_batch):
            for h in range(halves):
                y_ref[:, pl.ds(b * D_SSM + h * LANES, LANES)] = out_ref[z, h, batch_rows(b), :]

    project_in(0, uf_ref, sf_ref)
    project_in(1, ub_ref, sb_ref)
    recur(0, sf_ref)
    recur(1, sb_ref)
    project_out(0, sf_ref, yf_ref)
    project_out(1, sb_ref, yb_ref)


def _s5(u_tb, bd, cd, lam, n_batch, t_len):
    rows = S5_CHUNK * n_batch
    width = n_batch * D_SSM
    n_chunks = t_len // S5_CHUNK
    ctx_chunks = N_CTX // S5_CHUNK

    def bwd_block(i):
        return (jnp.where(i < ctx_chunks, ctx_chunks - 1 - i, n_chunks - 1 + ctx_chunks - i), 0)

    return pl.pallas_call(
        functools.partial(_s5_kernel, n_batch=n_batch),
        grid=(n_chunks,),
        in_specs=[
            pl.BlockSpec((S5_CHUNK, width), lambda i: (i, 0)),
            pl.BlockSpec((S5_CHUNK, width), bwd_block),
            pl.BlockSpec((2, D_SSM, 2 * N_STATE), lambda i: (0, 0, 0)),
            pl.BlockSpec((2, 2 * N_STATE, D_SSM), lambda i: (0, 0, 0)),
            pl.BlockSpec((4, N_STATE), lambda i: (0, 0)),
        ],
        out_specs=[
            pl.BlockSpec((S5_CHUNK, width), lambda i: (i, 0)),
            pl.BlockSpec((S5_CHUNK, width), bwd_block),
        ],
        out_shape=[jax.ShapeDtypeStruct((t_len, width), F32)] * 2,
        scratch_shapes=[
            pltpu.VMEM((rows, 2 * N_STATE), F32),
            pltpu.VMEM((rows, 2 * N_STATE), F32),
            pltpu.VMEM((2, D_SSM // LANES, rows, LANES), F32),
            pltpu.VMEM((2, D_SSM // LANES, rows, LANES), F32),
            pltpu.VMEM((4, n_batch, N_STATE), F32),
        ],
        compiler_params=_cparams(("arbitrary",)),
        name="s5_scan",
    )(u_tb, u_tb, bd, cd, lam)


def _attn_kernel(sink_ref, q_ref, kvc_ref, kv0_ref, kv1_ref, kv2_ref, wb_ref, o_ref, *, n_lat):
    n = pl.program_id(1) - N_CTX // ATTN_BLOCK
    rep = N_Q_HEADS // N_KV_HEADS
    rows = rep * ATTN_BLOCK
    head_of_row = lax.broadcasted_iota(I32, (rows, 1), 0) // ATTN_BLOCK
    contract_last = (((1,), (1,)), ((), ()))

    def attend(with_band):
        q = q_ref[0]
        kvc = kvc_ref[0]
        if with_band:
            band = jnp.concatenate([kv0_ref[0], kv1_ref[0], kv2_ref[0]], axis=0)
            key_block = n - 1 + lax.broadcasted_iota(I32, (1, 3 * ATTN_BLOCK), 1) // ATTN_BLOCK
            block_ok = (key_block >= 0) & (key_block < n_lat // ATTN_BLOCK)
            bias = wb_ref[...] + jnp.where(block_ok, 0.0, NEG)
        outs = []
        for g in range(N_KV_HEADS):
            qg = jnp.concatenate([q[:, (g * rep + r) * HEAD_DIM:(g * rep + r + 1) * HEAD_DIM] for r in range(rep)],
                                 axis=0)
            sink = jnp.zeros((rows, 1), F32)
            for r in range(rep):
                sink = jnp.where(head_of_row == r, sink_ref[g * rep + r], sink)
            kc = kvc[:, g * HEAD_DIM:(g + 1) * HEAD_DIM]
            vc = kvc[:, D_KV + g * HEAD_DIM:D_KV + (g + 1) * HEAD_DIM]
            sc = lax.dot_general(qg, kc, contract_last, preferred_element_type=F32)
            mx = jnp.maximum(jnp.max(sc, axis=-1, keepdims=True), sink)
            if with_band:
                kb = band[:, g * HEAD_DIM:(g + 1) * HEAD_DIM]
                vb = band[:, D_KV + g * HEAD_DIM:D_KV + (g + 1) * HEAD_DIM]
                sb = lax.dot_general(qg, kb, contract_last, preferred_element_type=F32) + bias
                mx = jnp.maximum(mx, jnp.max(sb, axis=-1, keepdims=True))
            pc = jnp.exp(sc - mx)
            den = jnp.sum(pc, axis=-1, keepdims=True) + jnp.exp(sink - mx)
            o = _dot(pc.astype(BF16), vc)
            if with_band:
                pb = jnp.exp(sb - mx)
                den = den + jnp.sum(pb, axis=-1, keepdims=True)
                o = o + _dot(pb.astype(BF16), vb)
            o = o / den
            outs.extend(o[r * ATTN_BLOCK:(r + 1) * ATTN_BLOCK] for r in range(rep))
        o_ref[0] = jnp.concatenate(outs, axis=-1).astype(BF16)

    @pl.when(n >= 0)
    def _():
        attend(True)

    @pl.when(n < 0)
    def _():
        attend(False)


def _attention(q, kv, sinks, n_batch, t_len):
    nqb = t_len // ATTN_BLOCK
    first = N_CTX // ATTN_BLOCK
    q3 = q.reshape(n_batch, t_len, D_Q)
    kv3 = kv.reshape(n_batch, t_len, 2 * D_KV)

    def band(off):
        return lambda b, j, s: (b, jnp.clip(j + off, first, nqb - 1), 0)

    rep = N_Q_HEADS // N_KV_HEADS
    iq = np.arange(rep * ATTN_BLOCK)[:, None] % ATTN_BLOCK
    ik = np.arange(3 * ATTN_BLOCK)[None, :]
    window_bias = jnp.asarray(np.where(np.abs(iq + ATTN_BLOCK - ik) <= WINDOW, 0.0, NEG), F32)

    out = pl.pallas_call(
        functools.partial(_attn_kernel, n_lat=t_len - N_CTX),
        grid_spec=pltpu.PrefetchScalarGridSpec(
            num_scalar_prefetch=1,
            grid=(n_batch, nqb),
            in_specs=[
                pl.BlockSpec((1, ATTN_BLOCK, D_Q), lambda b, j, s: (b, j, 0)),
                pl.BlockSpec((1, N_CTX, 2 * D_KV), lambda b, j, s: (b, 0, 0)),
                pl.BlockSpec((1, ATTN_BLOCK, 2 * D_KV), band(-1)),
                pl.BlockSpec((1, ATTN_BLOCK, 2 * D_KV), band(0)),
                pl.BlockSpec((1, ATTN_BLOCK, 2 * D_KV), band(1)),
                pl.BlockSpec((rep * ATTN_BLOCK, 3 * ATTN_BLOCK), lambda b, j, s: (0, 0)),
            ],
            out_specs=pl.BlockSpec((1, ATTN_BLOCK, D_Q), lambda b, j, s: (b, j, 0)),
        ),
        out_shape=jax.ShapeDtypeStruct((n_batch, t_len, D_Q), BF16),
        compiler_params=_cparams(("parallel", "parallel")),
        name="attention",
    )(sinks, q3, kv3, kv3, kv3, kv3, window_bias)
    return out.reshape(n_batch * t_len, D_Q)


def _gelu_tanh(x):
    return 0.5 * x * (1.0 + jnp.tanh(math.sqrt(2.0 / math.pi) * (x + 0.044715 * (x * x * x))))


def _merge_kernel(x_ref, yf_ref, yb_ref, u_ref, cz_ref, czp_ref, czn_ref, ya_ref, gt_ref, mod_ref,
                  d_ref, cw_ref, wglu_ref, wso_ref, wco_ref, wao_ref, wo_ref, n2g_ref,
                  wrh_ref, wrl_ref, br_ref, tri_ref, triu_ref, xo_ref, h2_ref, rt_ref, cnt_ref, lg_ref,
                  *, tiles_per_seq, n_tiles):
    @pl.when(pl.program_id(0) == 0)
    def _():
        lg_ref[...] = jnp.full(lg_ref.shape, NEG, F32)

    rt_ref[...], cnt_ref[...] = _route_tile(lg_ref[...], tri_ref, triu_ref)

    r = jnp.minimum(pl.program_id(0), n_tiles - 1) % tiles_per_seq
    ys = yf_ref[...] + yb_ref[...] + d_ref[...] * u_ref[...].astype(F32)
    z = _gelu_tanh(ys)
    glu = z * _sigmoid(_dot(z.astype(BF16), wglu_ref[...]))
    br_ssm = _dot(glu.astype(BF16), wso_ref[...])

    cz = cz_ref[...].astype(F32)
    cb = cz[:, 0:D_CONV]
    zz = cz[:, D_CONV:2 * D_CONV]
    seg_first = r <= 1
    seg_last = (r == 0) | (r == tiles_per_seq - 1)
    prev_row = jnp.where(seg_first, 0.0, czp_ref[7:8, D_CONV:2 * D_CONV].astype(F32))
    next_row = jnp.where(seg_last, 0.0, czn_ref[0:1, D_CONV:2 * D_CONV].astype(F32))
    row = lax.broadcasted_iota(I32, (TM, D_CONV), 0)
    z_dn = jnp.where(row == 0, prev_row, pltpu.roll(zz, 1, axis=0))
    z_up = jnp.where(row == TM - 1, next_row, pltpu.roll(zz, TM - 1, axis=0))
    y_conv = cb * (cw_ref[0:1, :] * z_dn + cw_ref[1:2, :] * zz + cw_ref[2:3, :] * z_up)
    br_conv = _dot(y_conv.astype(BF16), wco_ref[...])
    br_attn = _dot(ya_ref[...], wao_ref[...])

    merged = (gt_ref[:, 0:D_MODEL].astype(F32) * br_ssm
              + gt_ref[:, D_MODEL:2 * D_MODEL].astype(F32) * br_conv
              + gt_ref[:, 2 * D_MODEL:3 * D_MODEL].astype(F32) * br_attn)
    mix = _dot(merged.astype(BF16), wo_ref[...])
    m = mod_ref[0]
    xn = x_ref[...] + m[:, 2 * D_MODEL:3 * D_MODEL] * mix
    xo_ref[...] = xn
    h2 = _rms(xn, n2g_ref[...]) * (1.0 + m[:, 4 * D_MODEL:5 * D_MODEL]) + m[:, 3 * D_MODEL:4 * D_MODEL]
    hi = h2.astype(BF16)
    h2_ref[...] = hi
    lo = (h2 - hi.astype(F32)).astype(BF16)
    lg_ref[...] = _dot(hi, wrh_ref[...]) + _dot(lo, wrh_ref[...]) + _dot(hi, wrl_ref[...]) + br_ref[...]


def _merge(x, yf, yb, u_tb, cz, ya, gt, mod_tiles, d_skip, conv_w, wglu, wso, wco, wao, wo, n2g,
           wr_hi, wr_lo, b_r, tri, triu, tiles_per_seq):
    tok = x.shape[0]
    nt = tok // TM
    cur = lambda i: jnp.minimum(i, nt - 1)
    tile = lambda i: (cur(i), 0)
    routed = lambda i: (jnp.maximum(i - 1, 0), 0)
    const = lambda i: (0, 0)
    tb = lambda i: (cur(i) % tiles_per_seq, cur(i) // tiles_per_seq)
    rows8 = TM // 8
    return pl.pallas_call(
        functools.partial(_merge_kernel, tiles_per_seq=tiles_per_seq, n_tiles=nt),
        grid=(nt + 1,),
        in_specs=[
            pl.BlockSpec((TM, D_MODEL), tile),
            pl.BlockSpec((TM, D_SSM), tb),
            pl.BlockSpec((TM, D_SSM), tb),
            pl.BlockSpec((TM, D_SSM), tb),
            pl.BlockSpec((TM, 2 * D_CONV), tile),
            pl.BlockSpec((8, 2 * D_CONV), lambda i: (jnp.maximum(cur(i) * rows8 - 1, 0), 0)),
            pl.BlockSpec((8, 2 * D_CONV), lambda i: (jnp.minimum((cur(i) + 1) * rows8, tok // 8 - 1), 0)),
            pl.BlockSpec((TM, D_Q), tile),
            pl.BlockSpec((TM, N_BRANCH * D_MODEL), tile),
            pl.BlockSpec((1, 1, 6 * D_MODEL), lambda i: (cur(i), 0, 0)),
            pl.BlockSpec((1, D_SSM), const),
            pl.BlockSpec((3, D_CONV), const),
            pl.BlockSpec((D_SSM, D_SSM), const),
            pl.BlockSpec((D_SSM, D_MODEL), const),
            pl.BlockSpec((D_CONV, D_MODEL), const),
            pl.BlockSpec((D_Q, D_MODEL), const),
            pl.BlockSpec((D_MODEL, D_MODEL), const),
            pl.BlockSpec((1, D_MODEL), const),
            pl.BlockSpec((D_MODEL, LANES), const),
            pl.BlockSpec((D_MODEL, LANES), const),
            pl.BlockSpec((1, LANES), const),
            pl.BlockSpec((TM, TM), const),
            pl.BlockSpec((LANES, LANES), const),
        ],
        out_specs=[
            pl.BlockSpec((TM, D_MODEL), tile),
            pl.BlockSpec((TM, D_MODEL), tile),
            pl.BlockSpec((TM, LANES), routed),
            pl.BlockSpec((8, LANES), routed),
        ],
        out_shape=[
            jax.ShapeDtypeStruct((tok, D_MODEL), F32),
            jax.ShapeDtypeStruct((tok, D_MODEL), BF16),
            jax.ShapeDtypeStruct((tok, LANES), F32),
            jax.ShapeDtypeStruct((nt * 8, LANES), F32),
        ],
        scratch_shapes=[pltpu.VMEM((TM, LANES), F32)],
        compiler_params=_cparams(("arbitrary",)),
        name="branch_merge",
    )(x, yf, yb, u_tb, cz, cz, cz, ya, gt, mod_tiles, d_skip, conv_w, wglu, wso, wco, wao, wo, n2g,
      wr_hi, wr_lo, b_r, tri, triu)


def _route_tile(l, tri_ref, triu_ref):
    lane = lax.broadcasted_iota(I32, l.shape, 1)
    vals, idxs, hots = [], [], []
    for _ in range(TOP_K):
        mx = jnp.max(l, axis=-1, keepdims=True)
        idx = jnp.min(jnp.where(l == mx, lane, LANES), axis=-1, keepdims=True)
        hot = lane == idx
        l = jnp.where(hot, -3e38, l)
        vals.append(mx)
        idxs.append(idx)
        hots.append(hot)
    ex = [jnp.exp(v - vals[0]) for v in vals]
    den = ex[0] + ex[1] + ex[2] + ex[3]
    picked = jnp.zeros(l.shape, F32)
    for hot in hots:
        picked = picked + hot.astype(F32)
    cum = _dot(tri_ref[...], picked.astype(BF16))
    cnt = jnp.sum(picked, axis=0, keepdims=True)
    seg = jnp.floor((cnt + (SEG - 1.0)) * (1.0 / SEG)) * SEG
    run_start = _dot(jnp.broadcast_to(seg, (8, LANES)).astype(BF16), triu_ref[...])[0:1]
    pos = cum + run_start
    out = jnp.zeros(l.shape, F32)
    for k in range(TOP_K):
        row = jnp.sum(jnp.where(hots[k], pos, 0.0), axis=-1, keepdims=True)
        out = jnp.where(lane == k, idxs[k].astype(F32), out)
        out = jnp.where(lane == TOP_K + k, row, out)
        out = jnp.where(lane == 2 * TOP_K + k, ex[k] / den, out)
    return out, jnp.broadcast_to(cnt, (8, LANES))


MAX_PIECES = LOCAL_ROWS // SEG


def _for_each_piece(tile, grow_ref, tot_ref, fn):
    base = tile * MAX_PIECES

    def per_piece(p, c):
        fn(pl.multiple_of(p * SEG, SEG), pl.multiple_of(grow_ref[base + p], SEG))
        return c

    lax.fori_loop(0, tot_ref[tile], per_piece, 0)


WAIT_GROUP = 16


def _wait_pieces(n, descriptor):
    def many(g, c):
        descriptor(WAIT_GROUP * SEG).wait()
        return c

    def one(g, c):
        descriptor(SEG).wait()
        return c

    lax.fori_loop(0, lax.shift_right_logical(n, int(math.log2(WAIT_GROUP))), many, 0)
    lax.fori_loop(0, lax.bitwise_and(n, WAIT_GROUP - 1), one, 0)


ROW_CHUNK = 256


def _local_rows(rt, c):
    rows = c * ROW_CHUNK + lax.broadcasted_iota(I32, (TM, ROW_CHUNK), 1)
    return [rows == rt[:, TOP_K + k:TOP_K + k + 1].astype(I32) for k in range(TOP_K)]


def _dispatch_kernel(grow_ref, tot_ref, zstart_ref, nu_ref, h_ref, rt_ref, xs_hbm,
                     xl_ref, zero_ref, sem_ref, sem_z, *, n_blocks, n_tiles):
    i = pl.program_id(0)
    slot = i % 2

    @pl.when(i == 0)
    def _():
        zero_ref[...] = jnp.zeros_like(zero_ref)

        def zero_copy(start):
            rows = pl.ds(pl.multiple_of(start, MOE_BLOCK), MOE_BLOCK)
            return pltpu.make_async_copy(zero_ref, xs_hbm.at[rows, :], sem_z)

        def tail_start(b, c):
            zero_copy(b * MOE_BLOCK).start()
            return c

        def tail_wait(b, c):
            zero_copy(b * MOE_BLOCK).wait()
            return c

        for e in range(N_EXPERTS):
            @pl.when(zstart_ref[e] >= 0)
            def _():
                zero_copy(zstart_ref[e]).start()
        lax.fori_loop(nu_ref[0], n_blocks, tail_start, 0)
        for e in range(N_EXPERTS):
            @pl.when(zstart_ref[e] >= 0)
            def _():
                zero_copy(zstart_ref[e]).wait()
        lax.fori_loop(nu_ref[0], n_blocks, tail_wait, 0)

    rt = rt_ref[...]
    h = h_ref[...]
    for c in range(LOCAL_ROWS // ROW_CHUNK):
        place = jnp.zeros((TM, ROW_CHUNK), F32)
        for hit in _local_rows(rt, c):
            place = place + hit.astype(F32)
        xl_ref[slot, pl.ds(c * ROW_CHUNK, ROW_CHUNK), :] = lax.dot_general(
            place.astype(BF16), h, (((0,), (0,)), ((), ())), preferred_element_type=F32)

    def seg_copy(local_row, global_row, s, rows=SEG):
        return pltpu.make_async_copy(xl_ref.at[s, pl.ds(local_row, rows), :],
                                     xs_hbm.at[pl.ds(global_row, rows), :], sem_ref.at[s])

    def wait_tile(tile, s):
        _wait_pieces(tot_ref[tile], lambda rows: seg_copy(0, 0, s, rows))

    _for_each_piece(i, grow_ref, tot_ref, lambda lr, gr: seg_copy(lr, gr, slot).start())

    @pl.when(i > 0)
    def _():
        wait_tile(i - 1, 1 - slot)

    @pl.when(i == n_tiles - 1)
    def _():
        wait_tile(i, slot)


def _dispatch(h2, route, seg_meta, zstart, n_used, n_slots):
    tok = h2.shape[0]
    nt = tok // TM
    tile = lambda i, *_: (i, 0)
    return pl.pallas_call(
        functools.partial(_dispatch_kernel, n_blocks=n_slots // MOE_BLOCK, n_tiles=nt),
        grid_spec=pltpu.PrefetchScalarGridSpec(
            num_scalar_prefetch=4,
            grid=(nt,),
            in_specs=[
                pl.BlockSpec((TM, D_MODEL), tile),
                pl.BlockSpec((TM, LANES), tile),
            ],
            out_specs=pl.BlockSpec(memory_space=pl.ANY),
            scratch_shapes=[
                pltpu.VMEM((2, LOCAL_ROWS, D_MODEL), F32),
                pltpu.VMEM((MOE_BLOCK, D_MODEL), F32),
                pltpu.SemaphoreType.DMA((2,)),
                pltpu.SemaphoreType.DMA,
            ],
        ),
        out_shape=jax.ShapeDtypeStruct((n_slots, D_MODEL), F32),
        compiler_params=_cparams(("arbitrary",)),
        name="dispatch",
    )(*seg_meta, zstart, n_used, h2, route)


DEINT = 256


def _expert_kernel(be_ref, nu_ref, x_ref, wgu_ref, bg_ref, bu_ref, wd_ref, bd_ref, sel_ref, y_ref,
                   wg_s, wu_s, wd_s):
    i = pl.program_id(0)
    prev = be_ref[jnp.maximum(i - 1, 0)]

    @pl.when((i == 0) | (be_ref[i] != prev))
    def _():
        for c in range(2 * D_EXPERT // DEINT):
            w = wgu_ref[0, 0, :, c * DEINT:(c + 1) * DEINT].astype(BF16)
            cols = pl.ds(c * (DEINT // 2), DEINT // 2)
            split = _dot(w, sel_ref[...])
            wg_s[:, cols] = split[:, 0:DEINT // 2].astype(BF16)
            wu_s[:, cols] = split[:, DEINT // 2:DEINT].astype(BF16)
        wd_s[...] = wd_ref[0, 0].astype(BF16)

    @pl.when(i < nu_ref[0])
    def _():
        x = x_ref[...].astype(BF16)
        g = _dot(x, wg_s[...]) + bg_ref[0]
        u = _dot(x, wu_s[...]) + bu_ref[0]
        glu = jnp.minimum(g, SWIGLU_LIMIT)
        up = jnp.clip(u, -SWIGLU_LIMIT, SWIGLU_LIMIT)
        act = glu * _sigmoid(SWIGLU_ALPHA * glu) * (up + 1.0)
        y_ref[...] = _dot(act.astype(BF16), wd_s[...]) + bd_ref[0]

    @pl.when(i >= nu_ref[0])
    def _():
        y_ref[...] = jnp.zeros_like(y_ref)


def _experts(xs, block_expert, n_used, layer, wgu_all, bg, bu, wd_all, bd, sel):
    n_slots = xs.shape[0]
    n_blocks = n_slots // MOE_BLOCK
    wmap = lambda i, be, nu: (be[i], 0, 0)
    lwmap = lambda i, be, nu: (layer, be[i], 0, 0)
    const = lambda i, be, nu: (0, 0)
    return pl.pallas_call(
        _expert_kernel,
        grid_spec=pltpu.PrefetchScalarGridSpec(
            num_scalar_prefetch=2,
            grid=(n_blocks,),
            in_specs=[
                pl.BlockSpec((MOE_BLOCK, D_MODEL), lambda i, be, nu: (jnp.minimum(i, nu[0] - 1), 0)),
                pl.BlockSpec((1, 1, D_MODEL, 2 * D_EXPERT), lwmap),
                pl.BlockSpec((1, 1, D_EXPERT), wmap),
                pl.BlockSpec((1, 1, D_EXPERT), wmap),
                pl.BlockSpec((1, 1, D_EXPERT, D_MODEL), lwmap),
                pl.BlockSpec((1, 1, D_MODEL), wmap),
                pl.BlockSpec((DEINT, DEINT), const),
            ],
            out_specs=pl.BlockSpec((MOE_BLOCK, D_MODEL), lambda i, be, nu: (i, 0)),
            scratch_shapes=[
                pltpu.VMEM((D_MODEL, D_EXPERT), BF16),
                pltpu.VMEM((D_MODEL, D_EXPERT), BF16),
                pltpu.VMEM((D_EXPERT, D_MODEL), BF16),
            ],
        ),
        out_shape=jax.ShapeDtypeStruct((n_slots, D_MODEL), F32),
        compiler_params=_cparams(("arbitrary",)),
        name="experts",
    )(block_expert, n_used, xs, wgu_all, bg, bu, wd_all, bd, sel)


def _combine_kernel(grow_ref, tot_ref, ys_hbm, x_ref, rt_ref, mod_ref, o_ref, yl_ref, sem_ref,
                    *, n_tiles):
    i = pl.program_id(0)
    slot = i % 2

    def seg_copy(local_row, global_row, s, rows=SEG):
        return pltpu.make_async_copy(ys_hbm.at[pl.ds(global_row, rows), :],
                                     yl_ref.at[s, pl.ds(local_row, rows), :], sem_ref.at[s])

    def fetch_tile(tile, s):
        _for_each_piece(tile, grow_ref, tot_ref, lambda lr, gr: seg_copy(lr, gr, s).start())

    @pl.when(i == 0)
    def _():
        yl_ref[...] = jnp.zeros_like(yl_ref)
        fetch_tile(0, 0)

    @pl.when(i + 1 < n_tiles)
    def _():
        fetch_tile(i + 1, 1 - slot)

    _wait_pieces(tot_ref[i], lambda rows: seg_copy(0, 0, slot, rows))

    rt = rt_ref[...]
    y = jnp.zeros((TM, D_MODEL), F32)
    for c in range(LOCAL_ROWS // ROW_CHUNK):
        weights = jnp.zeros((TM, ROW_CHUNK), F32)
        for k, hit in enumerate(_local_rows(rt, c)):
            weights = weights + jnp.where(hit, rt[:, 2 * TOP_K + k:2 * TOP_K + k + 1], 0.0)
        w_hi = weights.astype(BF16)
        w_lo = (weights - w_hi.astype(F32)).astype(BF16)
        yl = yl_ref[slot, pl.ds(c * ROW_CHUNK, ROW_CHUNK), :].astype(BF16)
        y = y + _dot(w_hi, yl) + _dot(w_lo, yl)
    o_ref[...] = (x_ref[...] + mod_ref[0][:, 5 * D_MODEL:6 * D_MODEL] * y).reshape(o_ref.shape)


def _combine(x, ys, route, seg_meta, mod_tiles, out_seq=None):
    tok = x.shape[0]
    nt = tok // TM
    tile = lambda i, *_: (i, 0)
    if out_seq is None:
        out_spec = pl.BlockSpec((TM, D_MODEL), tile)
        out_shape = jax.ShapeDtypeStruct((tok, D_MODEL), F32)
    else:
        n_batch, tps = out_seq
        out_spec = pl.BlockSpec((1, TM, D_MODEL), lambda i, *_: (i // tps, jnp.maximum(i % tps - 1, 0), 0))
        out_shape = jax.ShapeDtypeStruct((n_batch, (tps - 1) * TM, D_MODEL), F32)
    return pl.pallas_call(
        functools.partial(_combine_kernel, n_tiles=nt),
        grid_spec=pltpu.PrefetchScalarGridSpec(
            num_scalar_prefetch=2,
            grid=(nt,),
            in_specs=[
                pl.BlockSpec(memory_space=pl.ANY),
                pl.BlockSpec((TM, D_MODEL), tile),
                pl.BlockSpec((TM, LANES), tile),
                pl.BlockSpec((1, 1, 6 * D_MODEL), lambda i, *_: (i, 0, 0)),
            ],
            out_specs=out_spec,
            scratch_shapes=[
                pltpu.VMEM((2, LOCAL_ROWS, D_MODEL), F32),
                pltpu.SemaphoreType.DMA((2,)),
            ],
        ),
        out_shape=out_shape,
        compiler_params=_cparams(("arbitrary",)),
        name="combine",
    )(*seg_meta, ys, x, route, mod_tiles)


def _rope_tables(t_len):
    n_lat = t_len - N_CTX
    t = np.arange(n_lat)
    n_pairs = HEAD_DIM // 4
    inv_freq = jnp.asarray(ROPE_BASE, F32) ** (-jnp.arange(n_pairs, dtype=F32) / n_pairs)
    row = jnp.asarray(t // GRID_W, F32)
    col = jnp.asarray(t % GRID_W, F32)
    ang = jnp.concatenate([row[:, None] * inv_freq, col[:, None] * inv_freq], axis=-1)
    ang = jnp.concatenate([jnp.zeros((N_CTX, HEAD_DIM // 2), F32), ang], axis=0)
    cos = jnp.tile(jnp.cos(ang), (1, LANES // (HEAD_DIM // 2)))
    sin = jnp.tile(jnp.sin(ang), (1, LANES // (HEAD_DIM // 2)))
    return cos, sin


def _s5_operands(lam_re, lam_im, log_dt, b_re, b_im, c_re, c_im):
    lam = lax.complex(lam_re, lam_im)
    dt = jnp.exp(log_dt)[..., None]
    lam_bar = jnp.exp(lam * dt)
    b_bar = ((lam_bar - 1.0) / lam)[..., None] * lax.complex(b_re, b_im)
    eye = jnp.eye(N_SSM_GROUPS, dtype=F32)

    def in_block(w):
        return jnp.einsum("zgph,gk->zghkp", w, eye).reshape(2, D_SSM, N_STATE)

    def out_block(w):
        return jnp.einsum("zghp,gk->zgpkh", w, eye).reshape(2, N_STATE, D_SSM)

    bd = jnp.concatenate([in_block(b_bar.real), in_block(b_bar.imag)], axis=-1).astype(BF16)
    cd = jnp.concatenate([out_block(c_re), out_block(-c_im)], axis=1).astype(BF16)
    lam_rows = jnp.stack([lam_bar[0].real.reshape(-1), lam_bar[0].imag.reshape(-1),
                          lam_bar[1].real.reshape(-1), lam_bar[1].imag.reshape(-1)])
    return bd, cd, lam_rows


def _moe_constants():
    tri = jnp.asarray(np.tril(np.ones((TM, TM)), -1), BF16)
    triu = jnp.asarray(np.triu(np.ones((LANES, LANES)), 1), BF16)
    pick = np.arange(DEINT)[:, None] == 2 * np.arange(DEINT // 2)[None, :]
    sel = jnp.asarray(np.concatenate([pick, np.roll(pick, 1, axis=0)], axis=1), BF16)
    return tri, triu, sel


def _moe(xs, h2, route, counts, mod_tiles, layer, w_gate_up, b_gate_up, w_down, b_down, consts, out_seq=None):
    sel = consts[2]
    tok = xs.shape[0]
    nt = tok // TM
    n_blocks = (tok * TOP_K + nt * N_EXPERTS * (SEG - 1) + N_EXPERTS * (MOE_BLOCK - 1)) // MOE_BLOCK
    n_slots = n_blocks * MOE_BLOCK

    counts = counts.reshape(nt, 8, LANES)[:, 0, :N_EXPERTS].astype(I32)
    seg = (counts + SEG - 1) // SEG * SEG
    region = jnp.sum(seg, axis=0)
    padded = (region + MOE_BLOCK - 1) // MOE_BLOCK * MOE_BLOCK
    pad_end = jnp.cumsum(padded)
    pad_start = pad_end - padded
    run_global = pad_start[None, :] + jnp.cumsum(seg, axis=0) - seg
    run_local = jnp.cumsum(seg, axis=1) - seg
    pieces = seg // SEG
    ends = jnp.cumsum(pieces, axis=1)
    shift = run_global - run_local
    jump = shift - jnp.concatenate([jnp.zeros((nt, 1), I32), shift[:, :-1]], axis=1)
    piece = jnp.arange(MAX_PIECES, dtype=I32)
    started = piece[None, None, :] >= (ends - pieces)[:, :, None]
    piece_global = piece[None, :] * SEG + jnp.sum(jnp.where(started, jump[:, :, None], 0), axis=1)
    seg_meta = (piece_global.reshape(-1).astype(I32), ends[:, -1].astype(I32))
    n_used = (pad_end[-1] // MOE_BLOCK).astype(I32)
    blk = jnp.minimum(jnp.arange(n_blocks, dtype=I32), n_used - 1) * MOE_BLOCK
    block_expert = jnp.minimum(jnp.sum(blk[:, None] >= pad_end[None, :], axis=1), N_EXPERTS - 1).astype(I32)
    zstart = jnp.where(region > 0, pad_end - MOE_BLOCK, -1).astype(I32)

    x_sorted = _dispatch(h2, route, seg_meta, zstart, n_used.reshape(1), n_slots)
    y_sorted = _experts(x_sorted, block_expert, n_used.reshape(1), layer, w_gate_up,
                        b_gate_up[:, None, 0::2], b_gate_up[:, None, 1::2],
                        w_down, b_down[:, None, :], sel)
    return _combine(xs, y_sorted, route, seg_meta, mod_tiles, out_seq)


def kernel(x, c, ctx, c_ctx, w_mod, b_mod, norm1_g, norm2_g, w_in, ssm_lam_re, ssm_lam_im, ssm_log_dt, ssm_b_re, ssm_b_im, ssm_c_re, ssm_c_im, ssm_d, w_glu, w_ssm_out, conv_w, w_conv_out, q_norm_g, k_norm_g, attn_sinks, w_attn_out, w_o, w_router, b_router, w_gate_up, b_gate_up, w_down, b_down):
    n_batch, n_lat, _ = x.shape
    t_len = N_CTX + n_lat
    tiles_per_seq = t_len // TM
    tok = n_batch * t_len
    nt = tok // TM

    xs = _assemble(ctx, x, tiles_per_seq)

    mod_rows = 8 * ((n_batch + 1 + 7) // 8)
    cvec = jnp.zeros((mod_rows, D_MODEL), F32).at[:n_batch].set(c).at[n_batch].set(c_ctx)
    mod_all = _modulation(cvec, w_mod, b_mod)
    tile_ids = np.arange(nt)
    tile_row = np.where(tile_ids % tiles_per_seq == 0, n_batch, tile_ids // tiles_per_seq)

    cos_t, sin_t = _rope_tables(t_len)
    head_sum = jnp.asarray(np.kron(np.eye(N_Q_HEADS), np.full((HEAD_DIM, HEAD_DIM), 1.0 / HEAD_DIM)), BF16)
    moe_consts = _moe_constants()

    for l in range(DEPTH):
        mod_tiles = mod_all[l][tile_row].reshape(nt, 1, 6 * D_MODEL)
        qg = jnp.tile(q_norm_g[l], N_Q_HEADS).reshape(1, D_Q)
        kg = jnp.tile(k_norm_g[l], N_KV_HEADS).reshape(1, D_KV)
        u_tb, cz, q, kv, gt = _in_proj(xs, mod_tiles, norm1_g[l].reshape(1, D_MODEL), w_in[l].astype(BF16),
                                       cos_t, sin_t, qg, kg, head_sum, n_batch, tiles_per_seq)

        bd, cd, lam_rows = _s5_operands(ssm_lam_re[l], ssm_lam_im[l], ssm_log_dt[l], ssm_b_re[l], ssm_b_im[l],
                                        ssm_c_re[l], ssm_c_im[l])
        yf, yb = _s5(u_tb, bd, cd, lam_rows, n_batch, t_len)
        ya = _attention(q, kv, attn_sinks[l], n_batch, t_len)

        wr = jnp.zeros((D_MODEL, LANES), F32).at[:, :N_EXPERTS].set(w_router[l])
        wr_hi = wr.astype(BF16)
        wr_lo = (wr - wr_hi.astype(F32)).astype(BF16)
        b_r = jnp.full((1, LANES), NEG, F32).at[0, :N_EXPERTS].set(b_router[l])
        xs, h2, route, counts = _merge(
            xs, yf, yb, u_tb,
            cz, ya, gt, mod_tiles, ssm_d[l].reshape(1, D_SSM), conv_w[l],
            w_glu[l].astype(BF16), w_ssm_out[l].astype(BF16), w_conv_out[l].astype(BF16),
            w_attn_out[l].astype(BF16), w_o[l].astype(BF16), norm2_g[l].reshape(1, D_MODEL),
            wr_hi, wr_lo, b_r, moe_consts[0], moe_consts[1], tiles_per_seq)

        out_seq = None
        if l == DEPTH - 1:
            out_seq = (n_batch, tiles_per_seq)
            ctx_tile = jnp.asarray(tile_ids % tiles_per_seq == 0)
            counts = jnp.where(jnp.repeat(ctx_tile, 8)[:, None], 0.0, counts)
            lane = jnp.arange(LANES)[None, :]
            row_lanes = (lane >= TOP_K) & (lane < 2 * TOP_K)
            route = jnp.where(jnp.repeat(ctx_tile, TM)[:, None] & row_lanes, float(LOCAL_ROWS), route)
        xs = _moe(xs, h2, route, counts, mod_tiles, l, w_gate_up, b_gate_up[l], w_down, b_down[l], moe_consts,
                  out_seq)

    return xs
```

```python
import functools
import math

import jax
import jax.numpy as jnp
import numpy as np
from jax import lax
from jax.experimental import pallas as pl
from jax.experimental.pallas import tpu as pltpu

F32 = jnp.float32
BF16 = jnp.bfloat16
I32 = jnp.int32

D_MODEL = 1024
DEPTH = 4
N_CTX = 256
HEAD_DIM = 64
N_Q_HEADS = 8
N_KV_HEADS = 2
D_Q = N_Q_HEADS * HEAD_DIM
D_KV = N_KV_HEADS * HEAD_DIM
WINDOW = 128
ATTN_BLOCK = 128
ROPE_BASE = 10000.0
GRID_W = 64
D_SSM = 256
SSM_GROUP = 16
N_SSM_GROUPS = 16
SSM_STATE = 64
N_STATE = N_SSM_GROUPS * SSM_STATE
D_CONV = 256
N_BRANCH = 3
D_IN = D_SSM + 3 * D_CONV + D_Q + 2 * D_KV + N_BRANCH * D_MODEL
N_EXPERTS = 32
TOP_K = 4
D_EXPERT = 1024
SWIGLU_LIMIT = 7.0
SWIGLU_ALPHA = 1.702
EPS = 1e-6

LANES = 128
TM = 256
MOE_BLOCK = 512
SEG = 8
LOCAL_ROWS = -(-(TM * TOP_K + N_EXPERTS * (SEG - 1)) // LANES) * LANES
S5_CHUNK = 64
S5_COLS = 512
NEG = -1e30
VMEM_LIMIT = 56 * 1024 * 1024


def _cparams(sem):
    return pltpu.CompilerParams(dimension_semantics=sem, vmem_limit_bytes=VMEM_LIMIT)


def _dot(a, b):
    return jnp.dot(a, b, preferred_element_type=F32)


def _sigmoid(x):
    return 1.0 / (1.0 + jnp.exp(-x))


def _rms(x, g):
    ms = jnp.mean(x * x, axis=-1, keepdims=True)
    return x * lax.rsqrt(ms + EPS) * g


def _mod_kernel(c_ref, w_ref, b_ref, o_ref):
    c = c_ref[...]
    s = (c * _sigmoid(c)).astype(BF16)
    o_ref[0] = _dot(s, w_ref[0].astype(BF16)) + b_ref[0]


def _modulation(cvec, w_mod, b_mod):
    rows = cvec.shape[0]
    nblk = 1536
    return pl.pallas_call(
        _mod_kernel,
        grid=(DEPTH, 6 * D_MODEL // nblk),
        in_specs=[
            pl.BlockSpec((rows, D_MODEL), lambda l, j: (0, 0)),
            pl.BlockSpec((1, D_MODEL, nblk), lambda l, j: (l, 0, j)),
            pl.BlockSpec((1, 1, nblk), lambda l, j: (l, 0, j)),
        ],
        out_specs=pl.BlockSpec((1, rows, nblk), lambda l, j: (l, 0, j)),
        out_shape=jax.ShapeDtypeStruct((DEPTH, rows, 6 * D_MODEL), F32),
        compiler_params=_cparams(("arbitrary", "arbitrary")),
        name="modulation",
    )(cvec, w_mod, b_mod.reshape(DEPTH, 1, 6 * D_MODEL))


def _assemble_kernel(c_ref, x_ref, o_ref, *, tiles_per_seq):
    r = pl.program_id(0) % tiles_per_seq

    @pl.when(r == 0)
    def _():
        o_ref[...] = c_ref[0]

    @pl.when(r > 0)
    def _():
        o_ref[...] = x_ref[0]


def _assemble(ctx, x, tiles_per_seq):
    n_batch = x.shape[0]
    nt = n_batch * tiles_per_seq
    return pl.pallas_call(
        functools.partial(_assemble_kernel, tiles_per_seq=tiles_per_seq),
        grid=(nt,),
        in_specs=[
            pl.BlockSpec((1, TM, D_MODEL), lambda i: (i // tiles_per_seq, 0, 0)),
            pl.BlockSpec((1, TM, D_MODEL), lambda i: (i // tiles_per_seq, jnp.maximum(i % tiles_per_seq - 1, 0), 0)),
        ],
        out_specs=pl.BlockSpec((TM, D_MODEL), lambda i: (i, 0)),
        out_shape=jax.ShapeDtypeStruct((nt * TM, D_MODEL), F32),
        compiler_params=_cparams(("arbitrary",)),
        name="assemble",
    )(ctx, x)


def _rot_half(x, width):
    lane = lax.broadcasted_iota(I32, x.shape, 1)
    first = (lane % HEAD_DIM) < (HEAD_DIM // 2)
    return jnp.where(first, -pltpu.roll(x, width - HEAD_DIM // 2, axis=1), pltpu.roll(x, HEAD_DIM // 2, axis=1))


def _inproj_kernel(x_ref, mod_ref, g_ref, w_ref, cos_ref, sin_ref, qg_ref, kg_ref, hs_ref,
                   u_ref, cz_ref, q_ref, kv_ref, gt_ref):
    m = mod_ref[0]
    h = _rms(x_ref[...], g_ref[...])
    h = (h * (1.0 + m[:, D_MODEL:2 * D_MODEL]) + m[:, 0:D_MODEL]).astype(BF16)
    o_gate = D_SSM + 3 * D_CONV + D_Q + 2 * D_KV
    y = _dot(h, w_ref[:, 0:o_gate])
    u_ref[...] = y[:, 0:D_SSM].astype(BF16)
    cb = y[:, D_SSM:D_SSM + D_CONV]
    cc = y[:, D_SSM + D_CONV:D_SSM + 2 * D_CONV]
    cx = y[:, D_SSM + 2 * D_CONV:D_SSM + 3 * D_CONV]
    cz_ref[...] = jnp.concatenate([cb, cc * cx], axis=-1).astype(BF16)
    o_q = D_SSM + 3 * D_CONV
    q = y[:, o_q:o_q + D_Q]
    k = y[:, o_q + D_Q:o_q + D_Q + D_KV]
    v = y[:, o_q + D_Q + D_KV:o_gate]
    cos = cos_ref[...]
    sin = sin_ref[...]
    q_ms = _dot((q * q).astype(BF16), hs_ref[...])
    qn = q * lax.rsqrt(q_ms + EPS) * qg_ref[...]
    cos_q = jnp.concatenate([cos] * (D_Q // LANES), axis=-1)
    sin_q = jnp.concatenate([sin] * (D_Q // LANES), axis=-1)
    qr = qn * cos_q + _rot_half(qn, D_Q) * sin_q
    q_ref[...] = (qr * (HEAD_DIM ** -0.5)).astype(BF16)
    k_ms = _dot((k * k).astype(BF16), hs_ref[0:D_KV, 0:D_KV])
    kn = k * lax.rsqrt(k_ms + EPS) * kg_ref[...]
    kr = kn * cos + _rot_half(kn, D_KV) * sin
    kv_ref[...] = jnp.concatenate([kr, v], axis=-1).astype(BF16)
    gt = _dot(h, w_ref[:, o_gate:D_IN])
    gt_ref[...] = _sigmoid(gt).astype(BF16)


def _in_proj(x, mod_tiles, norm_g, w_in_bf, cos_t, sin_t, qg, kg, head_sum, n_batch, tiles_per_seq):
    tok = x.shape[0]
    nt = tok // TM
    t_len = tiles_per_seq * TM
    tile = lambda i: (i, 0)
    const = lambda i: (0, 0)
    seq_tile = lambda i: (i % tiles_per_seq, 0)
    return pl.pallas_call(
        _inproj_kernel,
        grid=(nt,),
        in_specs=[
            pl.BlockSpec((TM, D_MODEL), tile),
            pl.BlockSpec((1, 1, 6 * D_MODEL), lambda i: (i, 0, 0)),
            pl.BlockSpec((1, D_MODEL), const),
            pl.BlockSpec((D_MODEL, D_IN), const),
            pl.BlockSpec((TM, LANES), seq_tile),
            pl.BlockSpec((TM, LANES), seq_tile),
            pl.BlockSpec((1, D_Q), const),
            pl.BlockSpec((1, D_KV), const),
            pl.BlockSpec((D_Q, D_Q), const),
        ],
        out_specs=[
            pl.BlockSpec((TM, D_SSM), lambda i: (i % tiles_per_seq, i // tiles_per_seq)),
            pl.BlockSpec((TM, 2 * D_CONV), tile),
            pl.BlockSpec((TM, D_Q), tile),
            pl.BlockSpec((TM, 2 * D_KV), tile),
            pl.BlockSpec((TM, N_BRANCH * D_MODEL), tile),
        ],
        out_shape=[
            jax.ShapeDtypeStruct((t_len, n_batch * D_SSM), BF16),
            jax.ShapeDtypeStruct((tok, 2 * D_CONV), BF16),
            jax.ShapeDtypeStruct((tok, D_Q), BF16),
            jax.ShapeDtypeStruct((tok, 2 * D_KV), BF16),
            jax.ShapeDtypeStruct((tok, N_BRANCH * D_MODEL), BF16),
        ],
        compiler_params=_cparams(("parallel",)),
        name="in_proj",
    )(x, mod_tiles, norm_g, w_in_bf, cos_t, sin_t, qg, kg, head_sum)


def _s5_kernel(uf_ref, ub_ref, bd_ref, cd_ref, lam_ref, yf_ref, yb_ref, sf_ref, sb_ref, in_ref, out_ref, carry_ref,
               *, n_batch):
    i = pl.program_id(0)

    @pl.when(i == 0)
    def _():
        carry_ref[...] = jnp.zeros_like(carry_ref)

    halves = D_SSM // LANES

    def batch_rows(b):
        return pl.ds(b, S5_CHUNK, stride=n_batch)

    def project_in(z, u_ref, s_ref):
        for b in range(n_batch):
            for h in range(halves):
                in_ref[z, h, batch_rows(b), :] = u_ref[:, pl.ds(b * D_SSM + h * LANES, LANES)].astype(F32)
        u_rows = jnp.concatenate([in_ref[z, h] for h in range(halves)], axis=-1)
        s_ref[...] = _dot(u_rows.astype(BF16), bd_ref[z])

    def recur(z, s_ref):
        for j in range(N_STATE // S5_COLS):
            re_cols = pl.ds(j * S5_COLS, S5_COLS)
            im_cols = pl.ds(N_STATE + j * S5_COLS, S5_COLS)
            lr = jnp.broadcast_to(lam_ref[2 * z:2 * z + 1, re_cols], (n_batch, S5_COLS))
            li = jnp.broadcast_to(lam_ref[2 * z + 1:2 * z + 2, re_cols], (n_batch, S5_COLS))
            xr = carry_ref[2 * z, :, re_cols]
            xi = carry_ref[2 * z + 1, :, re_cols]
            for s in range(S5_CHUNK):
                t = s if z == 0 else S5_CHUNK - 1 - s
                rows = pl.ds(t * n_batch, n_batch)
                xr, xi = (lr * xr - li * xi + s_ref[rows, re_cols],
                          lr * xi + li * xr + s_ref[rows, im_cols])
                s_ref[rows, re_cols] = xr
                s_ref[rows, im_cols] = xi
            carry_ref[2 * z, :, re_cols] = xr
            carry_ref[2 * z + 1, :, re_cols] = xi

    def project_out(z, s_ref, y_ref):
        y_rows = _dot(s_ref[...].astype(BF16), cd_ref[z])
        for h in range(halves):
            out_ref[z, h] = y_rows[:, h * LANES:(h + 1) * LANES]
        for b in range(n_batch):
            for h in range(halves):
                y_ref[:, pl.ds(b * D_SSM + h * LANES, LANES)] = out_ref[z, h, batch_rows(b), :]

    project_in(0, uf_ref, sf_ref)
    project_in(1, ub_ref, sb_ref)
    recur(0, sf_ref)
    recur(1, sb_ref)
    project_out(0, sf_ref, yf_ref)
    project_out(1, sb_ref, yb_ref)


def _s5(u_tb, bd, cd, lam, n_batch, t_len):
    rows = S5_CHUNK * n_batch
    width = n_batch * D_SSM
    n_chunks = t_len // S5_CHUNK
    ctx_chunks = N_CTX // S5_CHUNK

    def bwd_block(i):
        return (jnp.where(i < ctx_chunks, ctx_chunks - 1 - i, n_chunks - 1 + ctx_chunks - i), 0)

    return pl.pallas_call(
        functools.partial(_s5_kernel, n_batch=n_batch),
        grid=(n_chunks,),
        in_specs=[
            pl.BlockSpec((S5_CHUNK, width), lambda i: (i, 0)),
            pl.BlockSpec((S5_CHUNK, width), bwd_block),
            pl.BlockSpec((2, D_SSM, 2 * N_STATE), lambda i: (0, 0, 0)),
            pl.BlockSpec((2, 2 * N_STATE, D_SSM), lambda i: (0, 0, 0)),
            pl.BlockSpec((4, N_STATE), lambda i: (0, 0)),
        ],
        out_specs=[
            pl.BlockSpec((S5_CHUNK, width), lambda i: (i, 0)),
            pl.BlockSpec((S5_CHUNK, width), bwd_block),
        ],
        out_shape=[jax.ShapeDtypeStruct((t_len, width), F32)] * 2,
        scratch_shapes=[
            pltpu.VMEM((rows, 2 * N_STATE), F32),
            pltpu.VMEM((rows, 2 * N_STATE), F32),
            pltpu.VMEM((2, D_SSM // LANES, rows, LANES), F32),
            pltpu.VMEM((2, D_SSM // LANES, rows, LANES), F32),
            pltpu.VMEM((4, n_batch, N_STATE), F32),
        ],
        compiler_params=_cparams(("arbitrary",)),
        name="s5_scan",
    )(u_tb, u_tb, bd, cd, lam)


def _attn_kernel(sink_ref, q_ref, kvc_ref, kv0_ref, kv1_ref, kv2_ref, wb_ref, o_ref, *, n_lat):
    n = pl.program_id(1) - N_CTX // ATTN_BLOCK
    rep = N_Q_HEADS // N_KV_HEADS
    rows = rep * ATTN_BLOCK
    head_of_row = lax.broadcasted_iota(I32, (rows, 1), 0) // ATTN_BLOCK
    contract_last = (((1,), (1,)), ((), ()))

    def attend(with_band):
        q = q_ref[0]
        kvc = kvc_ref[0]
        if with_band:
            band = jnp.concatenate([kv0_ref[0], kv1_ref[0], kv2_ref[0]], axis=0)
            key_block = n - 1 + lax.broadcasted_iota(I32, (1, 3 * ATTN_BLOCK), 1) // ATTN_BLOCK
            block_ok = (key_block >= 0) & (key_block < n_lat // ATTN_BLOCK)
            bias = wb_ref[...] + jnp.where(block_ok, 0.0, NEG)
        outs = []
        for g in range(N_KV_HEADS):
            qg = jnp.concatenate([q[:, (g * rep + r) * HEAD_DIM:(g * rep + r + 1) * HEAD_DIM] for r in range(rep)],
                                 axis=0)
            sink = jnp.zeros((rows, 1), F32)
            for r in range(rep):
                sink = jnp.where(head_of_row == r, sink_ref[g * rep + r], sink)
            kc = kvc[:, g * HEAD_DIM:(g + 1) * HEAD_DIM]
            vc = kvc[:, D_KV + g * HEAD_DIM:D_KV + (g + 1) * HEAD_DIM]
            sc = lax.dot_general(qg, kc, contract_last, preferred_element_type=F32)
            mx = jnp.maximum(jnp.max(sc, axis=-1, keepdims=True), sink)
            if with_band:
                kb = band[:, g * HEAD_DIM:(g + 1) * HEAD_DIM]
                vb = band[:, D_KV + g * HEAD_DIM:D_KV + (g + 1) * HEAD_DIM]
                sb = lax.dot_general(qg, kb, contract_last, preferred_element_type=F32) + bias
                mx = jnp.maximum(mx, jnp.max(sb, axis=-1, keepdims=True))
            pc = jnp.exp(sc - mx)
            den = jnp.sum(pc, axis=-1, keepdims=True) + jnp.exp(sink - mx)
            o = _dot(pc.astype(BF16), vc)
            if with_band:
                pb = jnp.exp(sb - mx)
                den = den + jnp.sum(pb, axis=-1, keepdims=True)
                o = o + _dot(pb.astype(BF16), vb)
            o = o / den
            outs.extend(o[r * ATTN_BLOCK:(r + 1) * ATTN_BLOCK] for r in range(rep))
        o_ref[0] = jnp.concatenate(outs, axis=-1).astype(BF16)

    @pl.when(n >= 0)
    def _():
        attend(True)

    @pl.when(n < 0)
    def _():
        attend(False)


def _attention(q, kv, sinks, n_batch, t_len):
    nqb = t_len // ATTN_BLOCK
    first = N_CTX // ATTN_BLOCK
    q3 = q.reshape(n_batch, t_len, D_Q)
    kv3 = kv.reshape(n_batch, t_len, 2 * D_KV)

    def band(off):
        return lambda b, j, s: (b, jnp.clip(j + off, first, nqb - 1), 0)

    rep = N_Q_HEADS // N_KV_HEADS
    iq = np.arange(rep * ATTN_BLOCK)[:, None] % ATTN_BLOCK
    ik = np.arange(3 * ATTN_BLOCK)[None, :]
    window_bias = jnp.asarray(np.where(np.abs(iq + ATTN_BLOCK - ik) <= WINDOW, 0.0, NEG), F32)

    out = pl.pallas_call(
        functools.partial(_attn_kernel, n_lat=t_len - N_CTX),
        grid_spec=pltpu.PrefetchScalarGridSpec(
            num_scalar_prefetch=1,
            grid=(n_batch, nqb),
            in_specs=[
                pl.BlockSpec((1, ATTN_BLOCK, D_Q), lambda b, j, s: (b, j, 0)),
                pl.BlockSpec((1, N_CTX, 2 * D_KV), lambda b, j, s: (b, 0, 0)),
                pl.BlockSpec((1, ATTN_BLOCK, 2 * D_KV), band(-1)),
                pl.BlockSpec((1, ATTN_BLOCK, 2 * D_KV), band(0)),
                pl.BlockSpec((1, ATTN_BLOCK, 2 * D_KV), band(1)),
                pl.BlockSpec((rep * ATTN_BLOCK, 3 * ATTN_BLOCK), lambda b, j, s: (0, 0)),
            ],
            out_specs=pl.BlockSpec((1, ATTN_BLOCK, D_Q), lambda b, j, s: (b, j, 0)),
        ),
        out_shape=jax.ShapeDtypeStruct((n_batch, t_len, D_Q), BF16),
        compiler_params=_cparams(("parallel", "parallel")),
        name="attention",
    )(sinks, q3, kv3, kv3, kv3, kv3, window_bias)
    return out.reshape(n_batch * t_len, D_Q)


def _gelu_tanh(x):
    return 0.5 * x * (1.0 + jnp.tanh(math.sqrt(2.0 / math.pi) * (x + 0.044715 * (x * x * x))))


def _merge_kernel(x_ref, yf_ref, yb_ref, u_ref, cz_ref, czp_ref, czn_ref, ya_ref, gt_ref, mod_ref,
                  d_ref, cw_ref, wglu_ref, wso_ref, wco_ref, wao_ref, wo_ref, n2g_ref,
                  wrh_ref, wrl_ref, br_ref, tri_ref, triu_ref, xo_ref, h2_ref, rt_ref, cnt_ref, lg_ref,
                  *, tiles_per_seq, n_tiles):
    @pl.when(pl.program_id(0) == 0)
    def _():
        lg_ref[...] = jnp.full(lg_ref.shape, NEG, F32)

    rt_ref[...], cnt_ref[...] = _route_tile(lg_ref[...], tri_ref, triu_ref)

    r = jnp.minimum(pl.program_id(0), n_tiles - 1) % tiles_per_seq
    ys = yf_ref[...] + yb_ref[...] + d_ref[...] * u_ref[...].astype(F32)
    z = _gelu_tanh(ys)
    glu = z * _sigmoid(_dot(z.astype(BF16), wglu_ref[...]))
    br_ssm = _dot(glu.astype(BF16), wso_ref[...])

    cz = cz_ref[...].astype(F32)
    cb = cz[:, 0:D_CONV]
    zz = cz[:, D_CONV:2 * D_CONV]
    seg_first = r <= 1
    seg_last = (r == 0) | (r == tiles_per_seq - 1)
    prev_row = jnp.where(seg_first, 0.0, czp_ref[7:8, D_CONV:2 * D_CONV].astype(F32))
    next_row = jnp.where(seg_last, 0.0, czn_ref[0:1, D_CONV:2 * D_CONV].astype(F32))
    row = lax.broadcasted_iota(I32, (TM, D_CONV), 0)
    z_dn = jnp.where(row == 0, prev_row, pltpu.roll(zz, 1, axis=0))
    z_up = jnp.where(row == TM - 1, next_row, pltpu.roll(zz, TM - 1, axis=0))
    y_conv = cb * (cw_ref[0:1, :] * z_dn + cw_ref[1:2, :] * zz + cw_ref[2:3, :] * z_up)
    br_conv = _dot(y_conv.astype(BF16), wco_ref[...])
    br_attn = _dot(ya_ref[...], wao_ref[...])

    merged = (gt_ref[:, 0:D_MODEL].astype(F32) * br_ssm
              + gt_ref[:, D_MODEL:2 * D_MODEL].astype(F32) * br_conv
              + gt_ref[:, 2 * D_MODEL:3 * D_MODEL].astype(F32) * br_attn)
    mix = _dot(merged.astype(BF16), wo_ref[...])
    m = mod_ref[0]
    xn = x_ref[...] + m[:, 2 * D_MODEL:3 * D_MODEL] * mix
    xo_ref[...] = xn
    h2 = _rms(xn, n2g_ref[...]) * (1.0 + m[:, 4 * D_MODEL:5 * D_MODEL]) + m[:, 3 * D_MODEL:4 * D_MODEL]
    hi = h2.astype(BF16)
    h2_ref[...] = hi
    lo = (h2 - hi.astype(F32)).astype(BF16)
    lg_ref[...] = _dot(hi, wrh_ref[...]) + _dot(lo, wrh_ref[...]) + _dot(hi, wrl_ref[...]) + br_ref[...]


def _merge(x, yf, yb, u_tb, cz, ya, gt, mod_tiles, d_skip, conv_w, wglu, wso, wco, wao, wo, n2g,
           wr_hi, wr_lo, b_r, tri, triu, tiles_per_seq):
    tok = x.shape[0]
    nt = tok // TM
    cur = lambda i: jnp.minimum(i, nt - 1)
    tile = lambda i: (cur(i), 0)
    routed = lambda i: (jnp.maximum(i - 1, 0), 0)
    const = lambda i: (0, 0)
    tb = lambda i: (cur(i) % tiles_per_seq, cur(i) // tiles_per_seq)
    rows8 = TM // 8
    return pl.pallas_call(
        functools.partial(_merge_kernel, tiles_per_seq=tiles_per_seq, n_tiles=nt),
        grid=(nt + 1,),
        in_specs=[
            pl.BlockSpec((TM, D_MODEL), tile),
            pl.BlockSpec((TM, D_SSM), tb),
            pl.BlockSpec((TM, D_SSM), tb),
            pl.BlockSpec((TM, D_SSM), tb),
            pl.BlockSpec((TM, 2 * D_CONV), tile),
            pl.BlockSpec((8, 2 * D_CONV), lambda i: (jnp.maximum(cur(i) * rows8 - 1, 0), 0)),
            pl.BlockSpec((8, 2 * D_CONV), lambda i: (jnp.minimum((cur(i) + 1) * rows8, tok // 8 - 1), 0)),
            pl.BlockSpec((TM, D_Q), tile),
            pl.BlockSpec((TM, N_BRANCH * D_MODEL), tile),
            pl.BlockSpec((1, 1, 6 * D_MODEL), lambda i: (cur(i), 0, 0)),
            pl.BlockSpec((1, D_SSM), const),
            pl.BlockSpec((3, D_CONV), const),
            pl.BlockSpec((D_SSM, D_SSM), const),
            pl.BlockSpec((D_SSM, D_MODEL), const),
            pl.BlockSpec((D_CONV, D_MODEL), const),
            pl.BlockSpec((D_Q, D_MODEL), const),
            pl.BlockSpec((D_MODEL, D_MODEL), const),
            pl.BlockSpec((1, D_MODEL), const),
            pl.BlockSpec((D_MODEL, LANES), const),
            pl.BlockSpec((D_MODEL, LANES), const),
            pl.BlockSpec((1, LANES), const),
            pl.BlockSpec((TM, TM), const),
            pl.BlockSpec((LANES, LANES), const),
        ],
        out_specs=[
            pl.BlockSpec((TM, D_MODEL), tile),
            pl.BlockSpec((TM, D_MODEL), tile),
            pl.BlockSpec((TM, LANES), routed),
            pl.BlockSpec((8, LANES), routed),
        ],
        out_shape=[
            jax.ShapeDtypeStruct((tok, D_MODEL), F32),
            jax.ShapeDtypeStruct((tok, D_MODEL), BF16),
            jax.ShapeDtypeStruct((tok, LANES), F32),
            jax.ShapeDtypeStruct((nt * 8, LANES), F32),
        ],
        scratch_shapes=[pltpu.VMEM((TM, LANES), F32)],
        compiler_params=_cparams(("arbitrary",)),
        name="branch_merge",
    )(x, yf, yb, u_tb, cz, cz, cz, ya, gt, mod_tiles, d_skip, conv_w, wglu, wso, wco, wao, wo, n2g,
      wr_hi, wr_lo, b_r, tri, triu)


def _route_tile(l, tri_ref, triu_ref):
    lane = lax.broadcasted_iota(I32, l.shape, 1)
    vals, idxs, hots = [], [], []
    for _ in range(TOP_K):
        mx = jnp.max(l, axis=-1, keepdims=True)
        idx = jnp.min(jnp.where(l == mx, lane, LANES), axis=-1, keepdims=True)
        hot = lane == idx
        l = jnp.where(hot, -3e38, l)
        vals.append(mx)
        idxs.append(idx)
        hots.append(hot)
    ex = [jnp.exp(v - vals[0]) for v in vals]
    den = ex[0] + ex[1] + ex[2] + ex[3]
    picked = jnp.zeros(l.shape, F32)
    for hot in hots:
        picked = picked + hot.astype(F32)
    cum = _dot(tri_ref[...], picked.astype(BF16))
    cnt = jnp.sum(picked, axis=0, keepdims=True)
    seg = jnp.floor((cnt + (SEG - 1.0)) * (1.0 / SEG)) * SEG
    run_start = _dot(jnp.broadcast_to(seg, (8, LANES)).astype(BF16), triu_ref[...])[0:1]
    pos = cum + run_start
    out = jnp.zeros(l.shape, F32)
    for k in range(TOP_K):
        row = jnp.sum(jnp.where(hots[k], pos, 0.0), axis=-1, keepdims=True)
        out = jnp.where(lane == k, idxs[k].astype(F32), out)
        out = jnp.where(lane == TOP_K + k, row, out)
        out = jnp.where(lane == 2 * TOP_K + k, ex[k] / den, out)
    return out, jnp.broadcast_to(cnt, (8, LANES))


N_COPY_SIZES = int(math.log2(TM // SEG)) + 1


def _for_each_copy(tile, lrow_ref, grow_ref, cnt_ref, n_tiles, fn):
    for c in range(N_COPY_SIZES):
        base = (c * n_tiles + tile) * N_EXPERTS

        def per_copy(j, carry, c=c, base=base):
            fn(pl.multiple_of(lrow_ref[base + j], SEG), pl.multiple_of(grow_ref[base + j], SEG), SEG << c)
            return carry

        lax.fori_loop(0, cnt_ref[c * n_tiles + tile], per_copy, 0)


WAIT_GROUP = 16


def _wait_pieces(n, descriptor):
    def many(g, c):
        descriptor(WAIT_GROUP * SEG).wait()
        return c

    def one(g, c):
        descriptor(SEG).wait()
        return c

    lax.fori_loop(0, lax.shift_right_logical(n, int(math.log2(WAIT_GROUP))), many, 0)
    lax.fori_loop(0, lax.bitwise_and(n, WAIT_GROUP - 1), one, 0)


ROW_CHUNK = 256


def _local_rows(rt, c):
    rows = c * ROW_CHUNK + lax.broadcasted_iota(I32, (TM, ROW_CHUNK), 1)
    return [rows == rt[:, TOP_K + k:TOP_K + k + 1].astype(I32) for k in range(TOP_K)]


def _dispatch_kernel(lrow_ref, grow_ref, cnt_ref, tot_ref, zstart_ref, nu_ref, h_ref, rt_ref, xs_hbm,
                     xl_ref, zero_ref, sem_ref, sem_z, *, n_blocks, n_tiles):
    i = pl.program_id(0)
    slot = i % 2

    @pl.when(i == 0)
    def _():
        zero_ref[...] = jnp.zeros_like(zero_ref)

        def zero_copy(start):
            rows = pl.ds(pl.multiple_of(start, MOE_BLOCK), MOE_BLOCK)
            return pltpu.make_async_copy(zero_ref, xs_hbm.at[rows, :], sem_z)

        def tail_start(b, c):
            zero_copy(b * MOE_BLOCK).start()
            return c

        def tail_wait(b, c):
            zero_copy(b * MOE_BLOCK).wait()
            return c

        for e in range(N_EXPERTS):
            @pl.when(zstart_ref[e] >= 0)
            def _():
                zero_copy(zstart_ref[e]).start()
        lax.fori_loop(nu_ref[0], n_blocks, tail_start, 0)
        for e in range(N_EXPERTS):
            @pl.when(zstart_ref[e] >= 0)
            def _():
                zero_copy(zstart_ref[e]).wait()
        lax.fori_loop(nu_ref[0], n_blocks, tail_wait, 0)

    rt = rt_ref[...]
    h = h_ref[...]
    for c in range(LOCAL_ROWS // ROW_CHUNK):
        place = jnp.zeros((TM, ROW_CHUNK), F32)
        for hit in _local_rows(rt, c):
            place = place + hit.astype(F32)
        xl_ref[slot, pl.ds(c * ROW_CHUNK, ROW_CHUNK), :] = lax.dot_general(
            place.astype(BF16), h, (((0,), (0,)), ((), ())), preferred_element_type=F32)

    def seg_copy(local_row, global_row, s, rows=SEG):
        return pltpu.make_async_copy(xl_ref.at[s, pl.ds(local_row, rows), :],
                                     xs_hbm.at[pl.ds(global_row, rows), :], sem_ref.at[s])

    def wait_tile(tile, s):
        _wait_pieces(tot_ref[tile], lambda rows: seg_copy(0, 0, s, rows))

    _for_each_copy(i, lrow_ref, grow_ref, cnt_ref, n_tiles,
                   lambda lr, gr, rows: seg_copy(lr, gr, slot, rows).start())

    @pl.when(i > 0)
    def _():
        wait_tile(i - 1, 1 - slot)

    @pl.when(i == n_tiles - 1)
    def _():
        wait_tile(i, slot)


def _dispatch(h2, route, seg_meta, zstart, n_used, n_slots):
    tok = h2.shape[0]
    nt = tok // TM
    tile = lambda i, *_: (i, 0)
    return pl.pallas_call(
        functools.partial(_dispatch_kernel, n_blocks=n_slots // MOE_BLOCK, n_tiles=nt),
        grid_spec=pltpu.PrefetchScalarGridSpec(
            num_scalar_prefetch=6,
            grid=(nt,),
            in_specs=[
                pl.BlockSpec((TM, D_MODEL), tile),
                pl.BlockSpec((TM, LANES), tile),
            ],
            out_specs=pl.BlockSpec(memory_space=pl.ANY),
            scratch_shapes=[
                pltpu.VMEM((2, LOCAL_ROWS, D_MODEL), F32),
                pltpu.VMEM((MOE_BLOCK, D_MODEL), F32),
                pltpu.SemaphoreType.DMA((2,)),
                pltpu.SemaphoreType.DMA,
            ],
        ),
        out_shape=jax.ShapeDtypeStruct((n_slots, D_MODEL), F32),
        compiler_params=_cparams(("arbitrary",)),
        name="dispatch",
    )(*seg_meta, zstart, n_used, h2, route)


DEINT = 256


def _expert_kernel(be_ref, nu_ref, x_ref, wgu_ref, bg_ref, bu_ref, wd_ref, bd_ref, sel_ref, y_ref,
                   wg_s, wu_s, wd_s):
    i = pl.program_id(0)
    prev = be_ref[jnp.maximum(i - 1, 0)]

    @pl.when((i == 0) | (be_ref[i] != prev))
    def _():
        for c in range(2 * D_EXPERT // DEINT):
            w = wgu_ref[0, 0, :, c * DEINT:(c + 1) * DEINT].astype(BF16)
            cols = pl.ds(c * (DEINT // 2), DEINT // 2)
            split = _dot(w, sel_ref[...])
            wg_s[:, cols] = split[:, 0:DEINT // 2].astype(BF16)
            wu_s[:, cols] = split[:, DEINT // 2:DEINT].astype(BF16)
        wd_s[...] = wd_ref[0, 0].astype(BF16)

    @pl.when(i < nu_ref[0])
    def _():
        x = x_ref[...].astype(BF16)
        g = _dot(x, wg_s[...]) + bg_ref[0]
        u = _dot(x, wu_s[...]) + bu_ref[0]
        glu = jnp.minimum(g, SWIGLU_LIMIT)
        up = jnp.clip(u, -SWIGLU_LIMIT, SWIGLU_LIMIT)
        act = glu * _sigmoid(SWIGLU_ALPHA * glu) * (up + 1.0)
        y_ref[...] = _dot(act.astype(BF16), wd_s[...]) + bd_ref[0]

    @pl.when(i >= nu_ref[0])
    def _():
        y_ref[...] = jnp.zeros_like(y_ref)


def _experts(xs, block_expert, n_used, layer, wgu_all, bg, bu, wd_all, bd, sel):
    n_slots = xs.shape[0]
    n_blocks = n_slots // MOE_BLOCK
    wmap = lambda i, be, nu: (be[i], 0, 0)
    lwmap = lambda i, be, nu: (layer, be[i], 0, 0)
    const = lambda i, be, nu: (0, 0)
    return pl.pallas_call(
        _expert_kernel,
        grid_spec=pltpu.PrefetchScalarGridSpec(
            num_scalar_prefetch=2,
            grid=(n_blocks,),
            in_specs=[
                pl.BlockSpec((MOE_BLOCK, D_MODEL), lambda i, be, nu: (jnp.minimum(i, nu[0] - 1), 0)),
                pl.BlockSpec((1, 1, D_MODEL, 2 * D_EXPERT), lwmap),
                pl.BlockSpec((1, 1, D_EXPERT), wmap),
                pl.BlockSpec((1, 1, D_EXPERT), wmap),
                pl.BlockSpec((1, 1, D_EXPERT, D_MODEL), lwmap),
                pl.BlockSpec((1, 1, D_MODEL), wmap),
                pl.BlockSpec((DEINT, DEINT), const),
            ],
            out_specs=pl.BlockSpec((MOE_BLOCK, D_MODEL), lambda i, be, nu: (i, 0)),
            scratch_shapes=[
                pltpu.VMEM((D_MODEL, D_EXPERT), BF16),
                pltpu.VMEM((D_MODEL, D_EXPERT), BF16),
                pltpu.VMEM((D_EXPERT, D_MODEL), BF16),
            ],
        ),
        out_shape=jax.ShapeDtypeStruct((n_slots, D_MODEL), F32),
        compiler_params=_cparams(("arbitrary",)),
        name="experts",
    )(block_expert, n_used, xs, wgu_all, bg, bu, wd_all, bd, sel)


def _combine_kernel(lrow_ref, grow_ref, cnt_ref, tot_ref, ys_hbm, x_ref, rt_ref, mod_ref, o_ref, yl_ref, sem_ref,
                    *, n_tiles):
    i = pl.program_id(0)
    slot = i % 2

    def seg_copy(local_row, global_row, s, rows=SEG):
        return pltpu.make_async_copy(ys_hbm.at[pl.ds(global_row, rows), :],
                                     yl_ref.at[s, pl.ds(local_row, rows), :], sem_ref.at[s])

    def fetch_tile(tile, s):
        _for_each_copy(tile, lrow_ref, grow_ref, cnt_ref, n_tiles,
                       lambda lr, gr, rows: seg_copy(lr, gr, s, rows).start())

    @pl.when(i == 0)
    def _():
        yl_ref[...] = jnp.zeros_like(yl_ref)
        fetch_tile(0, 0)

    @pl.when(i + 1 < n_tiles)
    def _():
        fetch_tile(i + 1, 1 - slot)

    _wait_pieces(tot_ref[i], lambda rows: seg_copy(0, 0, slot, rows))

    rt = rt_ref[...]
    y = jnp.zeros((TM, D_MODEL), F32)
    for c in range(LOCAL_ROWS // ROW_CHUNK):
        weights = jnp.zeros((TM, ROW_CHUNK), F32)
        for k, hit in enumerate(_local_rows(rt, c)):
            weights = weights + jnp.where(hit, rt[:, 2 * TOP_K + k:2 * TOP_K + k + 1], 0.0)
        w_hi = weights.astype(BF16)
        w_lo = (weights - w_hi.astype(F32)).astype(BF16)
        yl = yl_ref[slot, pl.ds(c * ROW_CHUNK, ROW_CHUNK), :].astype(BF16)
        y = y + _dot(w_hi, yl) + _dot(w_lo, yl)
    o_ref[...] = (x_ref[...] + mod_ref[0][:, 5 * D_MODEL:6 * D_MODEL] * y).reshape(o_ref.shape)


def _combine(x, ys, route, seg_meta, mod_tiles, out_seq=None):
    tok = x.shape[0]
    nt = tok // TM
    tile = lambda i, *_: (i, 0)
    if out_seq is None:
        out_spec = pl.BlockSpec((TM, D_MODEL), tile)
        out_shape = jax.ShapeDtypeStruct((tok, D_MODEL), F32)
    else:
        n_batch, tps = out_seq
        out_spec = pl.BlockSpec((1, TM, D_MODEL), lambda i, *_: (i // tps, jnp.maximum(i % tps - 1, 0), 0))
        out_shape = jax.ShapeDtypeStruct((n_batch, (tps - 1) * TM, D_MODEL), F32)
    return pl.pallas_call(
        functools.partial(_combine_kernel, n_tiles=nt),
        grid_spec=pltpu.PrefetchScalarGridSpec(
            num_scalar_prefetch=4,
            grid=(nt,),
            in_specs=[
                pl.BlockSpec(memory_space=pl.ANY),
                pl.BlockSpec((TM, D_MODEL), tile),
                pl.BlockSpec((TM, LANES), tile),
                pl.BlockSpec((1, 1, 6 * D_MODEL), lambda i, *_: (i, 0, 0)),
            ],
            out_specs=out_spec,
            scratch_shapes=[
                pltpu.VMEM((2, LOCAL_ROWS, D_MODEL), F32),
                pltpu.SemaphoreType.DMA((2,)),
            ],
        ),
        out_shape=out_shape,
        compiler_params=_cparams(("arbitrary",)),
        name="combine",
    )(*seg_meta, ys, x, route, mod_tiles)


def _rope_tables(t_len):
    n_lat = t_len - N_CTX
    t = np.arange(n_lat)
    n_pairs = HEAD_DIM // 4
    inv_freq = jnp.asarray(ROPE_BASE, F32) ** (-jnp.arange(n_pairs, dtype=F32) / n_pairs)
    row = jnp.asarray(t // GRID_W, F32)
    col = jnp.asarray(t % GRID_W, F32)
    ang = jnp.concatenate([row[:, None] * inv_freq, col[:, None] * inv_freq], axis=-1)
    ang = jnp.concatenate([jnp.zeros((N_CTX, HEAD_DIM // 2), F32), ang], axis=0)
    cos = jnp.tile(jnp.cos(ang), (1, LANES // (HEAD_DIM // 2)))
    sin = jnp.tile(jnp.sin(ang), (1, LANES // (HEAD_DIM // 2)))
    return cos, sin


def _s5_operands(lam_re, lam_im, log_dt, b_re, b_im, c_re, c_im):
    lam = lax.complex(lam_re, lam_im)
    dt = jnp.exp(log_dt)[..., None]
    lam_bar = jnp.exp(lam * dt)
    b_bar = ((lam_bar - 1.0) / lam)[..., None] * lax.complex(b_re, b_im)
    eye = jnp.eye(N_SSM_GROUPS, dtype=F32)

    def in_block(w):
        return jnp.einsum("zgph,gk->zghkp", w, eye).reshape(2, D_SSM, N_STATE)

    def out_block(w):
        return jnp.einsum("zghp,gk->zgpkh", w, eye).reshape(2, N_STATE, D_SSM)

    bd = jnp.concatenate([in_block(b_bar.real), in_block(b_bar.imag)], axis=-1).astype(BF16)
    cd = jnp.concatenate([out_block(c_re), out_block(-c_im)], axis=1).astype(BF16)
    lam_rows = jnp.stack([lam_bar[0].real.reshape(-1), lam_bar[0].imag.reshape(-1),
                          lam_bar[1].real.reshape(-1), lam_bar[1].imag.reshape(-1)])
    return bd, cd, lam_rows


def _moe_constants():
    tri = jnp.asarray(np.tril(np.ones((TM, TM)), -1), BF16)
    triu = jnp.asarray(np.triu(np.ones((LANES, LANES)), 1), BF16)
    pick = np.arange(DEINT)[:, None] == 2 * np.arange(DEINT // 2)[None, :]
    sel = jnp.asarray(np.concatenate([pick, np.roll(pick, 1, axis=0)], axis=1), BF16)
    return tri, triu, sel


def _moe(xs, h2, route, counts, mod_tiles, layer, w_gate_up, b_gate_up, w_down, b_down, consts, out_seq=None):
    sel = consts[2]
    tok = xs.shape[0]
    nt = tok // TM
    n_blocks = (tok * TOP_K + nt * N_EXPERTS * (SEG - 1) + N_EXPERTS * (MOE_BLOCK - 1)) // MOE_BLOCK
    n_slots = n_blocks * MOE_BLOCK

    counts = counts.reshape(nt, 8, LANES)[:, 0, :N_EXPERTS].astype(I32)
    seg = (counts + SEG - 1) // SEG * SEG
    region = jnp.sum(seg, axis=0)
    padded = (region + MOE_BLOCK - 1) // MOE_BLOCK * MOE_BLOCK
    pad_end = jnp.cumsum(padded)
    pad_start = pad_end - padded
    run_global = pad_start[None, :] + jnp.cumsum(seg, axis=0) - seg
    run_local = jnp.cumsum(seg, axis=1) - seg
    pieces = seg // SEG
    bit = jnp.arange(N_COPY_SIZES, dtype=I32)[:, None, None]
    has = (pieces[None] >> bit) & 1
    before = ((pieces[None] >> (bit + 1)) << (bit + 1)) * SEG
    rank = jnp.cumsum(has, axis=2) - has
    pick = (has[..., None] == 1) & (rank[..., None] == jnp.arange(N_EXPERTS, dtype=I32))
    copy_local = jnp.sum(jnp.where(pick, (run_local[None] + before)[..., None], 0), axis=2)
    copy_global = jnp.sum(jnp.where(pick, (run_global[None] + before)[..., None], 0), axis=2)
    seg_meta = (copy_local.reshape(-1).astype(I32), copy_global.reshape(-1).astype(I32),
                jnp.sum(has, axis=2).reshape(-1).astype(I32), jnp.sum(pieces, axis=1).astype(I32))
    n_used = (pad_end[-1] // MOE_BLOCK).astype(I32)
    blk = jnp.minimum(jnp.arange(n_blocks, dtype=I32), n_used - 1) * MOE_BLOCK
    block_expert = jnp.minimum(jnp.sum(blk[:, None] >= pad_end[None, :], axis=1), N_EXPERTS - 1).astype(I32)
    zstart = jnp.where(region > 0, pad_end - MOE_BLOCK, -1).astype(I32)

    x_sorted = _dispatch(h2, route, seg_meta, zstart, n_used.reshape(1), n_slots)
    y_sorted = _experts(x_sorted, block_expert, n_used.reshape(1), layer, w_gate_up,
                        b_gate_up[:, None, 0::2], b_gate_up[:, None, 1::2],
                        w_down, b_down[:, None, :], sel)
    return _combine(xs, y_sorted, route, seg_meta, mod_tiles, out_seq)


def kernel(x, c, ctx, c_ctx, w_mod, b_mod, norm1_g, norm2_g, w_in, ssm_lam_re, ssm_lam_im, ssm_log_dt, ssm_b_re, ssm_b_im, ssm_c_re, ssm_c_im, ssm_d, w_glu, w_ssm_out, conv_w, w_conv_out, q_norm_g, k_norm_g, attn_sinks, w_attn_out, w_o, w_router, b_router, w_gate_up, b_gate_up, w_down, b_down):
    n_batch, n_lat, _ = x.shape
    t_len = N_CTX + n_lat
    tiles_per_seq = t_len // TM
    tok = n_batch * t_len
    nt = tok // TM

    xs = _assemble(ctx, x, tiles_per_seq)

    mod_rows = 8 * ((n_batch + 1 + 7) // 8)
    cvec = jnp.zeros((mod_rows, D_MODEL), F32).at[:n_batch].set(c).at[n_batch].set(c_ctx)
    mod_all = _modulation(cvec, w_mod, b_mod)
    tile_ids = np.arange(nt)
    tile_row = np.where(tile_ids % tiles_per_seq == 0, n_batch, tile_ids // tiles_per_seq)

    cos_t, sin_t = _rope_tables(t_len)
    head_sum = jnp.asarray(np.kron(np.eye(N_Q_HEADS), np.full((HEAD_DIM, HEAD_DIM), 1.0 / HEAD_DIM)), BF16)
    moe_consts = _moe_constants()

    for l in range(DEPTH):
        mod_tiles = mod_all[l][tile_row].reshape(nt, 1, 6 * D_MODEL)
        qg = jnp.tile(q_norm_g[l], N_Q_HEADS).reshape(1, D_Q)
        kg = jnp.tile(k_norm_g[l], N_KV_HEADS).reshape(1, D_KV)
        u_tb, cz, q, kv, gt = _in_proj(xs, mod_tiles, norm1_g[l].reshape(1, D_MODEL), w_in[l].astype(BF16),
                                       cos_t, sin_t, qg, kg, head_sum, n_batch, tiles_per_seq)

        bd, cd, lam_rows = _s5_operands(ssm_lam_re[l], ssm_lam_im[l], ssm_log_dt[l], ssm_b_re[l], ssm_b_im[l],
                                        ssm_c_re[l], ssm_c_im[l])
        yf, yb = _s5(u_tb, bd, cd, lam_rows, n_batch, t_len)
        ya = _attention(q, kv, attn_sinks[l], n_batch, t_len)

        wr = jnp.zeros((D_MODEL, LANES), F32).at[:, :N_EXPERTS].set(w_router[l])
        wr_hi = wr.astype(BF16)
        wr_lo = (wr - wr_hi.astype(F32)).astype(BF16)
        b_r = jnp.full((1, LANES), NEG, F32).at[0, :N_EXPERTS].set(b_router[l])
        xs, h2, route, counts = _merge(
            xs, yf, yb, u_tb,
            cz, ya, gt, mod_tiles, ssm_d[l].reshape(1, D_SSM), conv_w[l],
            w_glu[l].astype(BF16), w_ssm_out[l].astype(BF16), w_conv_out[l].astype(BF16),
            w_attn_out[l].astype(BF16), w_o[l].astype(BF16), norm2_g[l].reshape(1, D_MODEL),
            wr_hi, wr_lo, b_r, moe_consts[0], moe_consts[1], tiles_per_seq)

        out_seq = None
        if l == DEPTH - 1:
            out_seq = (n_batch, tiles_per_seq)
            ctx_tile = jnp.asarray(tile_ids % tiles_per_seq == 0)
            counts = jnp.where(jnp.repeat(ctx_tile, 8)[:, None], 0.0, counts)
            lane = jnp.arange(LANES)[None, :]
            row_lanes = (lane >= TOP_K) & (lane < 2 * TOP_K)
            route = jnp.where(jnp.repeat(ctx_tile, TM)[:, None] & row_lanes, float(LOCAL_ROWS), route)
        xs = _moe(xs, h2, route, counts, mod_tiles, l, w_gate_up, b_gate_up[l], w_down, b_down[l], moe_consts,
                  out_seq)

    return xs
```
